```python
import math
import jax, jax.numpy as jnp
from jax import lax
import numpy as np

D_MODEL = 2048
BATCH = 4
SEQ = 4096
DEPTH = 2

N_HEADS = 16
HEAD_DIM = 128
N_KV = 4
GROUP = N_HEADS // N_KV
CMP_LEN = 32
CMP_STRIDE = 16
CMP_HIDDEN = 256
SEL_LEN = 64
N_SEL = 16
WINDOW = 512
Q_CHUNK = 32
ATTN_DIM = N_HEADS * HEAD_DIM
KV_DIM = N_KV * HEAD_DIM
N_NSA_BRANCH = 3
CONV_DIM = D_MODEL
CONV_WIDTH = 3
D_FF = 4 * D_MODEL
N_MERGE = 2
ROPE_THETA = 10000.0
EPS = 1e-6
SPLITS = [ATTN_DIM, 6 * KV_DIM, N_HEADS * N_NSA_BRANCH, 3 * CONV_DIM, N_MERGE * D_MODEL]
IN_COLS = sum(SPLITS)

kernel_name = "hybrid_nsa_shortconv_sqrelu"


def rmsnorm(x, g):
    xf = x.astype(jnp.float32)
    y = xf * lax.rsqrt(jnp.mean(xf * xf, axis=-1, keepdims=True) + EPS)
    return (y * g.astype(jnp.float32)).astype(x.dtype)


def rope(x, pos):
    half = x.shape[-1] // 2
    inv_freq = jnp.exp(-math.log(ROPE_THETA) * jnp.arange(half, dtype=jnp.float32) / half)
    ang = pos.astype(jnp.float32)[:, None] * inv_freq[None, :]
    cos, sin = jnp.cos(ang), jnp.sin(ang)
    xf = x.astype(jnp.float32)
    x1, x2 = xf[..., :half], xf[..., half:]
    out = jnp.concatenate([x1 * cos - x2 * sin, x2 * cos + x1 * sin], axis=-1)
    return out.astype(x.dtype)


def masked_softmax(s, mask):
    s = jnp.where(mask, s.astype(jnp.float32), -jnp.inf)
    m = jnp.max(s, axis=-1, keepdims=True)
    m = jnp.where(jnp.isfinite(m), m, 0.0)
    p = jnp.exp(s - m)
    d = jnp.sum(p, axis=-1, keepdims=True)
    return p / jnp.where(d > 0, d, 1.0)


def compress(k, pos_emb, w1, w2):
    b, g, s, dh = k.shape
    nc = (s - CMP_LEN) // CMP_STRIDE + 1
    idx = jnp.arange(nc)[:, None] * CMP_STRIDE + jnp.arange(CMP_LEN)[None, :]
    blocks = k[:, :, idx] + pos_emb
    flat = blocks.reshape(b, g, nc, CMP_LEN * dh)
    return jax.nn.silu(flat @ w1) @ w2


def cmp_to_sel_matrix(nc, ns):
    cs = np.arange(nc) * CMP_STRIDE
    ss = np.arange(ns) * SEL_LEN
    ov = np.minimum(cs[:, None] + CMP_LEN, ss[None, :] + SEL_LEN) - np.maximum(cs[:, None], ss[None, :])
    return jnp.asarray(np.clip(ov, 0, None) / CMP_LEN, dtype=jnp.float32)


def nsa_attention(q, kc, vc, ks, vs, kw, vw):
    b, g, r, s, dh = q.shape
    nc = kc.shape[2]
    ns = s // SEL_LEN
    n_sel = min(N_SEL, ns)
    nq = s // Q_CHUNK
    scale = dh ** -0.5
    cmp_end = jnp.arange(nc) * CMP_STRIDE + CMP_LEN - 1
    m_sel = cmp_to_sel_matrix(nc, ns)
    ks_blk = ks.reshape(b, g, ns, SEL_LEN, dh)
    vs_blk = vs.reshape(b, g, ns, SEL_LEN, dh)
    pad = jnp.zeros((b, g, WINDOW, dh), kw.dtype)
    kw_pad = jnp.concatenate([pad, kw], axis=2)
    vw_pad = jnp.concatenate([pad, vw], axis=2)
    bi = jnp.arange(b)[:, None, None, None]
    gi = jnp.arange(g)[None, :, None, None]
    blk_ids = jnp.arange(ns)
    tok_in_blk = jnp.arange(SEL_LEN)
    win_off = jnp.arange(WINDOW + Q_CHUNK)
    q_chunks = jnp.moveaxis(q.reshape(b, g, r, nq, Q_CHUNK, dh), 3, 0)

    def chunk(args):
        qc, c = args
        start = c * Q_CHUNK
        t = start + jnp.arange(Q_CHUNK)
        sc = jnp.einsum('bgrqd,bgnd->bgrqn', qc, kc) * scale
        p_cmp = masked_softmax(sc, cmp_end[None, :] <= t[:, None])
        o_cmp = jnp.einsum('bgrqn,bgnd->bgrqd', p_cmp.astype(vc.dtype), vc)
        imp = jnp.einsum('bgrqn,nm->bgqm', p_cmp, m_sel)
        tb = t // SEL_LEN
        valid = blk_ids[None, :] <= tb[:, None]
        forced = (blk_ids[None, :] == 0) | (blk_ids[None, :] == tb[:, None]) | (blk_ids[None, :] == tb[:, None] - 1)
        imp = jnp.where(valid, jnp.where(forced, jnp.inf, imp), -jnp.inf)
        _, idx = lax.top_k(imp, n_sel)
        kg = ks_blk[bi, gi, idx]
        vg = vs_blk[bi, gi, idx]
        ss_ = jnp.einsum('bgrqd,bgqnld->bgrqnl', qc, kg) * scale
        tok = idx[..., None] * SEL_LEN + tok_in_blk
        smask = (tok <= t[None, None, :, None, None])[:, :, None]
        shp = ss_.shape
        p_sel = masked_softmax(ss_.reshape(b, g, r, Q_CHUNK, -1),
                               smask.reshape(b, g, 1, Q_CHUNK, -1)).reshape(shp)
        o_sel = jnp.einsum('bgrqnl,bgqnld->bgrqd', p_sel.astype(vg.dtype), vg)
        kwc = lax.dynamic_slice_in_dim(kw_pad, start, WINDOW + Q_CHUNK, axis=2)
        vwc = lax.dynamic_slice_in_dim(vw_pad, start, WINDOW + Q_CHUNK, axis=2)
        kp = start - WINDOW + win_off
        wmask = (kp[None, :] <= t[:, None]) & (t[:, None] - kp[None, :] < WINDOW) & (kp[None, :] >= 0)
        sw = jnp.einsum('bgrqd,bgkd->bgrqk', qc, kwc) * scale
        p_win = masked_softmax(sw, wmask)
        o_win = jnp.einsum('bgrqk,bgkd->bgrqd', p_win.astype(vwc.dtype), vwc)
        return o_cmp, o_sel, o_win

    outs = lax.map(chunk, (q_chunks, jnp.arange(nq)))
    return [jnp.moveaxis(o, 0, 3).reshape(b, g, r, s, dh) for o in outs]


def short_conv(u, w):
    c = u.shape[-1]
    return lax.conv_general_dilated(u, w[:, None, :].astype(u.dtype), window_strides=(1,),
                                    padding=[(CONV_WIDTH - 1, 0)],
                                    dimension_numbers=('NWC', 'WIO', 'NWC'),
                                    feature_group_count=c)


def setup_inputs(seed: int = 0) -> dict:
    key = jax.random.key(seed)
    ks = jax.random.split(key, 20)
    f32 = jnp.float32
    nrm = lambda k, shape, fan: jax.random.normal(k, shape, f32) * (fan ** -0.5)
    gain = lambda k, shape: 1.0 + 0.02 * jax.random.normal(k, shape, f32)
    L = DEPTH
    return {
        "x": jax.random.normal(ks[0], (BATCH, SEQ, D_MODEL), f32),
        "norm1_g": gain(ks[1], (L, D_MODEL)),
        "w_in": nrm(ks[2], (L, D_MODEL, IN_COLS), D_MODEL),
        "cmp_pos_k": 0.1 * jax.random.normal(ks[3], (L, CMP_LEN, HEAD_DIM), f32),
        "cmp_w1_k": nrm(ks[4], (L, CMP_LEN * HEAD_DIM, CMP_HIDDEN), CMP_LEN * HEAD_DIM),
        "cmp_w2_k": nrm(ks[5], (L, CMP_HIDDEN, HEAD_DIM), CMP_HIDDEN),
        "cmp_pos_v": 0.1 * jax.random.normal(ks[6], (L, CMP_LEN, HEAD_DIM), f32),
        "cmp_w1_v": nrm(ks[7], (L, CMP_LEN * HEAD_DIM, CMP_HIDDEN), CMP_LEN * HEAD_DIM),
        "cmp_w2_v": nrm(ks[8], (L, CMP_HIDDEN, HEAD_DIM), CMP_HIDDEN),
        "conv_w": nrm(ks[9], (L, CONV_WIDTH, CONV_DIM), CONV_WIDTH),
        "w_attn_proj": nrm(ks[10], (L, ATTN_DIM, D_MODEL), ATTN_DIM),
        "w_conv_out": nrm(ks[11], (L, CONV_DIM, D_MODEL), CONV_DIM),
        "w_o": nrm(ks[12], (L, D_MODEL, D_MODEL), D_MODEL),
        "norm2_g": gain(ks[13], (L, D_MODEL)),
        "w_up": nrm(ks[14], (L, D_MODEL, D_FF), D_MODEL),
        "w_down": nrm(ks[15], (L, D_FF, D_MODEL), D_FF),
        "final_g": gain(ks[16], (D_MODEL,)),
    }


def reference(x, norm1_g, w_in, cmp_pos_k, cmp_w1_k, cmp_w2_k, cmp_pos_v, cmp_w1_v, cmp_w2_v,
              conv_w, w_attn_proj, w_conv_out, w_o, norm2_g, w_up, w_down, final_g):
    b, s, _ = x.shape
    pos = jnp.arange(s)
    nc = (s - CMP_LEN) // CMP_STRIDE + 1
    cmp_pos = jnp.arange(nc) * CMP_STRIDE + CMP_LEN - 1
    offsets = np.cumsum(SPLITS)[:-1].tolist()
    for i in range(DEPTH):
        h = rmsnorm(x, norm1_g[i])
        z = h @ w_in[i]
        q, kv, ng, cv, mg = jnp.split(z, offsets, axis=-1)
        q = rope(q.reshape(b, s, N_KV, GROUP, HEAD_DIM).transpose(0, 2, 3, 1, 4), pos)
        kv = kv.reshape(b, s, 6, N_KV, HEAD_DIM)
        k_cmp, v_cmp, k_sel, v_sel, k_win, v_win = [kv[:, :, j].transpose(0, 2, 1, 3) for j in range(6)]
        kc = rope(compress(k_cmp, cmp_pos_k[i], cmp_w1_k[i], cmp_w2_k[i]), cmp_pos)
        vc = compress(v_cmp, cmp_pos_v[i], cmp_w1_v[i], cmp_w2_v[i])
        o_cmp, o_sel, o_win = nsa_attention(q, kc, vc, rope(k_sel, pos), v_sel, rope(k_win, pos), v_win)
        to_bshd = lambda o: o.transpose(0, 3, 1, 2, 4).reshape(b, s, N_HEADS, HEAD_DIM)
        gb = jax.nn.sigmoid(ng.reshape(b, s, N_HEADS, N_NSA_BRANCH).astype(jnp.float32)).astype(x.dtype)
        o_attn = (gb[..., 0:1] * to_bshd(o_cmp) + gb[..., 1:2] * to_bshd(o_sel)
                  + gb[..., 2:3] * to_bshd(o_win)).reshape(b, s, ATTN_DIM)
        y_attn = o_attn @ w_attn_proj[i]
        x_in, gate_b, gate_c = jnp.split(cv, 3, axis=-1)
        y_conv = (gate_b * short_conv(gate_c * x_in, conv_w[i])) @ w_conv_out[i]
        gm = jax.nn.sigmoid(mg.astype(jnp.float32)).astype(x.dtype)
        g_attn, g_conv = jnp.split(gm, 2, axis=-1)
        x = x + (g_attn * y_attn + g_conv * y_conv) @ w_o[i]
        h2 = rmsnorm(x, norm2_g[i])
        x = x + jnp.square(jax.nn.relu(h2 @ w_up[i])) @ w_down[i]
    return rmsnorm(x, final_g)
```

```python
import functools
import math

import numpy as np
import jax
import jax.numpy as jnp
from jax import lax
from jax.experimental import pallas as pl
from jax.experimental.pallas import tpu as pltpu

N_HEADS = 16
HEAD_DIM = 128
N_KV = 4
GROUP = N_HEADS // N_KV
CMP_LEN = 32
CMP_STRIDE = 16
SEL_LEN = 64
N_SEL = 16
WINDOW = 512
N_NSA_BRANCH = 3
CONV_WIDTH = 3
ROPE_THETA = 10000.0
EPS = 1e-6

MXU_DTYPE = jnp.bfloat16
F32 = jnp.float32
NEG = -1e30
VMEM_LIMIT = 56 * 1024 * 1024
LANES = 128
NT_DIMS = (((1,), (1,)), ((), ()))


def _params(*sem):
    return pltpu.CompilerParams(dimension_semantics=sem, vmem_limit_bytes=VMEM_LIMIT)


def _dot(a, b):
    return jnp.dot(a, b, preferred_element_type=F32)


def _rms(x, g):
    return x * lax.rsqrt(jnp.mean(x * x, axis=-1, keepdims=True) + EPS) * g


def _norm_kernel(x_ref, g_ref, o_ref):
    o_ref[...] = _rms(x_ref[...], g_ref[...]).astype(o_ref.dtype)


def _rmsnorm(x2d, g, tm):
    m, d = x2d.shape
    return pl.pallas_call(
        _norm_kernel,
        grid=(m // tm,),
        in_specs=[pl.BlockSpec((tm, d), lambda i: (i, 0)), pl.BlockSpec((1, d), lambda i: (0, 0))],
        out_specs=pl.BlockSpec((tm, d), lambda i: (i, 0)),
        out_shape=jax.ShapeDtypeStruct((m, d), MXU_DTYPE),
        compiler_params=_params("parallel"),
        name="rmsnorm",
    )(x2d, g.reshape(1, d))


def _qkv_kernel(h_ref, w_ref, cos_ref, sin_ref, o_ref, *, n_rope_tiles, n_q_tiles, scale):
    j = pl.program_id(1)
    acc = _dot(h_ref[...], w_ref[...])
    heads = o_ref.shape[0]

    @pl.when(j < n_rope_tiles)
    def _():
        mult = jnp.where(j < n_q_tiles, scale, 1.0).astype(F32)
        cos = cos_ref[...] * mult
        sin = sin_ref[...] * mult
        for c in range(heads):
            xc = acc[:, c * LANES:(c + 1) * LANES]
            o_ref[c] = (xc * cos + pltpu.roll(xc, HEAD_DIM // 2, 1) * sin).astype(o_ref.dtype)

    @pl.when(j >= n_rope_tiles)
    def _():
        for c in range(heads):
            o_ref[c] = acc[:, c * LANES:(c + 1) * LANES].astype(o_ref.dtype)


def _qkv_proj(h, w, cos, sin, seq, tm, tn, n_rope_tiles, n_q_tiles):
    m, d = h.shape
    n = w.shape[1]
    hpt = tn // LANES
    tps = seq // tm
    kern = functools.partial(_qkv_kernel, n_rope_tiles=n_rope_tiles, n_q_tiles=n_q_tiles,
                             scale=HEAD_DIM ** -0.5)
    return pl.pallas_call(
        kern,
        grid=(m // tm, n // tn),
        in_specs=[pl.BlockSpec((tm, d), lambda i, j: (i, 0)),
                  pl.BlockSpec((d, tn), lambda i, j: (0, j)),
                  pl.BlockSpec((tm, LANES), lambda i, j: (i % tps, 0)),
                  pl.BlockSpec((tm, LANES), lambda i, j: (i % tps, 0))],
        out_specs=pl.BlockSpec((hpt, tm, LANES), lambda i, j: (j, i, 0)),
        out_shape=jax.ShapeDtypeStruct((n // LANES, m, LANES), MXU_DTYPE),
        compiler_params=_params("parallel", "arbitrary"),
        name="qkv_proj",
    )(h, w, cos, sin)


def _gate_kernel(h_ref, w_ref, o_ref):
    acc = _dot(h_ref[...], w_ref[...])
    for c in range(o_ref.shape[0]):
        o_ref[c] = jax.nn.sigmoid(acc[:, c * LANES:(c + 1) * LANES])


def _gate_proj(h, w, tm):
    m, d = h.shape
    n = w.shape[1]
    return pl.pallas_call(
        _gate_kernel,
        grid=(m // tm,),
        in_specs=[pl.BlockSpec((tm, d), lambda i: (i, 0)), pl.BlockSpec((d, n), lambda i: (0, 0))],
        out_specs=pl.BlockSpec((n // LANES, tm, LANES), lambda i: (0, i, 0)),
        out_shape=jax.ShapeDtypeStruct((n // LANES, m, LANES), F32),
        compiler_params=_params("parallel"),
        name="nsa_gate_proj",
    )(h, w)


def _compress_kernel(ck_ref, cv_ref, w1k_ref, w2k_ref, pk_ref, w1v_ref, w2v_ref, pv_ref,
                     cos_ref, sin_ref, kc_ref, vc_ref):
    nch = ck_ref.shape[0]
    half = w1k_ref.shape[0] // 2

    def phi(c_ref, w1_ref, w2_ref, p_ref):
        c = c_ref[...]
        first = _dot(c, w1_ref[:half, :])
        second = _dot(c, w1_ref[half:, :])
        pos = _dot(p_ref[...], w1_ref[...])[0:1, :]
        hid = first + pltpu.roll(second, nch - 1, 0) + pos
        act = hid * jax.nn.sigmoid(hid)
        return _dot(act.astype(MXU_DTYPE), w2_ref[...])

    kc = phi(ck_ref, w1k_ref, w2k_ref, pk_ref)
    kc = kc * cos_ref[...] + pltpu.roll(kc, HEAD_DIM // 2, 1) * sin_ref[...]
    kc_ref[...] = kc.astype(kc_ref.dtype)
    vc_ref[...] = phi(cv_ref, w1v_ref, w2v_ref, pv_ref).astype(vc_ref.dtype)


def _compress(chunks, kbase, vbase, w1k, w2k, pk, w1v, w2v, pv, cos_c, sin_c, batch):
    _, _, nch, cw = chunks.shape
    const = lambda a: pl.BlockSpec(a.shape, lambda b, g: (0,) * a.ndim)
    out_spec = pl.BlockSpec((None, None, nch, HEAD_DIM), lambda b, g: (b, g, 0, 0))
    out_sds = jax.ShapeDtypeStruct((batch, N_KV, nch, HEAD_DIM), MXU_DTYPE)
    return pl.pallas_call(
        _compress_kernel,
        grid=(batch, N_KV),
        in_specs=[pl.BlockSpec((None, None, nch, cw), lambda b, g: (kbase + g, b, 0, 0)),
                  pl.BlockSpec((None, None, nch, cw), lambda b, g: (vbase + g, b, 0, 0)),
                  const(w1k), const(w2k), const(pk), const(w1v), const(w2v), const(pv),
                  const(cos_c), const(sin_c)],
        out_specs=[out_spec, out_spec],
        out_shape=[out_sds, out_sds],
        compiler_params=_params("parallel", "parallel"),
        name="compress",
    )(chunks, chunks, w1k, w2k, pk, w1v, w2v, pv, cos_c, sin_c)


def _attn_kernel(q_ref, kc_ref, vc_ref, ks_ref, vs_ref, kw_ref, vw_ref, gate_ref, mselt_ref,
                 o_ref, *, tq, tk):
    i = pl.program_id(2)
    q0 = i * tq
    r = GROUP
    q = q_ref[...].reshape(r * tq, HEAD_DIM)
    t_col = q0 + lax.broadcasted_iota(jnp.int32, (tq, 1), 0)

    def softmax_parts(s, mask):
        sm = jnp.where(mask[None], s, NEG)
        m = jnp.max(sm, axis=-1, keepdims=True)
        p = jnp.where(mask[None], jnp.exp(sm - m), 0.0)
        d = jnp.sum(p, axis=-1, keepdims=True)
        return p, 1.0 / jnp.where(d > 0, d, 1.0)

    kc = kc_ref[...]
    ncp = kc.shape[0]
    s = lax.dot_general(q, kc, NT_DIMS, preferred_element_type=F32).reshape(r, tq, ncp)
    cmp_end = lax.broadcasted_iota(jnp.int32, (tq, ncp), 1) * CMP_STRIDE + (CMP_LEN - 1)
    p, dinv = softmax_parts(s, cmp_end <= t_col)
    p = p * dinv
    o_cmp = _dot(p.reshape(r * tq, ncp).astype(MXU_DTYPE), vc_ref[...]).reshape(r, tq, HEAD_DIM)

    p_sum = jnp.sum(p, axis=0)
    mselt = mselt_ref[...]
    ns = mselt.shape[0]
    imp = jnp.zeros((ns, tq), F32)
    rem = p_sum
    for _ in range(3):
        piece = rem.astype(MXU_DTYPE)
        imp = imp + lax.dot_general(mselt, piece, NT_DIMS, preferred_element_type=F32)
        rem = rem - piece.astype(F32)

    blk = lax.broadcasted_iota(jnp.int32, (ns, tq), 0)
    tb = (q0 + lax.broadcasted_iota(jnp.int32, (ns, tq), 1)) // SEL_LEN
    valid = blk <= tb
    forced = (blk == 0) | (blk == tb) | (blk == tb - 1)
    imp = jnp.where(valid, jnp.where(forced, jnp.inf, imp), -jnp.inf)
    rank = jnp.zeros((ns, tq), jnp.int32)
    for mp in range(ns):
        row = imp[mp:mp + 1, :]
        ahead = (row > imp) | ((row == imp) & (blk > mp))
        rank = rank + jnp.where(ahead, 1, 0)
    sel_t = jnp.where(rank < min(N_SEL, ns), 1.0, 0.0)
    sel = sel_t.T.astype(MXU_DTYPE)

    n_tiles = (q0 + tq - 1) // tk + 1

    def sel_body(j, carry):
        m_i, l_i, acc = carry
        k0 = pl.multiple_of(j * tk, tk)
        k = ks_ref[pl.ds(k0, tk), :]
        v = vs_ref[pl.ds(k0, tk), :]
        s = lax.dot_general(q, k, NT_DIMS, preferred_element_type=F32).reshape(r, tq, tk)
        key_blk = (k0 + lax.broadcasted_iota(jnp.int32, (ns, tk), 1)) // SEL_LEN
        expand = jnp.where(key_blk == lax.broadcasted_iota(jnp.int32, (ns, tk), 0), 1.0, 0.0)
        picked = _dot(sel, expand.astype(MXU_DTYPE))
        kpos = k0 + lax.broadcasted_iota(jnp.int32, (tq, tk), 1)
        mask = (picked > 0.5) & (kpos <= t_col)
        sm = jnp.where(mask[None], s, NEG)
        m_new = jnp.maximum(m_i, jnp.max(sm, axis=-1, keepdims=True))
        alpha = jnp.exp(m_i - m_new)
        p = jnp.exp(sm - m_new)
        l_new = alpha * l_i + jnp.sum(p, axis=-1, keepdims=True)
        pv = _dot(p.reshape(r * tq, tk).astype(MXU_DTYPE), v).reshape(r, tq, HEAD_DIM)
        return m_new, l_new, alpha * acc + pv

    init = (jnp.full((r, tq, 1), NEG, F32), jnp.zeros((r, tq, 1), F32),
            jnp.zeros((r, tq, HEAD_DIM), F32))
    _, l_s, acc_s = lax.fori_loop(0, n_tiles, sel_body, init)
    o_sel = acc_s * (1.0 / l_s)

    wlen = WINDOW + tq
    w0 = pl.multiple_of(jnp.maximum(q0 - WINDOW, 0), tq)
    kw = kw_ref[pl.ds(w0, wlen), :]
    vw = vw_ref[pl.ds(w0, wlen), :]
    s = lax.dot_general(q, kw, NT_DIMS, preferred_element_type=F32).reshape(r, tq, wlen)
    kpos = w0 + lax.broadcasted_iota(jnp.int32, (tq, wlen), 1)
    p, dinv = softmax_parts(s, (kpos <= t_col) & (t_col - kpos < WINDOW))
    o_win = _dot(p.reshape(r * tq, wlen).astype(MXU_DTYPE), vw).reshape(r, tq, HEAD_DIM) * dinv

    gates = gate_ref[...]
    for h in range(r):
        c = h * N_NSA_BRANCH
        o = (gates[:, c:c + 1] * o_cmp[h] + gates[:, c + 1:c + 2] * o_sel[h]
             + gates[:, c + 2:c + 3] * o_win[h])
        o_ref[:, h * HEAD_DIM:(h + 1) * HEAD_DIM] = o.astype(o_ref.dtype)


def _attention(qkv, kc, vc, gates, mselt, batch, seq, bases, tq, tk):
    nq = seq // tq
    ncp = kc.shape[2]
    ns = mselt.shape[0]
    qb, ksb, vsb, kwb, vwb = bases
    full = lambda base: pl.BlockSpec((None, seq, HEAD_DIM), lambda b, g, i: (base + g, b, 0))
    cmp_spec = pl.BlockSpec((None, None, ncp, HEAD_DIM), lambda b, g, i: (b, g, 0, 0))
    return pl.pallas_call(
        functools.partial(_attn_kernel, tq=tq, tk=tk),
        grid=(batch, N_KV, nq),
        in_specs=[pl.BlockSpec((GROUP, tq, HEAD_DIM), lambda b, g, i: (qb // GROUP + g, b * nq + i, 0)),
                  cmp_spec, cmp_spec, full(ksb), full(vsb), full(kwb), full(vwb),
                  pl.BlockSpec((None, tq, LANES), lambda b, g, i: (g, b * nq + i, 0)),
                  pl.BlockSpec((ns, ncp), lambda b, g, i: (0, 0))],
        out_specs=pl.BlockSpec((tq, GROUP * HEAD_DIM), lambda b, g, i: (b * nq + i, g)),
        out_shape=jax.ShapeDtypeStruct((batch * seq, N_HEADS * HEAD_DIM), MXU_DTYPE),
        compiler_params=_params("parallel", "parallel", "arbitrary"),
        name="nsa_attention",
    )(qkv, kc, vc, qkv, qkv, qkv, qkv, gates, mselt)


def _conv_kernel(h_ref, wx_ref, wb_ref, wc_ref, cw_ref, o_ref, ubuf, *, tiles_per_seq):
    i = pl.program_id(1)
    tm = h_ref.shape[0]
    h = h_ref[...]
    x_in = _dot(h, wx_ref[...])
    gate_b = _dot(h, wb_ref[...])
    gate_c = _dot(h, wc_ref[...])

    @pl.when(i % tiles_per_seq == 0)
    def _():
        ubuf[0:8, :] = jnp.zeros((8, ubuf.shape[1]), F32)

    ubuf[8:tm + 8, :] = gate_c * x_in
    w = cw_ref[...]
    conv = (w[2:3, :] * ubuf[8:tm + 8, :] + w[1:2, :] * ubuf[7:tm + 7, :]
            + w[0:1, :] * ubuf[6:tm + 6, :])
    o_ref[...] = (gate_b * conv).astype(o_ref.dtype)
    ubuf[0:8, :] = ubuf[tm:tm + 8, :]


def _conv_mixer(h, wx, wb, wc, cw, seq, tm, tn):
    m, d = h.shape
    n = wx.shape[1]
    wspec = pl.BlockSpec((d, tn), lambda j, i: (0, j))
    return pl.pallas_call(
        functools.partial(_conv_kernel, tiles_per_seq=seq // tm),
        grid=(n // tn, m // tm),
        in_specs=[pl.BlockSpec((tm, d), lambda j, i: (i, 0)), wspec, wspec, wspec,
                  pl.BlockSpec((8, tn), lambda j, i: (0, j))],
        out_specs=pl.BlockSpec((tm, tn), lambda j, i: (i, j)),
        out_shape=jax.ShapeDtypeStruct((m, n), MXU_DTYPE),
        scratch_shapes=[pltpu.VMEM((tm + 8, tn), F32)],
        compiler_params=_params("parallel", "arbitrary"),
        name="conv_mixer",
    )(h, wx, wb, wc, cw)


def _merge_kernel(oa_ref, v_ref, h_ref, wap_ref, wco_ref, wga_ref, wgc_ref, o_ref):
    h = h_ref[...]
    y_attn = _dot(oa_ref[...], wap_ref[...])
    y_conv = _dot(v_ref[...], wco_ref[...])
    g_attn = jax.nn.sigmoid(_dot(h, wga_ref[...]))
    g_conv = jax.nn.sigmoid(_dot(h, wgc_ref[...]))
    o_ref[...] = (g_attn * y_attn + g_conv * y_conv).astype(o_ref.dtype)


def _merge(oa, v, h, wap, wco, wga, wgc, tm, tn):
    m, d = h.shape
    n = wap.shape[1]
    aspec = pl.BlockSpec((tm, d), lambda i, j: (i, 0))
    wspec = pl.BlockSpec((d, tn), lambda i, j: (0, j))
    return pl.pallas_call(
        _merge_kernel,
        grid=(m // tm, n // tn),
        in_specs=[aspec, aspec, aspec, wspec, wspec, wspec, wspec],
        out_specs=pl.BlockSpec((tm, tn), lambda i, j: (i, j)),
        out_shape=jax.ShapeDtypeStruct((m, n), MXU_DTYPE),
        compiler_params=_params("parallel", "arbitrary"),
        name="gated_merge",
    )(oa, v, h, wap, wco, wga, wgc)


def _outproj_kernel(a_ref, w_ref, x_ref, g_ref, xo_ref, ho_ref):
    x = x_ref[...] + _dot(a_ref[...], w_ref[...])
    xo_ref[...] = x
    ho_ref[...] = _rms(x, g_ref[...]).astype(ho_ref.dtype)


def _outproj(a, w, x, g, tm):
    m, d = x.shape
    row = pl.BlockSpec((tm, d), lambda i: (i, 0))
    return pl.pallas_call(
        _outproj_kernel,
        grid=(m // tm,),
        in_specs=[row, pl.BlockSpec((d, d), lambda i: (0, 0)), row,
                  pl.BlockSpec((1, d), lambda i: (0, 0))],
        out_specs=[row, row],
        out_shape=[jax.ShapeDtypeStruct((m, d), F32), jax.ShapeDtypeStruct((m, d), MXU_DTYPE)],
        compiler_params=_params("parallel"),
        name="out_proj",
    )(a, w, x, g.reshape(1, d))


def _mlp_kernel(h_ref, wu_ref, wd_ref, x_ref, g_ref, *refs, last):
    acc = refs[-1]
    f = pl.program_id(1)

    @pl.when(f == 0)
    def _():
        acc[...] = x_ref[...]

    a = jnp.maximum(_dot(h_ref[...], wu_ref[...]), 0.0)
    acc[...] += _dot((a * a).astype(MXU_DTYPE), wd_ref[...])

    @pl.when(f == pl.num_programs(1) - 1)
    def _():
        x = acc[...]
        normed = _rms(x, g_ref[...])
        if last:
            refs[0][...] = normed
        else:
            refs[0][...] = x
            refs[1][...] = normed.astype(refs[1].dtype)


def _mlp(h, wu, wd, x, g, tm, tf, last):
    m, d = x.shape
    ff = wu.shape[1]
    row = pl.BlockSpec((tm, d), lambda i, f: (i, 0))
    if last:
        out_specs, out_shape = [row], [jax.ShapeDtypeStruct((m, d), F32)]
    else:
        out_specs = [row, row]
        out_shape = [jax.ShapeDtypeStruct((m, d), F32), jax.ShapeDtypeStruct((m, d), MXU_DTYPE)]
    return pl.pallas_call(
        functools.partial(_mlp_kernel, last=last),
        grid=(m // tm, ff // tf),
        in_specs=[row, pl.BlockSpec((d, tf), lambda i, f: (0, f)),
                  pl.BlockSpec((tf, d), lambda i, f: (f, 0)), row,
                  pl.BlockSpec((1, d), lambda i, f: (0, 0))],
        out_specs=out_specs,
        out_shape=out_shape,
        scratch_shapes=[pltpu.VMEM((tm, d), F32)],
        compiler_params=_params("parallel", "arbitrary"),
        name="relu2_mlp",
    )(h, wu, wd, x, g.reshape(1, d))


def _rope_tables(pos):
    half = HEAD_DIM // 2
    inv_freq = jnp.exp(-math.log(ROPE_THETA) * jnp.arange(half, dtype=F32) / half)
    ang = pos.astype(F32)[:, None] * inv_freq[None, :]
    cos, sin = jnp.cos(ang), jnp.sin(ang)
    return jnp.concatenate([cos, cos], axis=-1), jnp.concatenate([-sin, sin], axis=-1)


def _cmp_to_sel_t(ncp, ns):
    nc = ncp - 1
    cs = np.arange(nc) * CMP_STRIDE
    ss = np.arange(ns) * SEL_LEN
    ov = np.minimum(cs[:, None] + CMP_LEN, ss[None, :] + SEL_LEN) - np.maximum(cs[:, None], ss[None, :])
    m = np.zeros((ncp, ns), np.float32)
    m[:nc] = np.clip(ov, 0, None) / CMP_LEN
    return jnp.asarray(m.T, dtype=MXU_DTYPE)


def kernel(x, norm1_g, w_in, cmp_pos_k, cmp_w1_k, cmp_w2_k, cmp_pos_v, cmp_w1_v, cmp_w2_v,
           conv_w, w_attn_proj, w_conv_out, w_o, norm2_g, w_up, w_down, final_g):
    batch, seq, d = x.shape
    depth = w_in.shape[0]
    m = batch * seq
    attn_dim = N_HEADS * HEAD_DIM
    kv_dim = N_KV * HEAD_DIM
    n_gate = N_HEADS * N_NSA_BRANCH
    ncp = seq // CMP_STRIDE
    ns = seq // SEL_LEN
    cast = lambda a: a.astype(MXU_DTYPE)

    tm = min(1024, seq)
    tm_small = min(512, seq)
    tq = 128
    tk = 512

    cos_t, sin_t = _rope_tables(jnp.arange(seq))
    cos_c, sin_c = _rope_tables(jnp.arange(ncp) * CMP_STRIDE + CMP_LEN - 1)
    mselt = _cmp_to_sel_t(ncp, ns)

    q_base, ks_base, kw_base, kc_base, vc_base, vs_base, vw_base = 0, 16, 20, 24, 28, 32, 36

    x2 = x.reshape(m, d)
    h = _rmsnorm(x2, norm1_g[0], tm)
    out = None
    for l in range(depth):
        wl = w_in[l]
        o_kv = attn_dim
        kv = [wl[:, o_kv + j * kv_dim:o_kv + (j + 1) * kv_dim] for j in range(6)]
        w_qkv = cast(jnp.concatenate([wl[:, :attn_dim], kv[2], kv[4], kv[0], kv[1], kv[3], kv[5]], axis=1))
        o_ng = o_kv + 6 * kv_dim
        w_ng = wl[:, o_ng:o_ng + n_gate].reshape(d, N_KV, GROUP * N_NSA_BRANCH)
        w_ng = cast(jnp.pad(w_ng, ((0, 0), (0, 0), (0, LANES - GROUP * N_NSA_BRANCH))).reshape(d, N_KV * LANES))
        o_cv = o_ng + n_gate
        w_x, w_b, w_c = [cast(wl[:, o_cv + j * d:o_cv + (j + 1) * d]) for j in range(3)]
        o_mg = o_cv + 3 * d
        w_ga, w_gc = [cast(wl[:, o_mg + j * d:o_mg + (j + 1) * d]) for j in range(2)]

        qkv = _qkv_proj(h, w_qkv, cos_t, sin_t, seq, tm, 512, n_rope_tiles=6, n_q_tiles=4)
        gates = _gate_proj(h, w_ng, tm)

        chunks = qkv[kc_base:kc_base + 2 * N_KV].reshape(2 * N_KV, batch, ncp, CMP_STRIDE * HEAD_DIM)
        pad_pos = lambda p: cast(jnp.pad(p.reshape(1, CMP_LEN * HEAD_DIM), ((0, 7), (0, 0))))
        kc, vc = _compress(chunks, 0, N_KV, cast(cmp_w1_k[l]), cast(cmp_w2_k[l]), pad_pos(cmp_pos_k[l]),
                           cast(cmp_w1_v[l]), cast(cmp_w2_v[l]), pad_pos(cmp_pos_v[l]), cos_c, sin_c, batch)

        o_attn = _attention(qkv, kc, vc, gates, mselt, batch, seq,
                            (q_base, ks_base, vs_base, kw_base, vw_base), tq, tk)

        cw = jnp.pad(conv_w[l], ((0, 8 - CONV_WIDTH), (0, 0)))
        v_conv = _conv_mixer(h, w_x, w_b, w_c, cw, seq, tm_small, 512)

        merged = _merge(o_attn, v_conv, h, cast(w_attn_proj[l]), cast(w_conv_out[l]), w_ga, w_gc,
                        tm_small, 512)
        x2, h2 = _outproj(merged, cast(w_o[l]), x2, norm2_g[l], tm_small)

        last = l == depth - 1
        g_next = final_g if last else norm1_g[l + 1]
        res = _mlp(h2, cast(w_up[l]), cast(w_down[l]), x2, g_next, tm_small, 512, last)
        if last:
            out = res[0]
        else:
            x2, h = res
    return out.reshape(batch, seq, d)
```

```python
import functools
import math

import numpy as np
import jax
import jax.numpy as jnp
from jax import lax
from jax.experimental import pallas as pl
from jax.experimental.pallas import tpu as pltpu

N_HEADS = 16
HEAD_DIM = 128
N_KV = 4
GROUP = N_HEADS // N_KV
CMP_LEN = 32
CMP_STRIDE = 16
SEL_LEN = 64
N_SEL = 16
WINDOW = 512
N_NSA_BRANCH = 3
CONV_WIDTH = 3
ROPE_THETA = 10000.0
EPS = 1e-6

MXU_DTYPE = jnp.bfloat16
F32 = jnp.float32
NEG = -1e30
VMEM_LIMIT = 56 * 1024 * 1024
LANES = 128
NT_DIMS = (((1,), (1,)), ((), ()))


def _params(*sem):
    return pltpu.CompilerParams(dimension_semantics=sem, vmem_limit_bytes=VMEM_LIMIT)


def _dot(a, b):
    return jnp.dot(a, b, preferred_element_type=F32)


def _rms(x, g):
    return x * lax.rsqrt(jnp.mean(x * x, axis=-1, keepdims=True) + EPS) * g


def _norm_kernel(x_ref, g_ref, o_ref):
    o_ref[...] = _rms(x_ref[...], g_ref[...]).astype(o_ref.dtype)


def _rmsnorm(x2d, g, tm):
    m, d = x2d.shape
    return pl.pallas_call(
        _norm_kernel,
        grid=(m // tm,),
        in_specs=[pl.BlockSpec((tm, d), lambda i: (i, 0)), pl.BlockSpec((1, d), lambda i: (0, 0))],
        out_specs=pl.BlockSpec((tm, d), lambda i: (i, 0)),
        out_shape=jax.ShapeDtypeStruct((m, d), MXU_DTYPE),
        compiler_params=_params("parallel"),
        name="rmsnorm",
    )(x2d, g.reshape(1, d))


def _qkv_kernel(h_ref, w_ref, cos_ref, sin_ref, o_ref, *, n_rope_tiles, n_q_tiles, q_scale):
    j = pl.program_id(1)
    acc = _dot(h_ref[...], w_ref[...])
    heads = o_ref.shape[0]

    @pl.when(j < n_rope_tiles)
    def _():
        mult = jnp.where(j < n_q_tiles, q_scale, 1.0).astype(F32)
        cos = cos_ref[...] * mult
        sin = sin_ref[...] * mult
        for c in range(heads):
            xc = acc[:, c * LANES:(c + 1) * LANES]
            o_ref[c] = (xc * cos + pltpu.roll(xc, HEAD_DIM // 2, 1) * sin).astype(o_ref.dtype)

    @pl.when(j >= n_rope_tiles)
    def _():
        for c in range(heads):
            o_ref[c] = acc[:, c * LANES:(c + 1) * LANES].astype(o_ref.dtype)


def _qkv_proj(h, w, cos, sin, seq, tm, tn, n_rope_tiles, n_q_tiles):
    m, d = h.shape
    n = w.shape[1]
    hpt = tn // LANES
    tps = seq // tm
    kern = functools.partial(_qkv_kernel, n_rope_tiles=n_rope_tiles, n_q_tiles=n_q_tiles,
                             q_scale=HEAD_DIM ** -0.5 * math.log2(math.e))
    return pl.pallas_call(
        kern,
        grid=(m // tm, n // tn),
        in_specs=[pl.BlockSpec((tm, d), lambda i, j: (i, 0)),
                  pl.BlockSpec((d, tn), lambda i, j: (0, j)),
                  pl.BlockSpec((tm, LANES), lambda i, j: (i % tps, 0)),
                  pl.BlockSpec((tm, LANES), lambda i, j: (i % tps, 0))],
        out_specs=pl.BlockSpec((hpt, tm, LANES), lambda i, j: (j, i, 0)),
        out_shape=jax.ShapeDtypeStruct((n // LANES, m, LANES), MXU_DTYPE),
        compiler_params=_params("parallel", "arbitrary"),
        name="qkv_proj",
    )(h, w, cos, sin)


def _gate_kernel(h_ref, w_ref, o_ref):
    acc = _dot(h_ref[...], w_ref[...])
    for c in range(o_ref.shape[0]):
        o_ref[c] = jax.nn.sigmoid(acc[:, c * LANES:(c + 1) * LANES])


def _gate_proj(h, w, tm):
    m, d = h.shape
    n = w.shape[1]
    return pl.pallas_call(
        _gate_kernel,
        grid=(m // tm,),
        in_specs=[pl.BlockSpec((tm, d), lambda i: (i, 0)), pl.BlockSpec((d, n), lambda i: (0, 0))],
        out_specs=pl.BlockSpec((n // LANES, tm, LANES), lambda i: (0, i, 0)),
        out_shape=jax.ShapeDtypeStruct((n // LANES, m, LANES), F32),
        compiler_params=_params("parallel"),
        name="nsa_gate_proj",
    )(h, w)


def _compress_kernel(ck_ref, cv_ref, w1k_ref, w2k_ref, pk_ref, w1v_ref, w2v_ref, pv_ref,
                     cos_ref, sin_ref, kc_ref, vc_ref):
    nch = ck_ref.shape[0]
    half = w1k_ref.shape[0] // 2

    def phi(c_ref, w1_ref, w2_ref, p_ref):
        c = c_ref[...]
        first = _dot(c, w1_ref[:half, :])
        second = _dot(c, w1_ref[half:, :])
        pos = _dot(p_ref[...], w1_ref[...])[0:1, :]
        hid = first + pltpu.roll(second, nch - 1, 0) + pos
        act = hid * jax.nn.sigmoid(hid)
        return _dot(act.astype(MXU_DTYPE), w2_ref[...])

    kc = phi(ck_ref, w1k_ref, w2k_ref, pk_ref)
    kc = kc * cos_ref[...] + pltpu.roll(kc, HEAD_DIM // 2, 1) * sin_ref[...]
    kc_ref[...] = kc.astype(kc_ref.dtype)
    vc_ref[...] = phi(cv_ref, w1v_ref, w2v_ref, pv_ref).astype(vc_ref.dtype)


def _compress(chunks, kbase, vbase, w1k, w2k, pk, w1v, w2v, pv, cos_c, sin_c, batch):
    _, _, nch, cw = chunks.shape
    const = lambda a: pl.BlockSpec(a.shape, lambda b, g: (0,) * a.ndim)
    out_spec = pl.BlockSpec((None, None, nch, HEAD_DIM), lambda b, g: (b, g, 0, 0))
    out_sds = jax.ShapeDtypeStruct((batch, N_KV, nch, HEAD_DIM), MXU_DTYPE)
    return pl.pallas_call(
        _compress_kernel,
        grid=(batch, N_KV),
        in_specs=[pl.BlockSpec((None, None, nch, cw), lambda b, g: (kbase + g, b, 0, 0)),
                  pl.BlockSpec((None, None, nch, cw), lambda b, g: (vbase + g, b, 0, 0)),
                  const(w1k), const(w2k), const(pk), const(w1v), const(w2v), const(pv),
                  const(cos_c), const(sin_c)],
        out_specs=[out_spec, out_spec],
        out_shape=[out_sds, out_sds],
        compiler_params=_params("parallel", "parallel"),
        name="compress",
    )(chunks, chunks, w1k, w2k, pk, w1v, w2v, pv, cos_c, sin_c)


def _attn_kernel(q_ref, kc_ref, vc_ref, ksx_ref, vsx_ref, kw_ref, vwx_ref, gate_ref, mselt_ref,
                 o_ref, qx_scr, s_scr, p_scr, bias_scr, m_scr, a_scr, acc_scr, psum_scr,
                 out_scr, *, tq, tk, rb):
    i = pl.program_id(2)
    q0 = pl.multiple_of(i * tq, tq)
    r = GROUP
    rows = r * tq
    ncp = kc_ref.shape[0]
    ns = mselt_ref.shape[0]
    wlen = WINDOW + tq
    n_chunks = rows // rb
    t_col = q0 + lax.broadcasted_iota(jnp.int32, (tq, 1), 0)
    q = q_ref[...].reshape(rows, HEAD_DIM)
    gates = gate_ref[...]
    head = lambda h: slice(h * tq, (h + 1) * tq)
    gate = lambda h, br: gates[:, h * N_NSA_BRANCH + br:h * N_NSA_BRANCH + br + 1]

    def load_scores(rs, bs, width, biased):
        xs = [s_scr[rs, kk * LANES:(kk + 1) * LANES] for kk in range(width // LANES)]
        if biased:
            xs = [x + bias_scr[bs, kk * LANES:(kk + 1) * LANES] for kk, x in enumerate(xs)]
        return xs

    def softmax_rows(width, chunks, first, biased, cmp):
        def slices(c):
            r0 = c * rb
            return slice(r0, r0 + rb), slice(r0 % tq, r0 % tq + rb)

        for c in chunks:
            rs, bs = slices(c)
            xs = load_scores(rs, bs, width, biased)
            mx = jnp.max(functools.reduce(jnp.maximum, xs), axis=-1, keepdims=True)
            if first:
                m_scr[rs, :] = jnp.broadcast_to(mx, (rb, LANES))
            else:
                m_old = m_scr[rs, :]
                m_new = jnp.maximum(m_old, mx)
                a_scr[rs, :] = jnp.exp2(m_old - m_new)
                m_scr[rs, :] = m_new

        for c in chunks:
            rs, bs = slices(c)
            xs = load_scores(rs, bs, width, biased)
            m = m_scr[rs, :]
            ps = [jnp.exp2(x - m) for x in xs]
            if cmp:
                ps = [jnp.where(x > 0.5 * NEG, p, 0.0) for x, p in zip(xs, ps)]
                lsum = jnp.sum(functools.reduce(jnp.add, ps), axis=-1, keepdims=True)
                inv = 1.0 / jnp.where(lsum > 0.0, lsum, 1.0)
                ps = [p * inv for p in ps]
                for kk, p in enumerate(ps):
                    psum_scr[bs, kk * LANES:(kk + 1) * LANES] += p
            elif not first:
                a = a_scr[rs, :]
                acc_scr[rs, :LANES] = a * acc_scr[rs, :LANES]
                acc_scr[rs, LANES:] = a * acc_scr[rs, LANES:]
            for kk, p in enumerate(ps):
                p_scr[rs, kk * LANES:(kk + 1) * LANES] = p.astype(p_scr.dtype)

    def branch_tile(width, q_rows, k, v, *, first, biased, cmp=False):
        half = rows // 2
        halves = [slice(0, half), slice(half, rows)]
        for hs in halves:
            s_scr[hs, :width] = lax.dot_general(q_rows(hs), k, NT_DIMS, preferred_element_type=F32)
        for hi, hs in enumerate(halves):
            chunks = range(hi * n_chunks // 2, (hi + 1) * n_chunks // 2)
            softmax_rows(width, chunks, first, biased, cmp)
            pv = _dot(p_scr[hs, :width], v)
            cols = slice(0, v.shape[1])
            if first:
                acc_scr[hs, cols] = pv
            else:
                acc_scr[hs, cols] += pv

    q_plain = lambda hs: q[hs]
    q_ext = lambda hs: qx_scr[hs, :]

    cmp_end = lax.broadcasted_iota(jnp.int32, (tq, ncp), 1) * CMP_STRIDE + (CMP_LEN - 1)
    bias_scr[:, :ncp] = jnp.where(cmp_end <= t_col, 0.0, NEG)
    psum_scr[...] = jnp.zeros(psum_scr.shape, F32)
    branch_tile(ncp, q_plain, kc_ref[...], vc_ref[...], first=True, biased=True, cmp=True)
    for h in range(r):
        out_scr[head(h), :] = gate(h, 0) * acc_scr[head(h), :LANES]

    mselt = mselt_ref[...]
    imp = jnp.zeros((ns, tq), F32)
    rem = psum_scr[...]
    for _ in range(3):
        piece = rem.astype(MXU_DTYPE)
        imp = imp + lax.dot_general(mselt, piece, NT_DIMS, preferred_element_type=F32)
        rem = rem - piece.astype(F32)

    blk = lax.broadcasted_iota(jnp.int32, (ns, tq), 0)
    tb = (q0 + lax.broadcasted_iota(jnp.int32, (ns, tq), 1)) // SEL_LEN
    forced = (blk == 0) | (blk == tb) | (blk == tb - 1)
    imp = jnp.where(blk <= tb, jnp.where(forced, jnp.inf, imp), -jnp.inf)
    sub = 8
    groups = [imp[sub * v:sub * (v + 1), :] for v in range(ns // sub)]
    ranks = [jnp.zeros((sub, tq), F32) for _ in groups]
    sub_id = lax.broadcasted_iota(jnp.int32, (sub, tq), 0)
    for mp in range(ns):
        row = jnp.broadcast_to(groups[mp // sub][mp % sub:mp % sub + 1, :], (sub, tq))
        for v, x in enumerate(groups):
            if sub * v > mp:
                ahead = row >= x
            elif sub * v + sub - 1 <= mp:
                ahead = row > x
            else:
                ahead = (row > x) | ((row == x) & (sub_id > mp % sub))
            ranks[v] = ranks[v] + jnp.where(ahead, 1.0, 0.0)
    rank = jnp.concatenate(ranks, axis=0)
    off = jnp.where((rank < float(min(N_SEL, ns))) & (blk < q0 // SEL_LEN), 0.0, NEG)
    off = off.T.astype(MXU_DTYPE)
    if ns < LANES:
        off = jnp.concatenate([off, jnp.zeros((tq, LANES - ns), MXU_DTYPE)], axis=1)
    for h in range(r):
        qx_scr[head(h), :LANES] = q_ref[h]
        qx_scr[head(h), LANES:] = off

    tri = lax.broadcasted_iota(jnp.int32, (tq, tq), 1) <= lax.broadcasted_iota(jnp.int32, (tq, tq), 0)
    bias_scr[:, :tq] = jnp.where(tri, 0.0, NEG)
    branch_tile(tq, q_plain, ksx_ref[pl.ds(q0, tq), :LANES], vsx_ref[pl.ds(q0, tq), :],
                first=True, biased=True)

    def sel_tile(j, carry):
        k0 = pl.multiple_of(j * tk, tk)
        branch_tile(tk, q_ext, ksx_ref[pl.ds(k0, tk), :], vsx_ref[pl.ds(k0, tk), :],
                    first=False, biased=False)
        return carry

    lax.fori_loop(0, (q0 + tk - 1) // tk, sel_tile, 0)
    o = acc_scr[:, :LANES] * (1.0 / acc_scr[:, LANES:])
    for h in range(r):
        out_scr[head(h), :] += gate(h, 1) * o[head(h), :]

    w0 = pl.multiple_of(jnp.maximum(q0 - WINDOW, 0), tq)
    kpos = w0 + lax.broadcasted_iota(jnp.int32, (tq, wlen), 1)
    bias_scr[:, :wlen] = jnp.where((kpos <= t_col) & (t_col - kpos < WINDOW), 0.0, NEG)
    branch_tile(wlen, q_plain, kw_ref[pl.ds(w0, wlen), :], vwx_ref[pl.ds(w0, wlen), :],
                first=True, biased=True)
    o = acc_scr[:, :LANES] * (1.0 / acc_scr[:, LANES:])
    for h in range(r):
        o_ref[:, h * HEAD_DIM:(h + 1) * HEAD_DIM] = (
            out_scr[head(h), :] + gate(h, 2) * o[head(h), :]).astype(o_ref.dtype)


def _attention(qkv, ksx, vsx, vwx, kc, vc, gates, mselt, batch, seq, q_base, kw_base, tq, tk):
    nq = seq // tq
    ncp = kc.shape[2]
    ns = mselt.shape[0]
    assert ns <= LANES and seq % tk == 0 and tk % tq == 0 and WINDOW % tq == 0
    rows = GROUP * tq
    wmax = max(tk, WINDOW + tq, ncp)
    ext_spec = pl.BlockSpec((None, seq, 2 * LANES), lambda b, g, i: (g, b, 0))
    cmp_spec = pl.BlockSpec((None, None, ncp, HEAD_DIM), lambda b, g, i: (b, g, 0, 0))
    stat = pltpu.VMEM((rows, LANES), F32)
    return pl.pallas_call(
        functools.partial(_attn_kernel, tq=tq, tk=tk, rb=16),
        grid=(batch, N_KV, nq),
        in_specs=[pl.BlockSpec((GROUP, tq, HEAD_DIM), lambda b, g, i: (q_base // GROUP + g, b * nq + i, 0)),
                  cmp_spec, cmp_spec, ext_spec, ext_spec,
                  pl.BlockSpec((None, seq, HEAD_DIM), lambda b, g, i: (kw_base + g, b, 0)),
                  ext_spec,
                  pl.BlockSpec((None, tq, LANES), lambda b, g, i: (g, b * nq + i, 0)),
                  pl.BlockSpec((ns, ncp), lambda b, g, i: (0, 0))],
        out_specs=pl.BlockSpec((tq, GROUP * HEAD_DIM), lambda b, g, i: (b * nq + i, g)),
        out_shape=jax.ShapeDtypeStruct((batch * seq, N_HEADS * HEAD_DIM), MXU_DTYPE),
        scratch_shapes=[pltpu.VMEM((rows, 2 * LANES), MXU_DTYPE),
                        pltpu.VMEM((rows, wmax), F32),
                        pltpu.VMEM((rows, wmax), MXU_DTYPE),
                        pltpu.VMEM((tq, wmax), F32),
                        stat, stat,
                        pltpu.VMEM((rows, 2 * LANES), F32),
                        pltpu.VMEM((tq, ncp), F32),
                        stat],
        compiler_params=_params("parallel", "parallel", "arbitrary"),
        name="nsa_attention",
    )(qkv, kc, vc, ksx, vsx, qkv, vwx, gates, mselt)


def _conv_kernel(h_ref, wx_ref, wb_ref, wc_ref, cw_ref, o_ref, ubuf, *, tiles_per_seq):
    i = pl.program_id(1)
    tm = h_ref.shape[0]
    h = h_ref[...]
    x_in = _dot(h, wx_ref[...])
    gate_b = _dot(h, wb_ref[...])
    gate_c = _dot(h, wc_ref[...])

    @pl.when(i % tiles_per_seq == 0)
    def _():
        ubuf[0:8, :] = jnp.zeros((8, ubuf.shape[1]), F32)

    ubuf[8:tm + 8, :] = gate_c * x_in
    w = cw_ref[...]
    conv = (w[2:3, :] * ubuf[8:tm + 8, :] + w[1:2, :] * ubuf[7:tm + 7, :]
            + w[0:1, :] * ubuf[6:tm + 6, :])
    o_ref[...] = (gate_b * conv).astype(o_ref.dtype)
    ubuf[0:8, :] = ubuf[tm:tm + 8, :]


def _conv_mixer(h, wx, wb, wc, cw, seq, tm, tn):
    m, d = h.shape
    n = wx.shape[1]
    wspec = pl.BlockSpec((d, tn), lambda j, i: (0, j))
    return pl.pallas_call(
        functools.partial(_conv_kernel, tiles_per_seq=seq // tm),
        grid=(n // tn, m // tm),
        in_specs=[pl.BlockSpec((tm, d), lambda j, i: (i, 0)), wspec, wspec, wspec,
                  pl.BlockSpec((8, tn), lambda j, i: (0, j))],
        out_specs=pl.BlockSpec((tm, tn), lambda j, i: (i, j)),
        out_shape=jax.ShapeDtypeStruct((m, n), MXU_DTYPE),
        scratch_shapes=[pltpu.VMEM((tm + 8, tn), F32)],
        compiler_params=_params("parallel", "arbitrary"),
        name="conv_mixer",
    )(h, wx, wb, wc, cw)


def _merge_kernel(oa_ref, v_ref, h_ref, wap_ref, wco_ref, wga_ref, wgc_ref, o_ref):
    h = h_ref[...]
    y_attn = _dot(oa_ref[...], wap_ref[...])
    y_conv = _dot(v_ref[...], wco_ref[...])
    g_attn = jax.nn.sigmoid(_dot(h, wga_ref[...]))
    g_conv = jax.nn.sigmoid(_dot(h, wgc_ref[...]))
    o_ref[...] = (g_attn * y_attn + g_conv * y_conv).astype(o_ref.dtype)


def _merge(oa, v, h, wap, wco, wga, wgc, tm, tn):
    m, d = h.shape
    n = wap.shape[1]
    aspec = pl.BlockSpec((tm, d), lambda i, j: (i, 0))
    wspec = pl.BlockSpec((d, tn), lambda i, j: (0, j))
    return pl.pallas_call(
        _merge_kernel,
        grid=(m // tm, n // tn),
        in_specs=[aspec, aspec, aspec, wspec, wspec, wspec, wspec],
        out_specs=pl.BlockSpec((tm, tn), lambda i, j: (i, j)),
        out_shape=jax.ShapeDtypeStruct((m, n), MXU_DTYPE),
        compiler_params=_params("parallel", "arbitrary"),
        name="gated_merge",
    )(oa, v, h, wap, wco, wga, wgc)


def _outproj_kernel(a_ref, w_ref, x_ref, g_ref, xo_ref, ho_ref):
    x = x_ref[...] + _dot(a_ref[...], w_ref[...])
    xo_ref[...] = x
    ho_ref[...] = _rms(x, g_ref[...]).astype(ho_ref.dtype)


def _outproj(a, w, x, g, tm):
    m, d = x.shape
    row = pl.BlockSpec((tm, d), lambda i: (i, 0))
    return pl.pallas_call(
        _outproj_kernel,
        grid=(m // tm,),
        in_specs=[row, pl.BlockSpec((d, d), lambda i: (0, 0)), row,
                  pl.BlockSpec((1, d), lambda i: (0, 0))],
        out_specs=[row, row],
        out_shape=[jax.ShapeDtypeStruct((m, d), F32), jax.ShapeDtypeStruct((m, d), MXU_DTYPE)],
        compiler_params=_params("parallel"),
        name="out_proj",
    )(a, w, x, g.reshape(1, d))


def _mlp_kernel(h_ref, wu_ref, wd_ref, x_ref, g_ref, *refs, last):
    acc = refs[-1]
    f = pl.program_id(1)

    @pl.when(f == 0)
    def _():
        acc[...] = x_ref[...]

    a = jnp.maximum(_dot(h_ref[...], wu_ref[...]), 0.0)
    acc[...] += _dot((a * a).astype(MXU_DTYPE), wd_ref[...])

    @pl.when(f == pl.num_programs(1) - 1)
    def _():
        x = acc[...]
        normed = _rms(x, g_ref[...])
        if last:
            refs[0][...] = normed
        else:
            refs[0][...] = x
            refs[1][...] = normed.astype(refs[1].dtype)


def _mlp(h, wu, wd, x, g, tm, tf, last):
    m, d = x.shape
    ff = wu.shape[1]
    row = pl.BlockSpec((tm, d), lambda i, f: (i, 0))
    if last:
        out_specs, out_shape = [row], [jax.ShapeDtypeStruct((m, d), F32)]
    else:
        out_specs = [row, row]
        out_shape = [jax.ShapeDtypeStruct((m, d), F32), jax.ShapeDtypeStruct((m, d), MXU_DTYPE)]
    return pl.pallas_call(
        functools.partial(_mlp_kernel, last=last),
        grid=(m // tm, ff // tf),
        in_specs=[row, pl.BlockSpec((d, tf), lambda i, f: (0, f)),
                  pl.BlockSpec((tf, d), lambda i, f: (f, 0)), row,
                  pl.BlockSpec((1, d), lambda i, f: (0, 0))],
        out_specs=out_specs,
        out_shape=out_shape,
        scratch_shapes=[pltpu.VMEM((tm, d), F32)],
        compiler_params=_params("parallel", "arbitrary"),
        name="relu2_mlp",
    )(h, wu, wd, x, g.reshape(1, d))


def _rope_tables(pos):
    half = HEAD_DIM // 2
    inv_freq = jnp.exp(-math.log(ROPE_THETA) * jnp.arange(half, dtype=F32) / half)
    ang = pos.astype(F32)[:, None] * inv_freq[None, :]
    cos, sin = jnp.cos(ang), jnp.sin(ang)
    return jnp.concatenate([cos, cos], axis=-1), jnp.concatenate([-sin, sin], axis=-1)


def _cmp_to_sel_t(ncp, ns):
    nc = ncp - 1
    cs = np.arange(nc) * CMP_STRIDE
    ss = np.arange(ns) * SEL_LEN
    ov = np.minimum(cs[:, None] + CMP_LEN, ss[None, :] + SEL_LEN) - np.maximum(cs[:, None], ss[None, :])
    m = np.zeros((ncp, ns), np.float32)
    m[:nc] = np.clip(ov, 0, None) / CMP_LEN
    return jnp.asarray(m.T, dtype=MXU_DTYPE)


def _block_onehot(seq):
    e = (np.arange(seq)[:, None] // SEL_LEN == np.arange(LANES)[None, :]).astype(np.float32)
    return jnp.asarray(e, dtype=MXU_DTYPE)


def kernel(x, norm1_g, w_in, cmp_pos_k, cmp_w1_k, cmp_w2_k, cmp_pos_v, cmp_w1_v, cmp_w2_v,
           conv_w, w_attn_proj, w_conv_out, w_o, norm2_g, w_up, w_down, final_g):
    batch, seq, d = x.shape
    depth = w_in.shape[0]
    m = batch * seq
    attn_dim = N_HEADS * HEAD_DIM
    kv_dim = N_KV * HEAD_DIM
    n_gate = N_HEADS * N_NSA_BRANCH
    ncp = seq // CMP_STRIDE
    ns = seq // SEL_LEN
    cast = lambda a: a.astype(MXU_DTYPE)

    tm = min(1024, seq)
    tm_small = min(512, seq)
    tq = 128
    tk = 512

    cos_t, sin_t = _rope_tables(jnp.arange(seq))
    cos_c, sin_c = _rope_tables(jnp.arange(ncp) * CMP_STRIDE + CMP_LEN - 1)
    mselt = _cmp_to_sel_t(ncp, ns)
    onehot = jnp.broadcast_to(jnp.tile(_block_onehot(seq), (batch, 1))[None], (N_KV, m, LANES))
    ones = jnp.ones((N_KV, m, LANES), MXU_DTYPE)

    q_base, ks_base, kw_base, kc_base, vc_base, vs_base, vw_base = 0, 16, 20, 24, 28, 32, 36

    x2 = x.reshape(m, d)
    h = _rmsnorm(x2, norm1_g[0], tm)
    out = None
    for l in range(depth):
        wl = w_in[l]
        o_kv = attn_dim
        kv = [wl[:, o_kv + j * kv_dim:o_kv + (j + 1) * kv_dim] for j in range(6)]
        w_qkv = cast(jnp.concatenate([wl[:, :attn_dim], kv[2], kv[4], kv[0], kv[1], kv[3], kv[5]], axis=1))
        o_ng = o_kv + 6 * kv_dim
        w_ng = wl[:, o_ng:o_ng + n_gate].reshape(d, N_KV, GROUP * N_NSA_BRANCH)
        w_ng = cast(jnp.pad(w_ng, ((0, 0), (0, 0), (0, LANES - GROUP * N_NSA_BRANCH))).reshape(d, N_KV * LANES))
        o_cv = o_ng + n_gate
        w_x, w_b, w_c = [cast(wl[:, o_cv + j * d:o_cv + (j + 1) * d]) for j in range(3)]
        o_mg = o_cv + 3 * d
        w_ga, w_gc = [cast(wl[:, o_mg + j * d:o_mg + (j + 1) * d]) for j in range(2)]

        qkv = _qkv_proj(h, w_qkv, cos_t, sin_t, seq, tm, 512, n_rope_tiles=6, n_q_tiles=4)
        gates = _gate_proj(h, w_ng, tm)

        chunks = qkv[kc_base:kc_base + 2 * N_KV].reshape(2 * N_KV, batch, ncp, CMP_STRIDE * HEAD_DIM)
        pad_pos = lambda p: cast(jnp.pad(p.reshape(1, CMP_LEN * HEAD_DIM), ((0, 7), (0, 0))))
        kc, vc = _compress(chunks, 0, N_KV, cast(cmp_w1_k[l]), cast(cmp_w2_k[l]), pad_pos(cmp_pos_k[l]),
                           cast(cmp_w1_v[l]), cast(cmp_w2_v[l]), pad_pos(cmp_pos_v[l]), cos_c, sin_c, batch)

        ksx = jnp.concatenate([qkv[ks_base:ks_base + N_KV], onehot], axis=-1)
        vsx = jnp.concatenate([qkv[vs_base:vs_base + N_KV], ones], axis=-1)
        vwx = jnp.concatenate([qkv[vw_base:vw_base + N_KV], ones], axis=-1)
        o_attn = _attention(qkv, ksx, vsx, vwx, kc, vc, gates, mselt, batch, seq,
                            q_base, kw_base, tq, tk)

        cw = jnp.pad(conv_w[l], ((0, 8 - CONV_WIDTH), (0, 0)))
        v_conv = _conv_mixer(h, w_x, w_b, w_c, cw, seq, tm_small, 512)

        merged = _merge(o_attn, v_conv, h, cast(w_attn_proj[l]), cast(w_conv_out[l]), w_ga, w_gc,
                        tm_small, 512)
        x2, h2 = _outproj(merged, cast(w_o[l]), x2, norm2_g[l], tm_small)

        last = l == depth - 1
        g_next = final_g if last else norm1_g[l + 1]
        res = _mlp(h2, cast(w_up[l]), cast(w_down[l]), x2, g_next, tm_small, 512, last)
        if last:
            out = res[0]
        else:
            x2, h = res
    return out.reshape(batch, seq, d)
```

```python
import collections
import functools
import math

import numpy as np
import jax
import jax.numpy as jnp
from jax import lax
from jax.experimental import pallas as pl
from jax.experimental.pallas import tpu as pltpu

N_HEADS = 16
HEAD_DIM = 128
N_KV = 4
GROUP = N_HEADS // N_KV
CMP_LEN = 32
CMP_STRIDE = 16
SEL_LEN = 64
N_SEL = 16
WINDOW = 512
N_NSA_BRANCH = 3
CONV_WIDTH = 3
ROPE_THETA = 10000.0
EPS = 1e-6

MXU_DTYPE = jnp.bfloat16
F32 = jnp.float32
NEG = -1e30
VMEM_LIMIT = 56 * 1024 * 1024
LANES = 128
NT_DIMS = (((1,), (1,)), ((), ()))


def _params(*sem):
    return pltpu.CompilerParams(dimension_semantics=sem, vmem_limit_bytes=VMEM_LIMIT)


def _dot(a, b):
    return jnp.dot(a, b, preferred_element_type=F32)


def _rms(x, g):
    return x * lax.rsqrt(jnp.mean(x * x, axis=-1, keepdims=True) + EPS) * g


def _norm_kernel(x_ref, g_ref, o_ref):
    o_ref[...] = _rms(x_ref[...], g_ref[...]).astype(o_ref.dtype)


def _rmsnorm(x2d, g, tm):
    m, d = x2d.shape
    return pl.pallas_call(
        _norm_kernel,
        grid=(m // tm,),
        in_specs=[pl.BlockSpec((tm, d), lambda i: (i, 0)), pl.BlockSpec((1, d), lambda i: (0, 0))],
        out_specs=pl.BlockSpec((tm, d), lambda i: (i, 0)),
        out_shape=jax.ShapeDtypeStruct((m, d), MXU_DTYPE),
        compiler_params=_params("parallel"),
        name="rmsnorm",
    )(x2d, g.reshape(1, d))


def _qkv_kernel(h_ref, w_ref, cos_ref, sin_ref, o_ref, *, n_rope_tiles, n_q_tiles, q_scale):
    j = pl.program_id(1)
    acc = _dot(h_ref[...], w_ref[...])
    heads = o_ref.shape[0]

    @pl.when(j < n_rope_tiles)
    def _():
        mult = jnp.where(j < n_q_tiles, q_scale, 1.0).astype(F32)
        cos = cos_ref[...] * mult
        sin = sin_ref[...] * mult
        for c in range(heads):
            xc = acc[:, c * LANES:(c + 1) * LANES]
            o_ref[c] = (xc * cos + pltpu.roll(xc, HEAD_DIM // 2, 1) * sin).astype(o_ref.dtype)

    @pl.when(j >= n_rope_tiles)
    def _():
        for c in range(heads):
            o_ref[c] = acc[:, c * LANES:(c + 1) * LANES].astype(o_ref.dtype)


def _qkv_proj(h, w, cos, sin, seq, tm, tn, n_rope_tiles, n_q_tiles):
    m, d = h.shape
    n = w.shape[1]
    hpt = tn // LANES
    tps = seq // tm
    kern = functools.partial(_qkv_kernel, n_rope_tiles=n_rope_tiles, n_q_tiles=n_q_tiles,
                             q_scale=HEAD_DIM ** -0.5 * math.log2(math.e))
    return pl.pallas_call(
        kern,
        grid=(m // tm, n // tn),
        in_specs=[pl.BlockSpec((tm, d), lambda i, j: (i, 0)),
                  pl.BlockSpec((d, tn), lambda i, j: (0, j)),
                  pl.BlockSpec((tm, LANES), lambda i, j: (i % tps, 0)),
                  pl.BlockSpec((tm, LANES), lambda i, j: (i % tps, 0))],
        out_specs=pl.BlockSpec((hpt, tm, LANES), lambda i, j: (j, i, 0)),
        out_shape=jax.ShapeDtypeStruct((n // LANES, m, LANES), MXU_DTYPE),
        compiler_params=_params("parallel", "arbitrary"),
        name="qkv_proj",
    )(h, w, cos, sin)


def _gate_kernel(h_ref, w_ref, o_ref):
    acc = _dot(h_ref[...], w_ref[...])
    for c in range(o_ref.shape[0]):
        o_ref[c] = jax.nn.sigmoid(acc[:, c * LANES:(c + 1) * LANES])


def _gate_proj(h, w, tm):
    m, d = h.shape
    n = w.shape[1]
    return pl.pallas_call(
        _gate_kernel,
        grid=(m // tm,),
        in_specs=[pl.BlockSpec((tm, d), lambda i: (i, 0)), pl.BlockSpec((d, n), lambda i: (0, 0))],
        out_specs=pl.BlockSpec((n // LANES, tm, LANES), lambda i: (0, i, 0)),
        out_shape=jax.ShapeDtypeStruct((n // LANES, m, LANES), F32),
        compiler_params=_params("parallel"),
        name="nsa_gate_proj",
    )(h, w)


def _compress_kernel(ck_ref, cv_ref, w1k_ref, w2k_ref, pk_ref, w1v_ref, w2v_ref, pv_ref,
                     cos_ref, sin_ref, kc_ref, vc_ref):
    nch = ck_ref.shape[0]
    half = w1k_ref.shape[0] // 2

    def phi(c_ref, w1_ref, w2_ref, p_ref):
        c = c_ref[...]
        first = _dot(c, w1_ref[:half, :])
        second = _dot(c, w1_ref[half:, :])
        pos = _dot(p_ref[...], w1_ref[...])[0:1, :]
        hid = first + pltpu.roll(second, nch - 1, 0) + pos
        act = hid * jax.nn.sigmoid(hid)
        return _dot(act.astype(MXU_DTYPE), w2_ref[...])

    kc = phi(ck_ref, w1k_ref, w2k_ref, pk_ref)
    kc = kc * cos_ref[...] + pltpu.roll(kc, HEAD_DIM // 2, 1) * sin_ref[...]
    kc_ref[...] = kc.astype(kc_ref.dtype)
    vc_ref[...] = phi(cv_ref, w1v_ref, w2v_ref, pv_ref).astype(vc_ref.dtype)


def _compress(chunks, kbase, vbase, w1k, w2k, pk, w1v, w2v, pv, cos_c, sin_c, batch):
    _, _, nch, cw = chunks.shape
    const = lambda a: pl.BlockSpec(a.shape, lambda b, g: (0,) * a.ndim)
    out_spec = pl.BlockSpec((None, None, nch, HEAD_DIM), lambda b, g: (b, g, 0, 0))
    out_sds = jax.ShapeDtypeStruct((batch, N_KV, nch, HEAD_DIM), MXU_DTYPE)
    return pl.pallas_call(
        _compress_kernel,
        grid=(batch, N_KV),
        in_specs=[pl.BlockSpec((None, None, nch, cw), lambda b, g: (kbase + g, b, 0, 0)),
                  pl.BlockSpec((None, None, nch, cw), lambda b, g: (vbase + g, b, 0, 0)),
                  const(w1k), const(w2k), const(pk), const(w1v), const(w2v), const(pv),
                  const(cos_c), const(sin_c)],
        out_specs=[out_spec, out_spec],
        out_shape=[out_sds, out_sds],
        compiler_params=_params("parallel", "parallel"),
        name="compress",
    )(chunks, chunks, w1k, w2k, pk, w1v, w2v, pv, cos_c, sin_c)


_Branch = collections.namedtuple("_Branch", "s p m acc bias")


def _attn_kernel(q_ref, kc_ref, vc_ref, ksx_ref, vsx_ref, kw_ref, vwx_ref, gate_ref, mselt_ref,
                 o_ref, qx_scr, sc_scr, pc_scr, sd_scr, pd_scr, sw_scr, pw_scr, sa_scr, sb_scr,
                 pa_scr, pb_scr, bc_scr, bd_scr, bw_scr, mc_scr, mw_scr, m_scr, a_scr,
                 accc_scr, accw_scr, acc_scr, psum_scr, imp_scr, rank_scr, *, tq, tk, rb):
    i = pl.program_id(2)
    q0 = pl.multiple_of(i * tq, tq)
    r = GROUP
    rows = r * tq
    ncp = kc_ref.shape[0]
    ns = mselt_ref.shape[0]
    wlen = WINDOW + tq
    n_chunks = rows // rb
    t_col = q0 + lax.broadcasted_iota(jnp.int32, (tq, 1), 0)
    q = q_ref[...].reshape(rows, HEAD_DIM)
    gates = gate_ref[...]
    head = lambda h: slice(h * tq, (h + 1) * tq)
    gate = lambda h, br: gates[:, h * N_NSA_BRANCH + br:h * N_NSA_BRANCH + br + 1]
    half = rows // 2
    halves = [slice(0, half), slice(half, rows)]
    lane = lambda kk: slice(kk * LANES, (kk + 1) * LANES)

    def load_scores(br, rs, bs, width):
        xs = [br.s[rs, lane(kk)] for kk in range(width // LANES)]
        return [x if b is None else x + b[bs, :] for x, b in zip(xs, br.bias)]

    def softmax_rows(br, width, chunks, first, cmp):
        def slices(c):
            r0 = c * rb
            return slice(r0, r0 + rb), slice(r0 % tq, r0 % tq + rb)

        for c in chunks:
            rs, bs = slices(c)
            xs = load_scores(br, rs, bs, width)
            mx = jnp.max(functools.reduce(jnp.maximum, xs), axis=-1, keepdims=True)
            if first:
                br.m[rs, :] = jnp.broadcast_to(mx, (rb, LANES))
            else:
                m_old = br.m[rs, :]
                m_new = jnp.maximum(m_old, mx)
                a_scr[rs, :] = jnp.exp2(m_old - m_new)
                br.m[rs, :] = m_new

        for c in chunks:
            rs, bs = slices(c)
            xs = load_scores(br, rs, bs, width)
            m = br.m[rs, :]
            ps = [jnp.exp2(x - m) for x in xs]
            if cmp:
                ps = [jnp.where(x > 0.5 * NEG, p, 0.0) for x, p in zip(xs, ps)]
                lsum = jnp.sum(functools.reduce(jnp.add, ps), axis=-1, keepdims=True)
                inv = 1.0 / jnp.where(lsum > 0.0, lsum, 1.0)
                ps = [p * inv for p in ps]
                for kk, p in enumerate(ps):
                    psum_scr[bs, lane(kk)] += p
            elif not first:
                a = a_scr[rs, :]
                br.acc[rs, :LANES] = a * br.acc[rs, :LANES]
                br.acc[rs, LANES:] = a * br.acc[rs, LANES:]
            for kk, p in enumerate(ps):
                br.p[rs, lane(kk)] = p.astype(br.p.dtype)

    def scores(br, width, q_rows, k):
        for hs in halves:
            br.s[hs, :width] = lax.dot_general(q_rows(hs), k, NT_DIMS, preferred_element_type=F32)

    def weighted_values(br, width, v, *, first, cmp=False):
        for hi, hs in enumerate(halves):
            chunks = range(hi * n_chunks // 2, (hi + 1) * n_chunks // 2)
            softmax_rows(br, width, chunks, first, cmp)
            pv = _dot(br.p[hs, :width], v)
            cols = slice(0, v.shape[1])
            if first:
                br.acc[hs, cols] = pv
            else:
                br.acc[hs, cols] += pv

    q_plain = lambda hs: q[hs]
    q_ext = lambda hs: qx_scr[hs, :]
    row_id = lax.broadcasted_iota(jnp.int32, (tq, LANES), 0)
    col_id = lax.broadcasted_iota(jnp.int32, (tq, LANES), 1)

    n_cl = ncp // LANES
    for kk in range(n_cl):
        cmp_end = (col_id + kk * LANES) * CMP_STRIDE + (CMP_LEN - 1)
        bc_scr[:, lane(kk)] = jnp.where(cmp_end <= t_col, 0.0, NEG)
    n_dl = tq // LANES
    for kk in range(n_dl):
        bd_scr[:, lane(kk)] = jnp.where(col_id + kk * LANES <= row_id, 0.0, NEG)
    n_wl = wlen // LANES
    for kk in range(n_dl):
        bw_scr[:, lane(kk)] = jnp.where(col_id + kk * LANES > row_id, 0.0, NEG)
        bw_scr[:, lane(n_dl + kk)] = jnp.where(col_id + kk * LANES <= row_id, 0.0, NEG)

    cmp_br = _Branch(sc_scr, pc_scr, mc_scr, accc_scr, [bc_scr.at[:, lane(kk)] for kk in range(n_cl)])
    diag_br = _Branch(sd_scr, pd_scr, m_scr, acc_scr, [bd_scr.at[:, lane(kk)] for kk in range(n_dl)])
    win_bias = ([bw_scr.at[:, lane(kk)] for kk in range(n_dl)] + [None] * (n_wl - 2 * n_dl)
                + [bw_scr.at[:, lane(n_dl + kk)] for kk in range(n_dl)])
    win_br = _Branch(sw_scr, pw_scr, mw_scr, accw_scr, win_bias)
    sel_a = _Branch(sa_scr, pa_scr, m_scr, acc_scr, [None] * (tk // LANES))
    sel_b = _Branch(sb_scr, pb_scr, m_scr, acc_scr, [None] * (tk // LANES))

    psum_scr[...] = jnp.zeros(psum_scr.shape, F32)
    scores(cmp_br, ncp, q_plain, kc_ref[...])
    scores(win_br, wlen, q_plain, kw_ref[pl.ds(q0, wlen), :])
    scores(diag_br, tq, q_plain, ksx_ref[pl.ds(q0, tq), :LANES])
    for kk in range(WINDOW // LANES):
        @pl.when(q0 + (kk + 1) * LANES <= WINDOW)
        def _():
            sw_scr[:, lane(kk)] = jnp.full((rows, LANES), NEG, F32)

    weighted_values(cmp_br, ncp, vc_ref[...], first=True, cmp=True)
    mselt = mselt_ref[...]
    imp = jnp.zeros((ns, tq), F32)
    rem = psum_scr[...]
    for _ in range(3):
        piece = rem.astype(MXU_DTYPE)
        imp = imp + lax.dot_general(mselt, piece, NT_DIMS, preferred_element_type=F32)
        rem = rem - piece.astype(F32)
    blk = lax.broadcasted_iota(jnp.int32, (ns, tq), 0)
    tb = (q0 + lax.broadcasted_iota(jnp.int32, (ns, tq), 1)) // SEL_LEN
    forced = (blk == 0) | (blk == tb) | (blk == tb - 1)
    imp_scr[...] = jnp.where(blk <= tb, jnp.where(forced, jnp.inf, imp), -jnp.inf)
    rank_scr[...] = jnp.zeros(rank_scr.shape, F32)

    weighted_values(win_br, wlen, vwx_ref[pl.ds(q0, wlen), :], first=True)
    weighted_values(diag_br, tq, vsx_ref[pl.ds(q0, tq), :], first=True)

    sub = 8
    n_grp = ns // sub
    sub_id = lax.broadcasted_iota(jnp.int32, (sub, tq), 0)
    grp = lambda v: slice(sub * v, sub * (v + 1))
    for gm in range(n_grp):
        @pl.when(sub * gm * SEL_LEN <= q0 + tq - 1)
        def _():
            xs = [imp_scr[grp(v), :] for v in range(n_grp)]
            ranks = [rank_scr[grp(v), :] for v in range(n_grp)]
            for mp in range(sub * gm, sub * (gm + 1)):
                row = jnp.broadcast_to(xs[gm][mp % sub:mp % sub + 1, :], (sub, tq))
                for v, x in enumerate(xs):
                    if sub * v > mp:
                        ahead = row >= x
                    elif sub * v + sub - 1 <= mp:
                        ahead = row > x
                    else:
                        ahead = (row > x) | ((row == x) & (sub_id > mp % sub))
                    ranks[v] = ranks[v] + jnp.where(ahead, 1.0, 0.0)
            for v in range(n_grp):
                rank_scr[grp(v), :] = ranks[v]

    off = jnp.where((rank_scr[...] < float(min(N_SEL, ns))) & (blk < q0 // SEL_LEN), 0.0, NEG)
    off = off.T.astype(MXU_DTYPE)
    if ns < LANES:
        off = jnp.concatenate([off, jnp.zeros((tq, LANES - ns), MXU_DTYPE)], axis=1)
    for h in range(r):
        qx_scr[head(h), :LANES] = q_ref[h]
        qx_scr[head(h), LANES:] = off

    last_tile = ksx_ref.shape[0] // tk - 1

    def tile_rows(t):
        return pl.ds(pl.multiple_of(jnp.minimum(t, last_tile) * tk, tk), tk)

    def sel_pair(jj, carry):
        t = 2 * jj
        scores(sel_b, tk, q_ext, ksx_ref[tile_rows(t + 1), :])
        weighted_values(sel_a, tk, vsx_ref[tile_rows(t), :], first=False)
        scores(sel_a, tk, q_ext, ksx_ref[tile_rows(t + 2), :])
        weighted_values(sel_b, tk, vsx_ref[tile_rows(t + 1), :], first=False)
        return carry

    n_tiles = (q0 + tk - 1) // tk
    scores(sel_a, tk, q_ext, ksx_ref[tile_rows(0), :])
    lax.fori_loop(0, (n_tiles + 1) // 2, sel_pair, 0)

    o_sel = acc_scr[:, :LANES] * (1.0 / acc_scr[:, LANES:])
    o_win = accw_scr[:, :LANES] * (1.0 / accw_scr[:, LANES:])
    for h in range(r):
        o = (gate(h, 0) * accc_scr[head(h), :] + gate(h, 1) * o_sel[head(h), :]
             + gate(h, 2) * o_win[head(h), :])
        o_ref[:, h * HEAD_DIM:(h + 1) * HEAD_DIM] = o.astype(o_ref.dtype)


def _attention(qkv, ksx, vsx, kwp, vwx, kc, vc, gates, mselt, batch, seq, q_base, tq, tk):
    nq = seq // tq
    ncp = kc.shape[2]
    ns = mselt.shape[0]
    assert ns <= LANES and seq % (2 * tk) == 0 and tk % tq == 0 and WINDOW % tq == 0
    assert tq % LANES == 0 and ncp % LANES == 0
    rows = GROUP * tq
    wlen = WINDOW + tq
    ext_spec = pl.BlockSpec((None, seq, 2 * LANES), lambda b, g, i: (g, b, 0))
    cmp_spec = pl.BlockSpec((None, None, ncp, HEAD_DIM), lambda b, g, i: (b, g, 0, 0))
    f32 = lambda *shape: pltpu.VMEM(shape, F32)
    mxu = lambda *shape: pltpu.VMEM(shape, MXU_DTYPE)
    return pl.pallas_call(
        functools.partial(_attn_kernel, tq=tq, tk=tk, rb=16),
        grid=(batch, N_KV, nq),
        in_specs=[pl.BlockSpec((GROUP, tq, HEAD_DIM), lambda b, g, i: (q_base // GROUP + g, b * nq + i, 0)),
                  cmp_spec, cmp_spec, ext_spec, ext_spec,
                  pl.BlockSpec((None, None, seq + WINDOW, HEAD_DIM), lambda b, g, i: (g, b, 0, 0)),
                  pl.BlockSpec((None, None, seq + WINDOW, 2 * LANES), lambda b, g, i: (g, b, 0, 0)),
                  pl.BlockSpec((None, tq, LANES), lambda b, g, i: (g, b * nq + i, 0)),
                  pl.BlockSpec((ns, ncp), lambda b, g, i: (0, 0))],
        out_specs=pl.BlockSpec((tq, GROUP * HEAD_DIM), lambda b, g, i: (b * nq + i, g)),
        out_shape=jax.ShapeDtypeStruct((batch * seq, N_HEADS * HEAD_DIM), MXU_DTYPE),
        scratch_shapes=[mxu(rows, 2 * LANES),
                        f32(rows, ncp), mxu(rows, ncp),
                        f32(rows, tq), mxu(rows, tq),
                        f32(rows, wlen), mxu(rows, wlen),
                        f32(rows, tk), f32(rows, tk),
                        mxu(rows, tk), mxu(rows, tk),
                        f32(tq, ncp), f32(tq, tq), f32(tq, 2 * tq),
                        f32(rows, LANES), f32(rows, LANES), f32(rows, LANES),
                        f32(rows, LANES),
                        f32(rows, LANES),
                        f32(rows, 2 * LANES), f32(rows, 2 * LANES),
                        f32(tq, ncp),
                        f32(ns, tq), f32(ns, tq)],
        compiler_params=_params("parallel", "parallel", "arbitrary"),
        name="nsa_attention",
    )(qkv, kc, vc, ksx, vsx, kwp, vwx, gates, mselt)


def _conv_kernel(h_ref, wx_ref, wb_ref, wc_ref, cw_ref, o_ref, ubuf, *, tiles_per_seq):
    i = pl.program_id(1)
    tm = h_ref.shape[0]
    h = h_ref[...]
    x_in = _dot(h, wx_ref[...])
    gate_b = _dot(h, wb_ref[...])
    gate_c = _dot(h, wc_ref[...])

    @pl.when(i % tiles_per_seq == 0)
    def _():
        ubuf[0:8, :] = jnp.zeros((8, ubuf.shape[1]), F32)

    ubuf[8:tm + 8, :] = gate_c * x_in
    w = cw_ref[...]
    conv = (w[2:3, :] * ubuf[8:tm + 8, :] + w[1:2, :] * ubuf[7:tm + 7, :]
            + w[0:1, :] * ubuf[6:tm + 6, :])
    o_ref[...] = (gate_b * conv).astype(o_ref.dtype)
    ubuf[0:8, :] = ubuf[tm:tm + 8, :]


def _conv_mixer(h, wx, wb, wc, cw, seq, tm, tn):
    m, d = h.shape
    n = wx.shape[1]
    wspec = pl.BlockSpec((d, tn), lambda j, i: (0, j))
    return pl.pallas_call(
        functools.partial(_conv_kernel, tiles_per_seq=seq // tm),
        grid=(n // tn, m // tm),
        in_specs=[pl.BlockSpec((tm, d), lambda j, i: (i, 0)), wspec, wspec, wspec,
                  pl.BlockSpec((8, tn), lambda j, i: (0, j))],
        out_specs=pl.BlockSpec((tm, tn), lambda j, i: (i, j)),
        out_shape=jax.ShapeDtypeStruct((m, n), MXU_DTYPE),
        scratch_shapes=[pltpu.VMEM((tm + 8, tn), F32)],
        compiler_params=_params("parallel", "arbitrary"),
        name="conv_mixer",
    )(h, wx, wb, wc, cw)


def _merge_kernel(oa_ref, v_ref, h_ref, wap_ref, wco_ref, wga_ref, wgc_ref, o_ref):
    h = h_ref[...]
    y_attn = _dot(oa_ref[...], wap_ref[...])
    y_conv = _dot(v_ref[...], wco_ref[...])
    g_attn = jax.nn.sigmoid(_dot(h, wga_ref[...]))
    g_conv = jax.nn.sigmoid(_dot(h, wgc_ref[...]))
    o_ref[...] = (g_attn * y_attn + g_conv * y_conv).astype(o_ref.dtype)


def _merge(oa, v, h, wap, wco, wga, wgc, tm, tn):
    m, d = h.shape
    n = wap.shape[1]
    aspec = pl.BlockSpec((tm, d), lambda i, j: (i, 0))
    wspec = pl.BlockSpec((d, tn), lambda i, j: (0, j))
    return pl.pallas_call(
        _merge_kernel,
        grid=(m // tm, n // tn),
        in_specs=[aspec, aspec, aspec, wspec, wspec, wspec, wspec],
        out_specs=pl.BlockSpec((tm, tn), lambda i, j: (i, j)),
        out_shape=jax.ShapeDtypeStruct((m, n), MXU_DTYPE),
        compiler_params=_params("parallel", "arbitrary"),
        name="gated_merge",
    )(oa, v, h, wap, wco, wga, wgc)


def _outproj_kernel(a_ref, w_ref, x_ref, g_ref, xo_ref, ho_ref):
    x = x_ref[...] + _dot(a_ref[...], w_ref[...])
    xo_ref[...] = x
    ho_ref[...] = _rms(x, g_ref[...]).astype(ho_ref.dtype)


def _outproj(a, w, x, g, tm):
    m, d = x.shape
    row = pl.BlockSpec((tm, d), lambda i: (i, 0))
    return pl.pallas_call(
        _outproj_kernel,
        grid=(m // tm,),
        in_specs=[row, pl.BlockSpec((d, d), lambda i: (0, 0)), row,
                  pl.BlockSpec((1, d), lambda i: (0, 0))],
        out_specs=[row, row],
        out_shape=[jax.ShapeDtypeStruct((m, d), F32), jax.ShapeDtypeStruct((m, d), MXU_DTYPE)],
        compiler_params=_params("parallel"),
        name="out_proj",
    )(a, w, x, g.reshape(1, d))


def _mlp_kernel(h_ref, wu_ref, wd_ref, x_ref, g_ref, *refs, last):
    acc = refs[-1]
    f = pl.program_id(1)

    @pl.when(f == 0)
    def _():
        acc[...] = x_ref[...]

    a = jnp.maximum(_dot(h_ref[...], wu_ref[...]), 0.0)
    acc[...] += _dot((a * a).astype(MXU_DTYPE), wd_ref[...])

    @pl.when(f == pl.num_programs(1) - 1)
    def _():
        x = acc[...]
        normed = _rms(x, g_ref[...])
        if last:
            refs[0][...] = normed
        else:
            refs[0][...] = x
            refs[1][...] = normed.astype(refs[1].dtype)


def _mlp(h, wu, wd, x, g, tm, tf, last):
    m, d = x.shape
    ff = wu.shape[1]
    row = pl.BlockSpec((tm, d), lambda i, f: (i, 0))
    if last:
        out_specs, out_shape = [row], [jax.ShapeDtypeStruct((m, d), F32)]
    else:
        out_specs = [row, row]
        out_shape = [jax.ShapeDtypeStruct((m, d), F32), jax.ShapeDtypeStruct((m, d), MXU_DTYPE)]
    return pl.pallas_call(
        functools.partial(_mlp_kernel, last=last),
        grid=(m // tm, ff // tf),
        in_specs=[row, pl.BlockSpec((d, tf), lambda i, f: (0, f)),
                  pl.BlockSpec((tf, d), lambda i, f: (f, 0)), row,
                  pl.BlockSpec((1, d), lambda i, f: (0, 0))],
        out_specs=out_specs,
        out_shape=out_shape,
        scratch_shapes=[pltpu.VMEM((tm, d), F32)],
        compiler_params=_params("parallel", "arbitrary"),
        name="relu2_mlp",
    )(h, wu, wd, x, g.reshape(1, d))


def _rope_tables(pos):
    half = HEAD_DIM // 2
    inv_freq = jnp.exp(-math.log(ROPE_THETA) * jnp.arange(half, dtype=F32) / half)
    ang = pos.astype(F32)[:, None] * inv_freq[None, :]
    cos, sin = jnp.cos(ang), jnp.sin(ang)
    return jnp.concatenate([cos, cos], axis=-1), jnp.concatenate([-sin, sin], axis=-1)


def _cmp_to_sel_t(ncp, ns):
    nc = ncp - 1
    cs = np.arange(nc) * CMP_STRIDE
    ss = np.arange(ns) * SEL_LEN
    ov = np.minimum(cs[:, None] + CMP_LEN, ss[None, :] + SEL_LEN) - np.maximum(cs[:, None], ss[None, :])
    m = np.zeros((ncp, ns), np.float32)
    m[:nc] = np.clip(ov, 0, None) / CMP_LEN
    return jnp.asarray(m.T, dtype=MXU_DTYPE)


def _block_onehot(seq):
    e = (np.arange(seq)[:, None] // SEL_LEN == np.arange(LANES)[None, :]).astype(np.float32)
    return jnp.asarray(e, dtype=MXU_DTYPE)


def kernel(x, norm1_g, w_in, cmp_pos_k, cmp_w1_k, cmp_w2_k, cmp_pos_v, cmp_w1_v, cmp_w2_v,
           conv_w, w_attn_proj, w_conv_out, w_o, norm2_g, w_up, w_down, final_g):
    batch, seq, d = x.shape
    depth = w_in.shape[0]
    m = batch * seq
    attn_dim = N_HEADS * HEAD_DIM
    kv_dim = N_KV * HEAD_DIM
    n_gate = N_HEADS * N_NSA_BRANCH
    ncp = seq // CMP_STRIDE
    ns = seq // SEL_LEN
    cast = lambda a: a.astype(MXU_DTYPE)

    tm = min(1024, seq)
    tm_small = min(512, seq)
    tq = 128
    tk = 512

    cos_t, sin_t = _rope_tables(jnp.arange(seq))
    cos_c, sin_c = _rope_tables(jnp.arange(ncp) * CMP_STRIDE + CMP_LEN - 1)
    mselt = _cmp_to_sel_t(ncp, ns)
    onehot = jnp.broadcast_to(jnp.tile(_block_onehot(seq), (batch, 1))[None], (N_KV, m, LANES))
    ones = jnp.ones((N_KV, m, LANES), MXU_DTYPE)

    q_base, ks_base, kw_base, kc_base, vc_base, vs_base, vw_base = 0, 16, 20, 24, 28, 32, 36

    x2 = x.reshape(m, d)
    h = _rmsnorm(x2, norm1_g[0], tm)
    out = None
    for l in range(depth):
        wl = w_in[l]
        o_kv = attn_dim
        kv = [wl[:, o_kv + j * kv_dim:o_kv + (j + 1) * kv_dim] for j in range(6)]
        w_qkv = cast(jnp.concatenate([wl[:, :attn_dim], kv[2], kv[4], kv[0], kv[1], kv[3], kv[5]], axis=1))
        o_ng = o_kv + 6 * kv_dim
        w_ng = wl[:, o_ng:o_ng + n_gate].reshape(d, N_KV, GROUP * N_NSA_BRANCH)
        w_ng = cast(jnp.pad(w_ng, ((0, 0), (0, 0), (0, LANES - GROUP * N_NSA_BRANCH))).reshape(d, N_KV * LANES))
        o_cv = o_ng + n_gate
        w_x, w_b, w_c = [cast(wl[:, o_cv + j * d:o_cv + (j + 1) * d]) for j in range(3)]
        o_mg = o_cv + 3 * d
        w_ga, w_gc = [cast(wl[:, o_mg + j * d:o_mg + (j + 1) * d]) for j in range(2)]

        qkv = _qkv_proj(h, w_qkv, cos_t, sin_t, seq, tm, 512, n_rope_tiles=6, n_q_tiles=4)
        gates = _gate_proj(h, w_ng, tm)

        chunks = qkv[kc_base:kc_base + 2 * N_KV].reshape(2 * N_KV, batch, ncp, CMP_STRIDE * HEAD_DIM)
        pad_pos = lambda p: cast(jnp.pad(p.reshape(1, CMP_LEN * HEAD_DIM), ((0, 7), (0, 0))))
        kc, vc = _compress(chunks, 0, N_KV, cast(cmp_w1_k[l]), cast(cmp_w2_k[l]), pad_pos(cmp_pos_k[l]),
                           cast(cmp_w1_v[l]), cast(cmp_w2_v[l]), pad_pos(cmp_pos_v[l]), cos_c, sin_c, batch)

        ksx = jnp.concatenate([qkv[ks_base:ks_base + N_KV], onehot], axis=-1)
        vsx = jnp.concatenate([qkv[vs_base:vs_base + N_KV], ones], axis=-1)
        vwx = jnp.concatenate([qkv[vw_base:vw_base + N_KV], ones], axis=-1)
        front = lambda a: jnp.pad(a.reshape(N_KV, batch, seq, a.shape[-1]), ((0, 0), (0, 0), (WINDOW, 0), (0, 0)))
        o_attn = _attention(qkv, ksx, vsx, front(qkv[kw_base:kw_base + N_KV]), front(vwx), kc, vc,
                            gates, mselt, batch, seq, q_base, tq, tk)

        cw = jnp.pad(conv_w[l], ((0, 8 - CONV_WIDTH), (0, 0)))
        v_conv = _conv_mixer(h, w_x, w_b, w_c, cw, seq, tm_small, 512)

        merged = _merge(o_attn, v_conv, h, cast(w_attn_proj[l]), cast(w_conv_out[l]), w_ga, w_gc,
                        tm_small, 512)
        x2, h2 = _outproj(merged, cast(w_o[l]), x2, norm2_g[l], tm_small)

        last = l == depth - 1
        g_next = final_g if last else norm1_g[l + 1]
        res = _mlp(h2, cast(w_up[l]), cast(w_down[l]), x2, g_next, tm_small, 512, last)
        if last:
            out = res[0]
        else:
            x2, h = res
    return out.reshape(batch, seq, d)
```

```python
import collections
import functools
import math

import numpy as np
import jax
import jax.numpy as jnp
from jax import lax
from jax.experimental import pallas as pl
from jax.experimental.pallas import tpu as pltpu

N_HEADS = 16
HEAD_DIM = 128
N_KV = 4
GROUP = N_HEADS // N_KV
CMP_LEN = 32
CMP_STRIDE = 16
SEL_LEN = 64
N_SEL = 16
WINDOW = 512
N_NSA_BRANCH = 3
CONV_WIDTH = 3
ROPE_THETA = 10000.0
EPS = 1e-6

MXU_DTYPE = jnp.bfloat16
F32 = jnp.float32
NEG = -1e30
VMEM_LIMIT = 56 * 1024 * 1024
LANES = 128
NT_DIMS = (((1,), (1,)), ((), ()))


def _params(*sem):
    return pltpu.CompilerParams(dimension_semantics=sem, vmem_limit_bytes=VMEM_LIMIT)


def _dot(a, b):
    return jnp.dot(a, b, preferred_element_type=F32)


def _rms(x, g):
    return x * lax.rsqrt(jnp.mean(x * x, axis=-1, keepdims=True) + EPS) * g


def _norm_kernel(x_ref, g_ref, o_ref):
    o_ref[...] = _rms(x_ref[...], g_ref[...]).astype(o_ref.dtype)


def _rmsnorm(x2d, g, tm):
    m, d = x2d.shape
    return pl.pallas_call(
        _norm_kernel,
        grid=(m // tm,),
        in_specs=[pl.BlockSpec((tm, d), lambda i: (i, 0)), pl.BlockSpec((1, d), lambda i: (0, 0))],
        out_specs=pl.BlockSpec((tm, d), lambda i: (i, 0)),
        out_shape=jax.ShapeDtypeStruct((m, d), MXU_DTYPE),
        compiler_params=_params("parallel"),
        name="rmsnorm",
    )(x2d, g.reshape(1, d))


def _qkv_kernel(h_ref, w_ref, cos_ref, sin_ref, o_ref, *, n_rope_tiles, n_q_tiles, q_scale):
    j = pl.program_id(1)
    acc = _dot(h_ref[...], w_ref[...])
    heads = o_ref.shape[0]

    @pl.when(j < n_rope_tiles)
    def _():
        mult = jnp.where(j < n_q_tiles, q_scale, 1.0).astype(F32)
        cos = cos_ref[...] * mult
        sin = sin_ref[...] * mult
        for c in range(heads):
            xc = acc[:, c * LANES:(c + 1) * LANES]
            o_ref[c] = (xc * cos + pltpu.roll(xc, HEAD_DIM // 2, 1) * sin).astype(o_ref.dtype)

    @pl.when(j >= n_rope_tiles)
    def _():
        for c in range(heads):
            o_ref[c] = acc[:, c * LANES:(c + 1) * LANES].astype(o_ref.dtype)


def _qkv_proj(h, w, cos, sin, seq, tm, tn, n_rope_tiles, n_q_tiles):
    m, d = h.shape
    n = w.shape[1]
    hpt = tn // LANES
    tps = seq // tm
    kern = functools.partial(_qkv_kernel, n_rope_tiles=n_rope_tiles, n_q_tiles=n_q_tiles,
                             q_scale=HEAD_DIM ** -0.5 * math.log2(math.e))
    return pl.pallas_call(
        kern,
        grid=(m // tm, n // tn),
        in_specs=[pl.BlockSpec((tm, d), lambda i, j: (i, 0)),
                  pl.BlockSpec((d, tn), lambda i, j: (0, j)),
                  pl.BlockSpec((tm, LANES), lambda i, j: (i % tps, 0)),
                  pl.BlockSpec((tm, LANES), lambda i, j: (i % tps, 0))],
        out_specs=pl.BlockSpec((hpt, tm, LANES), lambda i, j: (j, i, 0)),
        out_shape=jax.ShapeDtypeStruct((n // LANES, m, LANES), MXU_DTYPE),
        compiler_params=_params("parallel", "arbitrary"),
        name="qkv_proj",
    )(h, w, cos, sin)


def _gate_kernel(h_ref, w_ref, o_ref):
    acc = _dot(h_ref[...], w_ref[...])
    for c in range(o_ref.shape[0]):
        o_ref[c] = jax.nn.sigmoid(acc[:, c * LANES:(c + 1) * LANES])


def _gate_proj(h, w, tm):
    m, d = h.shape
    n = w.shape[1]
    return pl.pallas_call(
        _gate_kernel,
        grid=(m // tm,),
        in_specs=[pl.BlockSpec((tm, d), lambda i: (i, 0)), pl.BlockSpec((d, n), lambda i: (0, 0))],
        out_specs=pl.BlockSpec((n // LANES, tm, LANES), lambda i: (0, i, 0)),
        out_shape=jax.ShapeDtypeStruct((n // LANES, m, LANES), F32),
        compiler_params=_params("parallel"),
        name="nsa_gate_proj",
    )(h, w)


def _compress_kernel(ck_ref, cv_ref, w1k_ref, w2k_ref, pk_ref, w1v_ref, w2v_ref, pv_ref,
                     cos_ref, sin_ref, kc_ref, vc_ref):
    nch = ck_ref.shape[0]
    half = w1k_ref.shape[0] // 2

    def phi(c_ref, w1_ref, w2_ref, p_ref):
        c = c_ref[...]
        first = _dot(c, w1_ref[:half, :])
        second = _dot(c, w1_ref[half:, :])
        pos = _dot(p_ref[...], w1_ref[...])[0:1, :]
        hid = first + pltpu.roll(second, nch - 1, 0) + pos
        act = hid * jax.nn.sigmoid(hid)
        return _dot(act.astype(MXU_DTYPE), w2_ref[...])

    kc = phi(ck_ref, w1k_ref, w2k_ref, pk_ref)
    kc = kc * cos_ref[...] + pltpu.roll(kc, HEAD_DIM // 2, 1) * sin_ref[...]
    kc_ref[...] = kc.astype(kc_ref.dtype)
    vc_ref[...] = phi(cv_ref, w1v_ref, w2v_ref, pv_ref).astype(vc_ref.dtype)


def _compress(chunks, kbase, vbase, w1k, w2k, pk, w1v, w2v, pv, cos_c, sin_c, batch):
    _, _, nch, cw = chunks.shape
    const = lambda a: pl.BlockSpec(a.shape, lambda b, g: (0,) * a.ndim)
    out_spec = pl.BlockSpec((None, None, nch, HEAD_DIM), lambda b, g: (b, g, 0, 0))
    out_sds = jax.ShapeDtypeStruct((batch, N_KV, nch, HEAD_DIM), MXU_DTYPE)
    return pl.pallas_call(
        _compress_kernel,
        grid=(batch, N_KV),
        in_specs=[pl.BlockSpec((None, None, nch, cw), lambda b, g: (kbase + g, b, 0, 0)),
                  pl.BlockSpec((None, None, nch, cw), lambda b, g: (vbase + g, b, 0, 0)),
                  const(w1k), const(w2k), const(pk), const(w1v), const(w2v), const(pv),
                  const(cos_c), const(sin_c)],
        out_specs=[out_spec, out_spec],
        out_shape=[out_sds, out_sds],
        compiler_params=_params("parallel", "parallel"),
        name="compress",
    )(chunks, chunks, w1k, w2k, pk, w1v, w2v, pv, cos_c, sin_c)


_Branch = collections.namedtuple("_Branch", "s p m acc bias")


def _attn_kernel(q_ref, kc_ref, vc_ref, ksx_ref, vsx_ref, kw_ref, vwx_ref, gate_ref, mselt_ref,
                 o_ref, qx_scr, qw_scr, sc_scr, pc_scr, sd_scr, pd_scr, sw_scr, pw_scr, sa_scr, sb_scr,
                 pa_scr, pb_scr, bc_scr, bd_scr, bw_scr, mc_scr, mw_scr, m_scr, a_scr,
                 accc_scr, accw_scr, acc_scr, psum_scr, imp_scr, rank_scr, *, tq, tk, rb):
    i = pl.program_id(2)
    q0 = pl.multiple_of(i * tq, tq)
    r = GROUP
    rows = r * tq
    ncp = kc_ref.shape[0]
    ns = mselt_ref.shape[0]
    wlen = WINDOW + tq
    n_chunks = rows // rb
    t_col = q0 + lax.broadcasted_iota(jnp.int32, (tq, 1), 0)
    q = q_ref[...].reshape(rows, HEAD_DIM)
    gates = gate_ref[...]
    head = lambda h: slice(h * tq, (h + 1) * tq)
    gate = lambda h, br: gates[:, h * N_NSA_BRANCH + br:h * N_NSA_BRANCH + br + 1]
    half = rows // 2
    halves = [slice(0, half), slice(half, rows)]
    lane = lambda kk: slice(kk * LANES, (kk + 1) * LANES)

    def load_scores(br, rs, bs, width):
        xs = [br.s[rs, lane(kk)] for kk in range(width // LANES)]
        return [x if b is None else x + b[bs, :] for x, b in zip(xs, br.bias)]

    def softmax_rows(br, width, chunks, first, cmp):
        def slices(c):
            r0 = c * rb
            return slice(r0, r0 + rb), slice(r0 % tq, r0 % tq + rb)

        for c in chunks:
            rs, bs = slices(c)
            xs = load_scores(br, rs, bs, width)
            mx = jnp.max(functools.reduce(jnp.maximum, xs), axis=-1, keepdims=True)
            if first:
                br.m[rs, :] = jnp.broadcast_to(mx, (rb, LANES))
            else:
                m_old = br.m[rs, :]
                m_new = jnp.maximum(m_old, mx)
                a_scr[rs, :] = jnp.exp2(m_old - m_new)
                br.m[rs, :] = m_new

        for c in chunks:
            rs, bs = slices(c)
            xs = load_scores(br, rs, bs, width)
            m = br.m[rs, :]
            ps = [jnp.exp2(x - m) for x in xs]
            if cmp:
                ps = [jnp.where(x > 0.5 * NEG, p, 0.0) for x, p in zip(xs, ps)]
                lsum = jnp.sum(functools.reduce(jnp.add, ps), axis=-1, keepdims=True)
                inv = 1.0 / jnp.where(lsum > 0.0, lsum, 1.0)
                ps = [p * inv for p in ps]
                for kk, p in enumerate(ps):
                    psum_scr[bs, lane(kk)] += p
            elif not first:
                a = a_scr[rs, :]
                br.acc[rs, :LANES] = a * br.acc[rs, :LANES]
                br.acc[rs, LANES:] = a * br.acc[rs, LANES:]
            for kk, p in enumerate(ps):
                br.p[rs, lane(kk)] = p.astype(br.p.dtype)

    def scores(br, width, q_rows, k):
        for hs in halves:
            br.s[hs, :width] = lax.dot_general(q_rows(hs), k, NT_DIMS, preferred_element_type=F32)

    def weighted_values(br, width, v, *, first, cmp=False):
        for hi, hs in enumerate(halves):
            chunks = range(hi * n_chunks // 2, (hi + 1) * n_chunks // 2)
            softmax_rows(br, width, chunks, first, cmp)
            pv = _dot(br.p[hs, :width], v)
            cols = slice(0, v.shape[1])
            if first:
                br.acc[hs, cols] = pv
            else:
                br.acc[hs, cols] += pv

    q_plain = lambda hs: q[hs]
    q_ext = lambda hs: qx_scr[hs, :]
    row_id = lax.broadcasted_iota(jnp.int32, (tq, LANES), 0)
    col_id = lax.broadcasted_iota(jnp.int32, (tq, LANES), 1)

    n_cl = ncp // LANES
    for kk in range(n_cl):
        cmp_end = (col_id + kk * LANES) * CMP_STRIDE + (CMP_LEN - 1)
        bc_scr[:, lane(kk)] = jnp.where(cmp_end <= t_col, 0.0, NEG)
    n_dl = tq // LANES
    for kk in range(n_dl):
        own = (col_id <= row_id - kk * LANES) & (row_id < (kk + 1) * LANES)
        bd_scr[:, lane(kk)] = jnp.where(own, 0.0, NEG)
    n_wl = wlen // LANES
    for kk in range(n_dl):
        bw_scr[:, lane(kk)] = jnp.where(col_id + kk * LANES > row_id, 0.0, NEG)
        bw_scr[:, lane(n_dl + kk)] = jnp.where(col_id + kk * LANES <= row_id, 0.0, NEG)

    cmp_br = _Branch(sc_scr, pc_scr, mc_scr, accc_scr, [bc_scr.at[:, lane(kk)] for kk in range(n_cl)])
    diag_br = _Branch(sd_scr, pd_scr, m_scr, acc_scr, [bd_scr.at[:, lane(kk)] for kk in range(n_dl)])
    win_bias = ([bw_scr.at[:, lane(kk)] for kk in range(n_dl)] + [None] * (n_wl - 2 * n_dl)
                + [bw_scr.at[:, lane(n_dl + kk)] for kk in range(n_dl)])
    win_br = _Branch(sw_scr, pw_scr, mw_scr, accw_scr, win_bias)
    sel_a = _Branch(sa_scr, pa_scr, m_scr, acc_scr, [None] * (tk // LANES))
    sel_b = _Branch(sb_scr, pb_scr, m_scr, acc_scr, [None] * (tk // LANES))

    psum_scr[...] = jnp.zeros(psum_scr.shape, F32)
    pad_col = jnp.where(lax.broadcasted_iota(jnp.int32, (rows, LANES), 1) == 0, NEG, 0.0)
    qw_scr[:, :LANES] = q
    qw_scr[:, LANES:] = pad_col.astype(qw_scr.dtype)
    scores(cmp_br, ncp, q_plain, kc_ref[...])
    scores(win_br, wlen, lambda hs: qw_scr[hs, :], kw_ref[pl.ds(q0, wlen), :])

    weighted_values(cmp_br, ncp, vc_ref[...], first=True, cmp=True)
    scores(diag_br, tq, q_plain, ksx_ref[pl.ds(q0, tq), :LANES])
    mselt = mselt_ref[...]
    imp = jnp.zeros((ns, tq), F32)
    rem = psum_scr[...]
    for _ in range(3):
        piece = rem.astype(MXU_DTYPE)
        imp = imp + lax.dot_general(mselt, piece, NT_DIMS, preferred_element_type=F32)
        rem = rem - piece.astype(F32)
    blk = lax.broadcasted_iota(jnp.int32, (ns, tq), 0)
    tb = (q0 + lax.broadcasted_iota(jnp.int32, (ns, tq), 1)) // SEL_LEN
    forced = (blk == 0) | (blk == tb) | (blk == tb - 1)
    imp_scr[...] = jnp.where(blk <= tb, jnp.where(forced, jnp.inf, imp), -jnp.inf)
    rank_scr[...] = jnp.zeros(rank_scr.shape, F32)

    weighted_values(win_br, wlen, vwx_ref[pl.ds(q0, wlen), :], first=True)
    weighted_values(diag_br, tq, vsx_ref[pl.ds(q0, tq), :], first=True)

    sub = 8
    n_grp = ns // sub
    sub_id = lax.broadcasted_iota(jnp.int32, (sub, tq), 0)
    grp = lambda v: slice(sub * v, sub * (v + 1))
    for gm in range(n_grp):
        @pl.when(sub * gm * SEL_LEN <= q0 + tq - 1)
        def _():
            xs = [imp_scr[grp(v), :] for v in range(n_grp)]
            ranks = [rank_scr[grp(v), :] for v in range(n_grp)]
            for mp in range(sub * gm, sub * (gm + 1)):
                row = jnp.broadcast_to(xs[gm][mp % sub:mp % sub + 1, :], (sub, tq))
                for v, x in enumerate(xs):
                    if sub * v > mp:
                        ahead = row >= x
                    elif sub * v + sub - 1 <= mp:
                        ahead = row > x
                    else:
                        ahead = (row > x) | ((row == x) & (sub_id > mp % sub))
                    ranks[v] = ranks[v] + jnp.where(ahead, 1.0, 0.0)
            for v in range(n_grp):
                rank_scr[grp(v), :] = ranks[v]

    own_start = (q0 + lax.broadcasted_iota(jnp.int32, (ns, tq), 1)) // LANES * (LANES // SEL_LEN)
    off = jnp.where((rank_scr[...] < float(min(N_SEL, ns))) & (blk < own_start), 0.0, NEG)
    off = off.T.astype(MXU_DTYPE)
    if ns < LANES:
        off = jnp.concatenate([off, jnp.zeros((tq, LANES - ns), MXU_DTYPE)], axis=1)
    for h in range(r):
        qx_scr[head(h), :LANES] = q_ref[h]
        qx_scr[head(h), LANES:] = off

    last_tile = ksx_ref.shape[0] // tk - 1

    def tile_rows(t):
        return pl.ds(pl.multiple_of(jnp.minimum(t, last_tile) * tk, tk), tk)

    def sel_pair(jj, carry):
        t = 2 * jj
        scores(sel_b, tk, q_ext, ksx_ref[tile_rows(t + 1), :])
        weighted_values(sel_a, tk, vsx_ref[tile_rows(t), :], first=False)
        scores(sel_a, tk, q_ext, ksx_ref[tile_rows(t + 2), :])
        weighted_values(sel_b, tk, vsx_ref[tile_rows(t + 1), :], first=False)
        return carry

    n_tiles = (q0 + tq - LANES + tk - 1) // tk
    scores(sel_a, tk, q_ext, ksx_ref[tile_rows(0), :])
    lax.fori_loop(0, (n_tiles + 1) // 2, sel_pair, 0)

    o_sel = acc_scr[:, :LANES] * (1.0 / acc_scr[:, LANES:])
    o_win = accw_scr[:, :LANES] * (1.0 / accw_scr[:, LANES:])
    for h in range(r):
        o = (gate(h, 0) * accc_scr[head(h), :] + gate(h, 1) * o_sel[head(h), :]
             + gate(h, 2) * o_win[head(h), :])
        o_ref[:, h * HEAD_DIM:(h + 1) * HEAD_DIM] = o.astype(o_ref.dtype)


def _attention(qkv, ksx, vsx, kwp, vwx, kc, vc, gates, mselt, batch, seq, q_base, tq, tk):
    nq = seq // tq
    ncp = kc.shape[2]
    ns = mselt.shape[0]
    assert ns <= LANES and seq % (2 * tk) == 0 and tk % tq == 0 and WINDOW % tq == 0
    assert tq % LANES == 0 and ncp % LANES == 0
    rows = GROUP * tq
    wlen = WINDOW + tq
    ext_spec = pl.BlockSpec((None, seq, 2 * LANES), lambda b, g, i: (g, b, 0))
    cmp_spec = pl.BlockSpec((None, None, ncp, HEAD_DIM), lambda b, g, i: (b, g, 0, 0))
    f32 = lambda *shape: pltpu.VMEM(shape, F32)
    mxu = lambda *shape: pltpu.VMEM(shape, MXU_DTYPE)
    return pl.pallas_call(
        functools.partial(_attn_kernel, tq=tq, tk=tk, rb=16),
        grid=(batch, N_KV, nq),
        in_specs=[pl.BlockSpec((GROUP, tq, HEAD_DIM), lambda b, g, i: (q_base // GROUP + g, b * nq + i, 0)),
                  cmp_spec, cmp_spec, ext_spec, ext_spec,
                  pl.BlockSpec((None, None, seq + WINDOW, 2 * LANES), lambda b, g, i: (g, b, 0, 0)),
                  pl.BlockSpec((None, None, seq + WINDOW, 2 * LANES), lambda b, g, i: (g, b, 0, 0)),
                  pl.BlockSpec((None, tq, LANES), lambda b, g, i: (g, b * nq + i, 0)),
                  pl.BlockSpec((ns, ncp), lambda b, g, i: (0, 0))],
        out_specs=pl.BlockSpec((tq, GROUP * HEAD_DIM), lambda b, g, i: (b * nq + i, g)),
        out_shape=jax.ShapeDtypeStruct((batch * seq, N_HEADS * HEAD_DIM), MXU_DTYPE),
        scratch_shapes=[mxu(rows, 2 * LANES), mxu(rows, 2 * LANES),
                        f32(rows, ncp), mxu(rows, ncp),
                        f32(rows, tq), mxu(rows, tq),
                        f32(rows, wlen), mxu(rows, wlen),
                        f32(rows, tk), f32(rows, tk),
                        mxu(rows, tk), mxu(rows, tk),
                        f32(tq, ncp), f32(tq, tq), f32(tq, 2 * tq),
                        f32(rows, LANES), f32(rows, LANES), f32(rows, LANES),
                        f32(rows, LANES),
                        f32(rows, LANES),
                        f32(rows, 2 * LANES), f32(rows, 2 * LANES),
                        f32(tq, ncp),
                        f32(ns, tq), f32(ns, tq)],
        compiler_params=_params("parallel", "parallel", "arbitrary"),
        name="nsa_attention",
    )(qkv, kc, vc, ksx, vsx, kwp, vwx, gates, mselt)


def _conv_kernel(h_ref, wx_ref, wb_ref, wc_ref, cw_ref, o_ref, ubuf, *, tiles_per_seq):
    i = pl.program_id(1)
    tm = h_ref.shape[0]
    h = h_ref[...]
    x_in = _dot(h, wx_ref[...])
    gate_b = _dot(h, wb_ref[...])
    gate_c = _dot(h, wc_ref[...])

    @pl.when(i % tiles_per_seq == 0)
    def _():
        ubuf[0:8, :] = jnp.zeros((8, ubuf.shape[1]), F32)

    ubuf[8:tm + 8, :] = gate_c * x_in
    w = cw_ref[...]
    conv = (w[2:3, :] * ubuf[8:tm + 8, :] + w[1:2, :] * ubuf[7:tm + 7, :]
            + w[0:1, :] * ubuf[6:tm + 6, :])
    o_ref[...] = (gate_b * conv).astype(o_ref.dtype)
    ubuf[0:8, :] = ubuf[tm:tm + 8, :]


def _conv_mixer(h, wx, wb, wc, cw, seq, tm, tn):
    m, d = h.shape
    n = wx.shape[1]
    wspec = pl.BlockSpec((d, tn), lambda j, i: (0, j))
    return pl.pallas_call(
        functools.partial(_conv_kernel, tiles_per_seq=seq // tm),
        grid=(n // tn, m // tm),
        in_specs=[pl.BlockSpec((tm, d), lambda j, i: (i, 0)), wspec, wspec, wspec,
                  pl.BlockSpec((8, tn), lambda j, i: (0, j))],
        out_specs=pl.BlockSpec((tm, tn), lambda j, i: (i, j)),
        out_shape=jax.ShapeDtypeStruct((m, n), MXU_DTYPE),
        scratch_shapes=[pltpu.VMEM((tm + 8, tn), F32)],
        compiler_params=_params("parallel", "arbitrary"),
        name="conv_mixer",
    )(h, wx, wb, wc, cw)


def _merge_kernel(oa_ref, v_ref, h_ref, wap_ref, wco_ref, wga_ref, wgc_ref, o_ref):
    h = h_ref[...]
    y_attn = _dot(oa_ref[...], wap_ref[...])
    y_conv = _dot(v_ref[...], wco_ref[...])
    g_attn = jax.nn.sigmoid(_dot(h, wga_ref[...]))
    g_conv = jax.nn.sigmoid(_dot(h, wgc_ref[...]))
    o_ref[...] = (g_attn * y_attn + g_conv * y_conv).astype(o_ref.dtype)


def _merge(oa, v, h, wap, wco, wga, wgc, tm, tn):
    m, d = h.shape
    n = wap.shape[1]
    aspec = pl.BlockSpec((tm, d), lambda i, j: (i, 0))
    wspec = pl.BlockSpec((d, tn), lambda i, j: (0, j))
    return pl.pallas_call(
        _merge_kernel,
        grid=(m // tm, n // tn),
        in_specs=[aspec, aspec, aspec, wspec, wspec, wspec, wspec],
        out_specs=pl.BlockSpec((tm, tn), lambda i, j: (i, j)),
        out_shape=jax.ShapeDtypeStruct((m, n), MXU_DTYPE),
        compiler_params=_params("parallel", "arbitrary"),
        name="gated_merge",
    )(oa, v, h, wap, wco, wga, wgc)


def _outproj_kernel(a_ref, w_ref, x_ref, g_ref, xo_ref, ho_ref):
    x = x_ref[...] + _dot(a_ref[...], w_ref[...])
    xo_ref[...] = x
    ho_ref[...] = _rms(x, g_ref[...]).astype(ho_ref.dtype)


def _outproj(a, w, x, g, tm):
    m, d = x.shape
    row = pl.BlockSpec((tm, d), lambda i: (i, 0))
    return pl.pallas_call(
        _outproj_kernel,
        grid=(m // tm,),
        in_specs=[row, pl.BlockSpec((d, d), lambda i: (0, 0)), row,
                  pl.BlockSpec((1, d), lambda i: (0, 0))],
        out_specs=[row, row],
        out_shape=[jax.ShapeDtypeStruct((m, d), F32), jax.ShapeDtypeStruct((m, d), MXU_DTYPE)],
        compiler_params=_params("parallel"),
        name="out_proj",
    )(a, w, x, g.reshape(1, d))


def _mlp_kernel(h_ref, wu_ref, wd_ref, x_ref, g_ref, *refs, last):
    acc = refs[-1]
    f = pl.program_id(1)

    @pl.when(f == 0)
    def _():
        acc[...] = x_ref[...]

    a = jnp.maximum(_dot(h_ref[...], wu_ref[...]), 0.0)
    acc[...] += _dot((a * a).astype(MXU_DTYPE), wd_ref[...])

    @pl.when(f == pl.num_programs(1) - 1)
    def _():
        x = acc[...]
        normed = _rms(x, g_ref[...])
        if last:
            refs[0][...] = normed
        else:
            refs[0][...] = x
            refs[1][...] = normed.astype(refs[1].dtype)


def _mlp(h, wu, wd, x, g, tm, tf, last):
    m, d = x.shape
    ff = wu.shape[1]
    row = pl.BlockSpec((tm, d), lambda i, f: (i, 0))
    if last:
        out_specs, out_shape = [row], [jax.ShapeDtypeStruct((m, d), F32)]
    else:
        out_specs = [row, row]
        out_shape = [jax.ShapeDtypeStruct((m, d), F32), jax.ShapeDtypeStruct((m, d), MXU_DTYPE)]
    return pl.pallas_call(
        functools.partial(_mlp_kernel, last=last),
        grid=(m // tm, ff // tf),
        in_specs=[row, pl.BlockSpec((d, tf), lambda i, f: (0, f)),
                  pl.BlockSpec((tf, d), lambda i, f: (f, 0)), row,
                  pl.BlockSpec((1, d), lambda i, f: (0, 0))],
        out_specs=out_specs,
        out_shape=out_shape,
        scratch_shapes=[pltpu.VMEM((tm, d), F32)],
        compiler_params=_params("parallel", "arbitrary"),
        name="relu2_mlp",
    )(h, wu, wd, x, g.reshape(1, d))


def _rope_tables(pos):
    half = HEAD_DIM // 2
    inv_freq = jnp.exp(-math.log(ROPE_THETA) * jnp.arange(half, dtype=F32) / half)
    ang = pos.astype(F32)[:, None] * inv_freq[None, :]
    cos, sin = jnp.cos(ang), jnp.sin(ang)
    return jnp.concatenate([cos, cos], axis=-1), jnp.concatenate([-sin, sin], axis=-1)


def _cmp_to_sel_t(ncp, ns):
    nc = ncp - 1
    cs = np.arange(nc) * CMP_STRIDE
    ss = np.arange(ns) * SEL_LEN
    ov = np.minimum(cs[:, None] + CMP_LEN, ss[None, :] + SEL_LEN) - np.maximum(cs[:, None], ss[None, :])
    m = np.zeros((ncp, ns), np.float32)
    m[:nc] = np.clip(ov, 0, None) / CMP_LEN
    return jnp.asarray(m.T, dtype=MXU_DTYPE)


def _block_onehot(seq):
    e = (np.arange(seq)[:, None] // SEL_LEN == np.arange(LANES)[None, :]).astype(np.float32)
    return jnp.asarray(e, dtype=MXU_DTYPE)


def kernel(x, norm1_g, w_in, cmp_pos_k, cmp_w1_k, cmp_w2_k, cmp_pos_v, cmp_w1_v, cmp_w2_v,
           conv_w, w_attn_proj, w_conv_out, w_o, norm2_g, w_up, w_down, final_g):
    batch, seq, d = x.shape
    depth = w_in.shape[0]
    m = batch * seq
    attn_dim = N_HEADS * HEAD_DIM
    kv_dim = N_KV * HEAD_DIM
    n_gate = N_HEADS * N_NSA_BRANCH
    ncp = seq // CMP_STRIDE
    ns = seq // SEL_LEN
    cast = lambda a: a.astype(MXU_DTYPE)

    tm = min(1024, seq)
    tm_small = min(512, seq)
    tq = 256
    tk = 512

    cos_t, sin_t = _rope_tables(jnp.arange(seq))
    cos_c, sin_c = _rope_tables(jnp.arange(ncp) * CMP_STRIDE + CMP_LEN - 1)
    mselt = _cmp_to_sel_t(ncp, ns)
    onehot = jnp.broadcast_to(jnp.tile(_block_onehot(seq), (batch, 1))[None], (N_KV, m, LANES))
    ones = jnp.ones((N_KV, m, LANES), MXU_DTYPE)
    before_start = jnp.asarray(np.arange(2 * LANES) == LANES, dtype=MXU_DTYPE)

    q_base, ks_base, kw_base, kc_base, vc_base, vs_base, vw_base = 0, 16, 20, 24, 28, 32, 36

    x2 = x.reshape(m, d)
    h = _rmsnorm(x2, norm1_g[0], tm)
    out = None
    for l in range(depth):
        wl = w_in[l]
        o_kv = attn_dim
        kv = [wl[:, o_kv + j * kv_dim:o_kv + (j + 1) * kv_dim] for j in range(6)]
        w_qkv = cast(jnp.concatenate([wl[:, :attn_dim], kv[2], kv[4], kv[0], kv[1], kv[3], kv[5]], axis=1))
        o_ng = o_kv + 6 * kv_dim
        w_ng = wl[:, o_ng:o_ng + n_gate].reshape(d, N_KV, GROUP * N_NSA_BRANCH)
        w_ng = cast(jnp.pad(w_ng, ((0, 0), (0, 0), (0, LANES - GROUP * N_NSA_BRANCH))).reshape(d, N_KV * LANES))
        o_cv = o_ng + n_gate
        w_x, w_b, w_c = [cast(wl[:, o_cv + j * d:o_cv + (j + 1) * d]) for j in range(3)]
        o_mg = o_cv + 3 * d
        w_ga, w_gc = [cast(wl[:, o_mg + j * d:o_mg + (j + 1) * d]) for j in range(2)]

        qkv = _qkv_proj(h, w_qkv, cos_t, sin_t, seq, tm, 512, n_rope_tiles=6, n_q_tiles=4)
        gates = _gate_proj(h, w_ng, tm)

        chunks = qkv[kc_base:kc_base + 2 * N_KV].reshape(2 * N_KV, batch, ncp, CMP_STRIDE * HEAD_DIM)
        pad_pos = lambda p: cast(jnp.pad(p.reshape(1, CMP_LEN * HEAD_DIM), ((0, 7), (0, 0))))
        kc, vc = _compress(chunks, 0, N_KV, cast(cmp_w1_k[l]), cast(cmp_w2_k[l]), pad_pos(cmp_pos_k[l]),
                           cast(cmp_w1_v[l]), cast(cmp_w2_v[l]), pad_pos(cmp_pos_v[l]), cos_c, sin_c, batch)

        ksx = jnp.concatenate([qkv[ks_base:ks_base + N_KV], onehot], axis=-1)
        vsx = jnp.concatenate([qkv[vs_base:vs_base + N_KV], ones], axis=-1)
        vwx = jnp.concatenate([qkv[vw_base:vw_base + N_KV], ones], axis=-1)
        front = lambda a, rows: jnp.concatenate(
            [jnp.broadcast_to(rows, (N_KV, batch, WINDOW, 2 * LANES)),
             a.reshape(N_KV, batch, seq, 2 * LANES)], axis=2)
        kwx = jnp.concatenate([qkv[kw_base:kw_base + N_KV], jnp.zeros_like(ones)], axis=-1)
        o_attn = _attention(qkv, ksx, vsx, front(kwx, before_start), front(vwx, jnp.zeros((), MXU_DTYPE)),
                            kc, vc, gates, mselt, batch, seq, q_base, tq, tk)

        cw = jnp.pad(conv_w[l], ((0, 8 - CONV_WIDTH), (0, 0)))
        v_conv = _conv_mixer(h, w_x, w_b, w_c, cw, seq, tm_small, 512)

        merged = _merge(o_attn, v_conv, h, cast(w_attn_proj[l]), cast(w_conv_out[l]), w_ga, w_gc,
                        tm_small, 512)
        x2, h2 = _outproj(merged, cast(w_o[l]), x2, norm2_g[l], tm_small)

        last = l == depth - 1
        g_next = final_g if last else norm1_g[l + 1]
        res = _mlp(h2, cast(w_up[l]), cast(w_down[l]), x2, g_next, tm_small, 512, last)
        if last:
            out = res[0]
        else:
            x2, h = res
    return out.reshape(batch, seq, d)
```

```python
import collections
import functools
import math

import numpy as np
import jax
import jax.numpy as jnp
from jax import lax
from jax.experimental import pallas as pl
from jax.experimental.pallas import tpu as pltpu

N_HEADS = 16
HEAD_DIM = 128
N_KV = 4
GROUP = N_HEADS // N_KV
CMP_LEN = 32
CMP_STRIDE = 16
SEL_LEN = 64
N_SEL = 16
WINDOW = 512
N_NSA_BRANCH = 3
CONV_WIDTH = 3
ROPE_THETA = 10000.0
EPS = 1e-6

MXU_DTYPE = jnp.bfloat16
F32 = jnp.float32
NEG = -1e30
VMEM_LIMIT = 56 * 1024 * 1024
LANES = 128
NT_DIMS = (((1,), (1,)), ((), ()))


def _params(*sem):
    return pltpu.CompilerParams(dimension_semantics=sem, vmem_limit_bytes=VMEM_LIMIT)


def _dot(a, b):
    return jnp.dot(a, b, preferred_element_type=F32)


def _rms(x, g):
    return x * lax.rsqrt(jnp.mean(x * x, axis=-1, keepdims=True) + EPS) * g


def _norm_kernel(x_ref, g_ref, o_ref):
    o_ref[...] = _rms(x_ref[...], g_ref[...]).astype(o_ref.dtype)


def _rmsnorm(x2d, g, tm):
    m, d = x2d.shape
    return pl.pallas_call(
        _norm_kernel,
        grid=(m // tm,),
        in_specs=[pl.BlockSpec((tm, d), lambda i: (i, 0)), pl.BlockSpec((1, d), lambda i: (0, 0))],
        out_specs=pl.BlockSpec((tm, d), lambda i: (i, 0)),
        out_shape=jax.ShapeDtypeStruct((m, d), MXU_DTYPE),
        compiler_params=_params("parallel"),
        name="rmsnorm",
    )(x2d, g.reshape(1, d))


QKV_TILES = ("q", "q", "q", "q", "k_sel", "k_win", "k_cmp", "v_cmp", "v_sel", "v_win")


def _qkv_kernel(h_ref, w_ref, cos_ref, sin_ref, ksx_in, vsx_in, kwx_in, vwx_in,
                o_ref, ksx_ref, vsx_ref, kwx_ref, vwx_ref, *, q_scale):
    del ksx_in, vsx_in, kwx_in, vwx_in
    j = pl.program_id(1)
    acc = _dot(h_ref[...], w_ref[...])
    dest = {"q": o_ref, "k_cmp": o_ref, "v_cmp": o_ref, "k_sel": ksx_ref, "v_sel": vsx_ref,
            "k_win": kwx_ref, "v_win": vwx_ref}

    def put(ref, rope, scale):
        for c in range(ref.shape[0]):
            xc = acc[:, c * LANES:(c + 1) * LANES]
            if rope:
                xc = xc * (cos_ref[...] * scale) + pltpu.roll(xc, HEAD_DIM // 2, 1) * (sin_ref[...] * scale)
            ref[c] = xc.astype(ref.dtype)

    for name in dict.fromkeys(QKV_TILES):
        tiles = [t for t, n in enumerate(QKV_TILES) if n == name]

        @pl.when((j >= tiles[0]) & (j <= tiles[-1]))
        def _():
            put(dest[name], rope=name in ("q", "k_sel", "k_win"), scale=q_scale if name == "q" else 1.0)


def _qkv_proj(h, w, cos, sin, ksx, vsx, kwx, vwx, seq, tm):
    m, d = h.shape
    tn = N_KV * LANES
    assert w.shape[1] == tn * len(QKV_TILES) and seq % tm == 0 and WINDOW % tm == 0
    tps = seq // tm
    main_block = lambda j: jnp.minimum(j, 3) + (j >= 6).astype(jnp.int32) + (j >= 7).astype(jnp.int32)
    ext_spec = pl.BlockSpec((N_KV, tm, LANES), lambda i, j: (0, i, 0))
    win_spec = pl.BlockSpec((N_KV, None, tm, LANES), lambda i, j: (0, i // tps, WINDOW // tm + i % tps, 0))
    any_spec = pl.BlockSpec(memory_space=pl.ANY)
    sds = lambda a: jax.ShapeDtypeStruct(a.shape, a.dtype)
    kern = functools.partial(_qkv_kernel, q_scale=HEAD_DIM ** -0.5 * math.log2(math.e))
    return pl.pallas_call(
        kern,
        grid=(m // tm, len(QKV_TILES)),
        in_specs=[pl.BlockSpec((tm, d), lambda i, j: (i, 0)),
                  pl.BlockSpec((d, tn), lambda i, j: (0, j)),
                  pl.BlockSpec((tm, LANES), lambda i, j: (i % tps, 0)),
                  pl.BlockSpec((tm, LANES), lambda i, j: (i % tps, 0)),
                  any_spec, any_spec, any_spec, any_spec],
        out_specs=[pl.BlockSpec((N_KV, tm, LANES), lambda i, j: (main_block(j), i, 0)),
                   ext_spec, ext_spec, win_spec, win_spec],
        out_shape=[jax.ShapeDtypeStruct((6 * N_KV, m, LANES), MXU_DTYPE),
                   sds(ksx), sds(vsx), sds(kwx), sds(vwx)],
        input_output_aliases={4: 1, 5: 2, 6: 3, 7: 4},
        compiler_params=_params("parallel", "arbitrary"),
        name="qkv_proj",
    )(h, w, cos, sin, ksx, vsx, kwx, vwx)


def _gate_kernel(h_ref, w_ref, o_ref):
    acc = _dot(h_ref[...], w_ref[...])
    for c in range(o_ref.shape[0]):
        o_ref[c] = jax.nn.sigmoid(acc[:, c * LANES:(c + 1) * LANES])


def _gate_proj(h, w, tm):
    m, d = h.shape
    n = w.shape[1]
    return pl.pallas_call(
        _gate_kernel,
        grid=(m // tm,),
        in_specs=[pl.BlockSpec((tm, d), lambda i: (i, 0)), pl.BlockSpec((d, n), lambda i: (0, 0))],
        out_specs=pl.BlockSpec((n // LANES, tm, LANES), lambda i: (0, i, 0)),
        out_shape=jax.ShapeDtypeStruct((n // LANES, m, LANES), F32),
        compiler_params=_params("parallel"),
        name="nsa_gate_proj",
    )(h, w)


def _compress_kernel(ck_ref, cv_ref, w1k_ref, w2k_ref, pk_ref, w1v_ref, w2v_ref, pv_ref,
                     cos_ref, sin_ref, kc_ref, vc_ref):
    nch = ck_ref.shape[0]
    half = w1k_ref.shape[0] // 2

    def phi(c_ref, w1_ref, w2_ref, p_ref):
        c = c_ref[...]
        first = _dot(c, w1_ref[:half, :])
        second = _dot(c, w1_ref[half:, :])
        pos = _dot(p_ref[...], w1_ref[...])[0:1, :]
        hid = first + pltpu.roll(second, nch - 1, 0) + pos
        act = hid * jax.nn.sigmoid(hid)
        return _dot(act.astype(MXU_DTYPE), w2_ref[...])

    kc = phi(ck_ref, w1k_ref, w2k_ref, pk_ref)
    kc = kc * cos_ref[...] + pltpu.roll(kc, HEAD_DIM // 2, 1) * sin_ref[...]
    kc_ref[...] = kc.astype(kc_ref.dtype)
    vc_ref[...] = phi(cv_ref, w1v_ref, w2v_ref, pv_ref).astype(vc_ref.dtype)


def _compress(chunks, kbase, vbase, w1k, w2k, pk, w1v, w2v, pv, cos_c, sin_c, batch):
    _, _, nch, cw = chunks.shape
    const = lambda a: pl.BlockSpec(a.shape, lambda b, g: (0,) * a.ndim)
    out_spec = pl.BlockSpec((None, None, nch, HEAD_DIM), lambda b, g: (b, g, 0, 0))
    out_sds = jax.ShapeDtypeStruct((batch, N_KV, nch, HEAD_DIM), MXU_DTYPE)
    return pl.pallas_call(
        _compress_kernel,
        grid=(batch, N_KV),
        in_specs=[pl.BlockSpec((None, None, nch, cw), lambda b, g: (kbase + g, b, 0, 0)),
                  pl.BlockSpec((None, None, nch, cw), lambda b, g: (vbase + g, b, 0, 0)),
                  const(w1k), const(w2k), const(pk), const(w1v), const(w2v), const(pv),
                  const(cos_c), const(sin_c)],
        out_specs=[out_spec, out_spec],
        out_shape=[out_sds, out_sds],
        compiler_params=_params("parallel", "parallel"),
        name="compress",
    )(chunks, chunks, w1k, w2k, pk, w1v, w2v, pv, cos_c, sin_c)


_Branch = collections.namedtuple("_Branch", "s p m acc bias")


def _attn_kernel(q_ref, kc_ref, vc_ref, ksx_ref, vsx_ref, kw_ref, vwx_ref, gate_ref, mselt_ref,
                 o_ref, qx_scr, qw_scr, sc_scr, pc_scr, sd_scr, pd_scr, sw_scr, pw_scr, sa_scr, sb_scr,
                 pa_scr, pb_scr, bc_scr, bd_scr, bw_scr, mc_scr, mw_scr, m_scr, a_scr,
                 accc_scr, accw_scr, acc_scr, psum_scr, imp_scr, rank_scr, *, tq, tk, rb):
    i = pl.program_id(2)
    q0 = pl.multiple_of(i * tq, tq)
    r = GROUP
    rows = r * tq
    ncp = kc_ref.shape[0]
    ns = mselt_ref.shape[0]
    wlen = WINDOW + tq
    n_chunks = rows // rb
    t_col = q0 + lax.broadcasted_iota(jnp.int32, (tq, 1), 0)
    q = q_ref[...].reshape(rows, HEAD_DIM)
    gates = gate_ref[...]
    head = lambda h: slice(h * tq, (h + 1) * tq)
    gate = lambda h, br: gates[:, h * N_NSA_BRANCH + br:h * N_NSA_BRANCH + br + 1]
    half = rows // 2
    halves = [slice(0, half), slice(half, rows)]
    lane = lambda kk: slice(kk * LANES, (kk + 1) * LANES)

    def load_scores(br, rs, bs, width):
        xs = [br.s[rs, lane(kk)] for kk in range(width // LANES)]
        return [x if b is None else x + b[bs, :] for x, b in zip(xs, br.bias)]

    def softmax_rows(br, width, chunks, first, cmp):
        def slices(c):
            r0 = c * rb
            return slice(r0, r0 + rb), slice(r0 % tq, r0 % tq + rb)

        for c in chunks:
            rs, bs = slices(c)
            xs = load_scores(br, rs, bs, width)
            mx = jnp.max(functools.reduce(jnp.maximum, xs), axis=-1, keepdims=True)
            if first:
                br.m[rs, :] = jnp.broadcast_to(mx, (rb, LANES))
            else:
                m_old = br.m[rs, :]
                m_new = jnp.maximum(m_old, mx)
                a_scr[rs, :] = jnp.exp2(m_old - m_new)
                br.m[rs, :] = m_new

        for c in chunks:
            rs, bs = slices(c)
            xs = load_scores(br, rs, bs, width)
            m = br.m[rs, :]
            ps = [jnp.exp2(x - m) for x in xs]
            if cmp:
                ps = [jnp.where(x > 0.5 * NEG, p, 0.0) for x, p in zip(xs, ps)]
                lsum = jnp.sum(functools.reduce(jnp.add, ps), axis=-1, keepdims=True)
                inv = 1.0 / jnp.where(lsum > 0.0, lsum, 1.0)
                ps = [p * inv for p in ps]
                for kk, p in enumerate(ps):
                    psum_scr[bs, lane(kk)] += p
            elif not first:
                a = a_scr[rs, :]
                br.acc[rs, :LANES] = a * br.acc[rs, :LANES]
                br.acc[rs, LANES:] = a * br.acc[rs, LANES:]
            for kk, p in enumerate(ps):
                br.p[rs, lane(kk)] = p.astype(br.p.dtype)

    def scores(br, width, q_rows, k):
        for hs in halves:
            br.s[hs, :width] = lax.dot_general(q_rows(hs), k, NT_DIMS, preferred_element_type=F32)

    def weighted_values(br, width, v, *, first, cmp=False):
        for hi, hs in enumerate(halves):
            chunks = range(hi * n_chunks // 2, (hi + 1) * n_chunks // 2)
            softmax_rows(br, width, chunks, first, cmp)
            pv = _dot(br.p[hs, :width], v)
            cols = slice(0, v.shape[1])
            if first:
                br.acc[hs, cols] = pv
            else:
                br.acc[hs, cols] += pv

    q_plain = lambda hs: q[hs]
    q_ext = lambda hs: qx_scr[hs, :]
    row_id = lax.broadcasted_iota(jnp.int32, (tq, LANES), 0)
    col_id = lax.broadcasted_iota(jnp.int32, (tq, LANES), 1)

    n_cl = ncp // LANES
    for kk in range(n_cl):
        cmp_end = (col_id + kk * LANES) * CMP_STRIDE + (CMP_LEN - 1)
        bc_scr[:, lane(kk)] = jnp.where(cmp_end <= t_col, 0.0, NEG)
    n_dl = tq // LANES
    for kk in range(n_dl):
        own = (col_id <= row_id - kk * LANES) & (row_id < (kk + 1) * LANES)
        bd_scr[:, lane(kk)] = jnp.where(own, 0.0, NEG)
    n_wl = wlen // LANES
    for kk in range(n_dl):
        bw_scr[:, lane(kk)] = jnp.where(col_id + kk * LANES > row_id, 0.0, NEG)
        bw_scr[:, lane(n_dl + kk)] = jnp.where(col_id + kk * LANES <= row_id, 0.0, NEG)

    cmp_br = _Branch(sc_scr, pc_scr, mc_scr, accc_scr, [bc_scr.at[:, lane(kk)] for kk in range(n_cl)])
    diag_br = _Branch(sd_scr, pd_scr, m_scr, acc_scr, [bd_scr.at[:, lane(kk)] for kk in range(n_dl)])
    win_bias = ([bw_scr.at[:, lane(kk)] for kk in range(n_dl)] + [None] * (n_wl - 2 * n_dl)
                + [bw_scr.at[:, lane(n_dl + kk)] for kk in range(n_dl)])
    win_br = _Branch(sw_scr, pw_scr, mw_scr, accw_scr, win_bias)
    sel_a = _Branch(sa_scr, pa_scr, m_scr, acc_scr, [None] * (tk // LANES))
    sel_b = _Branch(sb_scr, pb_scr, m_scr, acc_scr, [None] * (tk // LANES))

    psum_scr[...] = jnp.zeros(psum_scr.shape, F32)
    pad_col = jnp.where(lax.broadcasted_iota(jnp.int32, (rows, LANES), 1) == 0, NEG, 0.0)
    qw_scr[:, :LANES] = q
    qw_scr[:, LANES:] = pad_col.astype(qw_scr.dtype)
    scores(cmp_br, ncp, q_plain, kc_ref[...])
    scores(win_br, wlen, lambda hs: qw_scr[hs, :], kw_ref[pl.ds(q0, wlen), :])

    weighted_values(cmp_br, ncp, vc_ref[...], first=True, cmp=True)
    scores(diag_br, tq, q_plain, ksx_ref[pl.ds(q0, tq), :LANES])
    mselt = mselt_ref[...]
    imp = jnp.zeros((ns, tq), F32)
    rem = psum_scr[...]
    for _ in range(3):
        piece = rem.astype(MXU_DTYPE)
        imp = imp + lax.dot_general(mselt, piece, NT_DIMS, preferred_element_type=F32)
        rem = rem - piece.astype(F32)
    blk = lax.broadcasted_iota(jnp.int32, (ns, tq), 0)
    tb = (q0 + lax.broadcasted_iota(jnp.int32, (ns, tq), 1)) // SEL_LEN
    forced = (blk == 0) | (blk == tb) | (blk == tb - 1)
    imp_scr[...] = jnp.where(blk <= tb, jnp.where(forced, jnp.inf, imp), -jnp.inf)
    rank_scr[...] = jnp.zeros(rank_scr.shape, F32)

    weighted_values(win_br, wlen, vwx_ref[pl.ds(q0, wlen), :], first=True)
    weighted_values(diag_br, tq, vsx_ref[pl.ds(q0, tq), :], first=True)

    sub = 8
    n_grp = ns // sub
    sub_id = lax.broadcasted_iota(jnp.int32, (sub, tq), 0)
    grp = lambda v: slice(sub * v, sub * (v + 1))
    for gm in range(n_grp):
        @pl.when(sub * gm * SEL_LEN <= q0 + tq - 1)
        def _():
            xs = [imp_scr[grp(v), :] for v in range(n_grp)]
            ranks = [rank_scr[grp(v), :] for v in range(n_grp)]
            for mp in range(sub * gm, sub * (gm + 1)):
                row = jnp.broadcast_to(xs[gm][mp % sub:mp % sub + 1, :], (sub, tq))
                for v, x in enumerate(xs):
                    if sub * v > mp:
                        ahead = row >= x
                    elif sub * v + sub - 1 <= mp:
                        ahead = row > x
                    else:
                        ahead = (row > x) | ((row == x) & (sub_id > mp % sub))
                    ranks[v] = ranks[v] + jnp.where(ahead, 1.0, 0.0)
            for v in range(n_grp):
                rank_scr[grp(v), :] = ranks[v]

    own_start = (q0 + lax.broadcasted_iota(jnp.int32, (ns, tq), 1)) // LANES * (LANES // SEL_LEN)
    off = jnp.where((rank_scr[...] < float(min(N_SEL, ns))) & (blk < own_start), 0.0, NEG)
    off = off.T.astype(MXU_DTYPE)
    if ns < LANES:
        off = jnp.concatenate([off, jnp.zeros((tq, LANES - ns), MXU_DTYPE)], axis=1)
    for h in range(r):
        qx_scr[head(h), :LANES] = q_ref[h]
        qx_scr[head(h), LANES:] = off

    last_tile = ksx_ref.shape[0] // tk - 1

    def tile_rows(t):
        return pl.ds(pl.multiple_of(jnp.minimum(t, last_tile) * tk, tk), tk)

    def sel_pair(jj, carry):
        t = 2 * jj
        scores(sel_b, tk, q_ext, ksx_ref[tile_rows(t + 1), :])
        weighted_values(sel_a, tk, vsx_ref[tile_rows(t), :], first=False)
        scores(sel_a, tk, q_ext, ksx_ref[tile_rows(t + 2), :])
        weighted_values(sel_b, tk, vsx_ref[tile_rows(t + 1), :], first=False)
        return carry

    n_tiles = (q0 + tq - LANES + tk - 1) // tk
    scores(sel_a, tk, q_ext, ksx_ref[tile_rows(0), :])
    lax.fori_loop(0, (n_tiles + 1) // 2, sel_pair, 0)

    o_sel = acc_scr[:, :LANES] * (1.0 / acc_scr[:, LANES:])
    o_win = accw_scr[:, :LANES] * (1.0 / accw_scr[:, LANES:])
    for h in range(r):
        o = (gate(h, 0) * accc_scr[head(h), :] + gate(h, 1) * o_sel[head(h), :]
             + gate(h, 2) * o_win[head(h), :])
        o_ref[:, h * HEAD_DIM:(h + 1) * HEAD_DIM] = o.astype(o_ref.dtype)


def _attention(qkv, ksx, vsx, kwp, vwx, kc, vc, gates, mselt, batch, seq, q_base, tq, tk):
    nq = seq // tq
    ncp = kc.shape[2]
    ns = mselt.shape[0]
    assert ns <= LANES and seq % (2 * tk) == 0 and tk % tq == 0 and WINDOW % tq == 0
    assert tq % LANES == 0 and ncp % LANES == 0
    rows = GROUP * tq
    wlen = WINDOW + tq
    ext_spec = pl.BlockSpec((None, seq, 2 * LANES), lambda b, g, i: (g, b, 0))
    cmp_spec = pl.BlockSpec((None, None, ncp, HEAD_DIM), lambda b, g, i: (b, g, 0, 0))
    f32 = lambda *shape: pltpu.VMEM(shape, F32)
    mxu = lambda *shape: pltpu.VMEM(shape, MXU_DTYPE)
    return pl.pallas_call(
        functools.partial(_attn_kernel, tq=tq, tk=tk, rb=16),
        grid=(batch, N_KV, nq),
        in_specs=[pl.BlockSpec((GROUP, tq, HEAD_DIM), lambda b, g, i: (q_base // GROUP + g, b * nq + i, 0)),
                  cmp_spec, cmp_spec, ext_spec, ext_spec,
                  pl.BlockSpec((None, None, seq + WINDOW, 2 * LANES), lambda b, g, i: (g, b, 0, 0)),
                  pl.BlockSpec((None, None, seq + WINDOW, 2 * LANES), lambda b, g, i: (g, b, 0, 0)),
                  pl.BlockSpec((None, tq, LANES), lambda b, g, i: (g, b * nq + i, 0)),
                  pl.BlockSpec((ns, ncp), lambda b, g, i: (0, 0))],
        out_specs=pl.BlockSpec((tq, GROUP * HEAD_DIM), lambda b, g, i: (b * nq + i, g)),
        out_shape=jax.ShapeDtypeStruct((batch * seq, N_HEADS * HEAD_DIM), MXU_DTYPE),
        scratch_shapes=[mxu(rows, 2 * LANES), mxu(rows, 2 * LANES),
                        f32(rows, ncp), mxu(rows, ncp),
                        f32(rows, tq), mxu(rows, tq),
                        f32(rows, wlen), mxu(rows, wlen),
                        f32(rows, tk), f32(rows, tk),
                        mxu(rows, tk), mxu(rows, tk),
                        f32(tq, ncp), f32(tq, tq), f32(tq, 2 * tq),
                        f32(rows, LANES), f32(rows, LANES), f32(rows, LANES),
                        f32(rows, LANES),
                        f32(rows, LANES),
                        f32(rows, 2 * LANES), f32(rows, 2 * LANES),
                        f32(tq, ncp),
                        f32(ns, tq), f32(ns, tq)],
        compiler_params=_params("parallel", "parallel", "arbitrary"),
        name="nsa_attention",
    )(qkv, kc, vc, ksx, vsx, kwp, vwx, gates, mselt)


def _conv_kernel(h_ref, wx_ref, wb_ref, wc_ref, cw_ref, o_ref, ubuf, *, tiles_per_seq):
    i = pl.program_id(1)
    tm = h_ref.shape[0]
    h = h_ref[...]
    x_in = _dot(h, wx_ref[...])
    gate_b = _dot(h, wb_ref[...])
    gate_c = _dot(h, wc_ref[...])

    @pl.when(i % tiles_per_seq == 0)
    def _():
        ubuf[0:8, :] = jnp.zeros((8, ubuf.shape[1]), F32)

    ubuf[8:tm + 8, :] = gate_c * x_in
    w = cw_ref[...]
    conv = (w[2:3, :] * ubuf[8:tm + 8, :] + w[1:2, :] * ubuf[7:tm + 7, :]
            + w[0:1, :] * ubuf[6:tm + 6, :])
    o_ref[...] = (gate_b * conv).astype(o_ref.dtype)
    ubuf[0:8, :] = ubuf[tm:tm + 8, :]


def _conv_mixer(h, wx, wb, wc, cw, seq, tm, tn):
    m, d = h.shape
    n = wx.shape[1]
    wspec = pl.BlockSpec((d, tn), lambda j, i: (0, j))
    return pl.pallas_call(
        functools.partial(_conv_kernel, tiles_per_seq=seq // tm),
        grid=(n // tn, m // tm),
        in_specs=[pl.BlockSpec((tm, d), lambda j, i: (i, 0)), wspec, wspec, wspec,
                  pl.BlockSpec((8, tn), lambda j, i: (0, j))],
        out_specs=pl.BlockSpec((tm, tn), lambda j, i: (i, j)),
        out_shape=jax.ShapeDtypeStruct((m, n), MXU_DTYPE),
        scratch_shapes=[pltpu.VMEM((tm + 8, tn), F32)],
        compiler_params=_params("parallel", "arbitrary"),
        name="conv_mixer",
    )(h, wx, wb, wc, cw)


def _merge_kernel(oa_ref, v_ref, h_ref, wap_ref, wco_ref, wga_ref, wgc_ref, o_ref):
    h = h_ref[...]
    y_attn = _dot(oa_ref[...], wap_ref[...])
    y_conv = _dot(v_ref[...], wco_ref[...])
    g_attn = jax.nn.sigmoid(_dot(h, wga_ref[...]))
    g_conv = jax.nn.sigmoid(_dot(h, wgc_ref[...]))
    o_ref[...] = (g_attn * y_attn + g_conv * y_conv).astype(o_ref.dtype)


def _merge(oa, v, h, wap, wco, wga, wgc, tm, tn):
    m, d = h.shape
    n = wap.shape[1]
    aspec = pl.BlockSpec((tm, d), lambda i, j: (i, 0))
    wspec = pl.BlockSpec((d, tn), lambda i, j: (0, j))
    return pl.pallas_call(
        _merge_kernel,
        grid=(m // tm, n // tn),
        in_specs=[aspec, aspec, aspec, wspec, wspec, wspec, wspec],
        out_specs=pl.BlockSpec((tm, tn), lambda i, j: (i, j)),
        out_shape=jax.ShapeDtypeStruct((m, n), MXU_DTYPE),
        compiler_params=_params("parallel", "arbitrary"),
        name="gated_merge",
    )(oa, v, h, wap, wco, wga, wgc)


def _outproj_kernel(a_ref, w_ref, x_ref, g_ref, xo_ref, ho_ref):
    x = x_ref[...] + _dot(a_ref[...], w_ref[...])
    xo_ref[...] = x
    ho_ref[...] = _rms(x, g_ref[...]).astype(ho_ref.dtype)


def _outproj(a, w, x, g, tm):
    m, d = x.shape
    row = pl.BlockSpec((tm, d), lambda i: (i, 0))
    return pl.pallas_call(
        _outproj_kernel,
        grid=(m // tm,),
        in_specs=[row, pl.BlockSpec((d, d), lambda i: (0, 0)), row,
                  pl.BlockSpec((1, d), lambda i: (0, 0))],
        out_specs=[row, row],
        out_shape=[jax.ShapeDtypeStruct((m, d), F32), jax.ShapeDtypeStruct((m, d), MXU_DTYPE)],
        compiler_params=_params("parallel"),
        name="out_proj",
    )(a, w, x, g.reshape(1, d))


def _mlp_kernel(h_ref, wu_ref, wd_ref, x_ref, g_ref, *refs, last):
    acc = refs[-1]
    f = pl.program_id(1)

    @pl.when(f == 0)
    def _():
        acc[...] = x_ref[...]

    a = jnp.maximum(_dot(h_ref[...], wu_ref[...]), 0.0)
    acc[...] += _dot((a * a).astype(MXU_DTYPE), wd_ref[...])

    @pl.when(f == pl.num_programs(1) - 1)
    def _():
        x = acc[...]
        normed = _rms(x, g_ref[...])
        if last:
            refs[0][...] = normed
        else:
            refs[0][...] = x
            refs[1][...] = normed.astype(refs[1].dtype)


def _mlp(h, wu, wd, x, g, tm, tf, last):
    m, d = x.shape
    ff = wu.shape[1]
    row = pl.BlockSpec((tm, d), lambda i, f: (i, 0))
    if last:
        out_specs, out_shape = [row], [jax.ShapeDtypeStruct((m, d), F32)]
    else:
        out_specs = [row, row]
        out_shape = [jax.ShapeDtypeStruct((m, d), F32), jax.ShapeDtypeStruct((m, d), MXU_DTYPE)]
    return pl.pallas_call(
        functools.partial(_mlp_kernel, last=last),
        grid=(m // tm, ff // tf),
        in_specs=[row, pl.BlockSpec((d, tf), lambda i, f: (0, f)),
                  pl.BlockSpec((tf, d), lambda i, f: (f, 0)), row,
                  pl.BlockSpec((1, d), lambda i, f: (0, 0))],
        out_specs=out_specs,
        out_shape=out_shape,
        scratch_shapes=[pltpu.VMEM((tm, d), F32)],
        compiler_params=_params("parallel", "arbitrary"),
        name="relu2_mlp",
    )(h, wu, wd, x, g.reshape(1, d))


def _rope_tables(pos):
    half = HEAD_DIM // 2
    inv_freq = jnp.exp(-math.log(ROPE_THETA) * jnp.arange(half, dtype=F32) / half)
    ang = pos.astype(F32)[:, None] * inv_freq[None, :]
    cos, sin = jnp.cos(ang), jnp.sin(ang)
    return jnp.concatenate([cos, cos], axis=-1), jnp.concatenate([-sin, sin], axis=-1)


def _cmp_to_sel_t(ncp, ns):
    nc = ncp - 1
    cs = np.arange(nc) * CMP_STRIDE
    ss = np.arange(ns) * SEL_LEN
    ov = np.minimum(cs[:, None] + CMP_LEN, ss[None, :] + SEL_LEN) - np.maximum(cs[:, None], ss[None, :])
    m = np.zeros((ncp, ns), np.float32)
    m[:nc] = np.clip(ov, 0, None) / CMP_LEN
    return jnp.asarray(m.T, dtype=MXU_DTYPE)


def _block_onehot(seq):
    e = (np.arange(seq)[:, None] // SEL_LEN == np.arange(LANES)[None, :]).astype(np.float32)
    return jnp.asarray(e, dtype=MXU_DTYPE)


def kernel(x, norm1_g, w_in, cmp_pos_k, cmp_w1_k, cmp_w2_k, cmp_pos_v, cmp_w1_v, cmp_w2_v,
           conv_w, w_attn_proj, w_conv_out, w_o, norm2_g, w_up, w_down, final_g):
    batch, seq, d = x.shape
    depth = w_in.shape[0]
    m = batch * seq
    attn_dim = N_HEADS * HEAD_DIM
    kv_dim = N_KV * HEAD_DIM
    n_gate = N_HEADS * N_NSA_BRANCH
    ncp = seq // CMP_STRIDE
    ns = seq // SEL_LEN
    cast = lambda a: a.astype(MXU_DTYPE)

    tm = min(1024, seq)
    tm_small = min(512, seq)
    tq = 256
    tk = 512

    cos_t, sin_t = _rope_tables(jnp.arange(seq))
    cos_c, sin_c = _rope_tables(jnp.arange(ncp) * CMP_STRIDE + CMP_LEN - 1)
    mselt = _cmp_to_sel_t(ncp, ns)
    onehot = jnp.broadcast_to(jnp.tile(_block_onehot(seq), (batch, 1))[None], (N_KV, m, LANES))
    zeros = jnp.zeros((N_KV, m, LANES), MXU_DTYPE)
    ksx0 = jnp.concatenate([zeros, onehot], axis=-1)
    vsx0 = jnp.concatenate([zeros, jnp.ones_like(zeros)], axis=-1)
    before_start = jnp.asarray(np.arange(2 * LANES) == LANES, dtype=MXU_DTYPE)
    front = lambda a, rows: jnp.concatenate(
        [jnp.broadcast_to(rows, (N_KV, batch, WINDOW, 2 * LANES)),
         a.reshape(N_KV, batch, seq, 2 * LANES)], axis=2)
    kwx0 = front(jnp.zeros_like(vsx0), before_start)
    vwx0 = front(vsx0, jnp.zeros((), MXU_DTYPE))

    q_base, kc_base = 0, 16

    x2 = x.reshape(m, d)
    h = _rmsnorm(x2, norm1_g[0], tm)
    out = None
    for l in range(depth):
        wl = w_in[l]
        o_kv = attn_dim
        kv = [wl[:, o_kv + j * kv_dim:o_kv + (j + 1) * kv_dim] for j in range(6)]
        w_qkv = cast(jnp.concatenate([wl[:, :attn_dim], kv[2], kv[4], kv[0], kv[1], kv[3], kv[5]], axis=1))
        o_ng = o_kv + 6 * kv_dim
        w_ng = wl[:, o_ng:o_ng + n_gate].reshape(d, N_KV, GROUP * N_NSA_BRANCH)
        w_ng = cast(jnp.pad(w_ng, ((0, 0), (0, 0), (0, LANES - GROUP * N_NSA_BRANCH))).reshape(d, N_KV * LANES))
        o_cv = o_ng + n_gate
        w_x, w_b, w_c = [cast(wl[:, o_cv + j * d:o_cv + (j + 1) * d]) for j in range(3)]
        o_mg = o_cv + 3 * d
        w_ga, w_gc = [cast(wl[:, o_mg + j * d:o_mg + (j + 1) * d]) for j in range(2)]

        qkv, ksx, vsx, kwx, vwx = _qkv_proj(h, w_qkv, cos_t, sin_t, ksx0, vsx0, kwx0, vwx0, seq, tm_small)
        gates = _gate_proj(h, w_ng, tm)

        chunks = qkv[kc_base:kc_base + 2 * N_KV].reshape(2 * N_KV, batch, ncp, CMP_STRIDE * HEAD_DIM)
        pad_pos = lambda p: cast(jnp.pad(p.reshape(1, CMP_LEN * HEAD_DIM), ((0, 7), (0, 0))))
        kc, vc = _compress(chunks, 0, N_KV, cast(cmp_w1_k[l]), cast(cmp_w2_k[l]), pad_pos(cmp_pos_k[l]),
                           cast(cmp_w1_v[l]), cast(cmp_w2_v[l]), pad_pos(cmp_pos_v[l]), cos_c, sin_c, batch)

        o_attn = _attention(qkv, ksx, vsx, kwx, vwx, kc, vc, gates, mselt, batch, seq, q_base, tq, tk)

        cw = jnp.pad(conv_w[l], ((0, 8 - CONV_WIDTH), (0, 0)))
        v_conv = _conv_mixer(h, w_x, w_b, w_c, cw, seq, tm_small, 512)

        merged = _merge(o_attn, v_conv, h, cast(w_attn_proj[l]), cast(w_conv_out[l]), w_ga, w_gc,
                        tm_small, 512)
        x2, h2 = _outproj(merged, cast(w_o[l]), x2, norm2_g[l], tm_small)

        last = l == depth - 1
        g_next = final_g if last else norm1_g[l + 1]
        res = _mlp(h2, cast(w_up[l]), cast(w_down[l]), x2, g_next, tm_small, 512, last)
        if last:
            out = res[0]
        else:
            x2, h = res
    return out.reshape(batch, seq, d)
```

```python
import collections
import functools
import math

import numpy as np
import jax
import jax.numpy as jnp
from jax import lax
from jax.experimental import pallas as pl
from jax.experimental.pallas import tpu as pltpu

N_HEADS = 16
HEAD_DIM = 128
N_KV = 4
GROUP = N_HEADS // N_KV
CMP_LEN = 32
CMP_STRIDE = 16
SEL_LEN = 64
N_SEL = 16
WINDOW = 512
N_NSA_BRANCH = 3
CONV_WIDTH = 3
ROPE_THETA = 10000.0
EPS = 1e-6

MXU_DTYPE = jnp.bfloat16
F32 = jnp.float32
NEG = -1e30
VMEM_LIMIT = 56 * 1024 * 1024
LANES = 128
NT_DIMS = (((1,), (1,)), ((), ()))


def _params(*sem):
    return pltpu.CompilerParams(dimension_semantics=sem, vmem_limit_bytes=VMEM_LIMIT)


def _dot(a, b):
    return jnp.dot(a, b, preferred_element_type=F32)


def _rms(x, g):
    return x * lax.rsqrt(jnp.mean(x * x, axis=-1, keepdims=True) + EPS) * g


def _norm_kernel(x_ref, g_ref, o_ref):
    o_ref[...] = _rms(x_ref[...], g_ref[...]).astype(o_ref.dtype)


def _rmsnorm(x2d, g, tm):
    m, d = x2d.shape
    return pl.pallas_call(
        _norm_kernel,
        grid=(m // tm,),
        in_specs=[pl.BlockSpec((tm, d), lambda i: (i, 0)), pl.BlockSpec((1, d), lambda i: (0, 0))],
        out_specs=pl.BlockSpec((tm, d), lambda i: (i, 0)),
        out_shape=jax.ShapeDtypeStruct((m, d), MXU_DTYPE),
        compiler_params=_params("parallel"),
        name="rmsnorm",
    )(x2d, g.reshape(1, d))


QKV_TILES = ("q", "q", "q", "q", "k_sel", "k_win", "k_cmp", "v_cmp", "v_sel", "v_win")


def _qkv_kernel(h_ref, w_ref, cos_ref, sin_ref, ksx_in, vsx_in, kwx_in, vwx_in,
                o_ref, ksx_ref, vsx_ref, kwx_ref, vwx_ref, *, q_scale):
    del ksx_in, vsx_in, kwx_in, vwx_in
    j = pl.program_id(1)
    acc = _dot(h_ref[...], w_ref[...])
    dest = {"q": o_ref, "k_cmp": o_ref, "v_cmp": o_ref, "k_sel": ksx_ref, "v_sel": vsx_ref,
            "k_win": kwx_ref, "v_win": vwx_ref}

    def put(ref, rope, scale):
        for c in range(ref.shape[0]):
            xc = acc[:, c * LANES:(c + 1) * LANES]
            if rope:
                xc = xc * (cos_ref[...] * scale) + pltpu.roll(xc, HEAD_DIM // 2, 1) * (sin_ref[...] * scale)
            ref[c] = xc.astype(ref.dtype)

    for name in dict.fromkeys(QKV_TILES):
        tiles = [t for t, n in enumerate(QKV_TILES) if n == name]

        @pl.when((j >= tiles[0]) & (j <= tiles[-1]))
        def _():
            put(dest[name], rope=name in ("q", "k_sel", "k_win"), scale=q_scale if name == "q" else 1.0)


def _qkv_proj(h, w, cos, sin, ksx, vsx, kwx, vwx, seq, tm):
    m, d = h.shape
    tn = N_KV * LANES
    assert w.shape[1] == tn * len(QKV_TILES) and seq % tm == 0 and kwx.shape[2] == seq + tm
    tps = seq // tm
    main_block = lambda j: jnp.minimum(j, 3) + (j >= 6).astype(jnp.int32) + (j >= 7).astype(jnp.int32)
    ext_spec = pl.BlockSpec((N_KV, tm, LANES), lambda i, j: (0, i, 0))
    win_spec = pl.BlockSpec((N_KV, None, tm, LANES), lambda i, j: (0, i // tps, 1 + i % tps, 0))
    any_spec = pl.BlockSpec(memory_space=pl.ANY)
    sds = lambda a: jax.ShapeDtypeStruct(a.shape, a.dtype)
    kern = functools.partial(_qkv_kernel, q_scale=HEAD_DIM ** -0.5 * math.log2(math.e))
    return pl.pallas_call(
        kern,
        grid=(m // tm, len(QKV_TILES)),
        in_specs=[pl.BlockSpec((tm, d), lambda i, j: (i, 0)),
                  pl.BlockSpec((d, tn), lambda i, j: (0, j)),
                  pl.BlockSpec((tm, LANES), lambda i, j: (i % tps, 0)),
                  pl.BlockSpec((tm, LANES), lambda i, j: (i % tps, 0)),
                  any_spec, any_spec, any_spec, any_spec],
        out_specs=[pl.BlockSpec((N_KV, tm, LANES), lambda i, j: (main_block(j), i, 0)),
                   ext_spec, ext_spec, win_spec, win_spec],
        out_shape=[jax.ShapeDtypeStruct((6 * N_KV, m, LANES), MXU_DTYPE),
                   sds(ksx), sds(vsx), sds(kwx), sds(vwx)],
        input_output_aliases={4: 1, 5: 2, 6: 3, 7: 4},
        compiler_params=_params("parallel", "arbitrary"),
        name="qkv_proj",
    )(h, w, cos, sin, ksx, vsx, kwx, vwx)


def _gate_kernel(h_ref, w_ref, o_ref):
    acc = _dot(h_ref[...], w_ref[...])
    for c in range(o_ref.shape[0]):
        o_ref[c] = jax.nn.sigmoid(acc[:, c * LANES:(c + 1) * LANES])


def _gate_proj(h, w, tm):
    m, d = h.shape
    n = w.shape[1]
    return pl.pallas_call(
        _gate_kernel,
        grid=(m // tm,),
        in_specs=[pl.BlockSpec((tm, d), lambda i: (i, 0)), pl.BlockSpec((d, n), lambda i: (0, 0))],
        out_specs=pl.BlockSpec((n // LANES, tm, LANES), lambda i: (0, i, 0)),
        out_shape=jax.ShapeDtypeStruct((n // LANES, m, LANES), F32),
        compiler_params=_params("parallel"),
        name="nsa_gate_proj",
    )(h, w)


def _compress_kernel(ck_ref, cv_ref, w1k_ref, w2k_ref, pk_ref, w1v_ref, w2v_ref, pv_ref,
                     cos_ref, sin_ref, kc_ref, vc_ref):
    nch = ck_ref.shape[0]
    half = w1k_ref.shape[0] // 2

    def phi(c_ref, w1_ref, w2_ref, p_ref):
        c = c_ref[...]
        first = _dot(c, w1_ref[:half, :])
        second = _dot(c, w1_ref[half:, :])
        pos = _dot(p_ref[...], w1_ref[...])[0:1, :]
        hid = first + pltpu.roll(second, nch - 1, 0) + pos
        act = hid * jax.nn.sigmoid(hid)
        return _dot(act.astype(MXU_DTYPE), w2_ref[...])

    kc = phi(ck_ref, w1k_ref, w2k_ref, pk_ref)
    kc = kc * cos_ref[...] + pltpu.roll(kc, HEAD_DIM // 2, 1) * sin_ref[...]
    kc_ref[...] = kc.astype(kc_ref.dtype)
    vc_ref[...] = phi(cv_ref, w1v_ref, w2v_ref, pv_ref).astype(vc_ref.dtype)


def _compress(chunks, kbase, vbase, w1k, w2k, pk, w1v, w2v, pv, cos_c, sin_c, batch):
    _, _, nch, cw = chunks.shape
    const = lambda a: pl.BlockSpec(a.shape, lambda b, g: (0,) * a.ndim)
    out_spec = pl.BlockSpec((None, None, nch, HEAD_DIM), lambda b, g: (b, g, 0, 0))
    out_sds = jax.ShapeDtypeStruct((batch, N_KV, nch, HEAD_DIM), MXU_DTYPE)
    return pl.pallas_call(
        _compress_kernel,
        grid=(batch, N_KV),
        in_specs=[pl.BlockSpec((None, None, nch, cw), lambda b, g: (kbase + g, b, 0, 0)),
                  pl.BlockSpec((None, None, nch, cw), lambda b, g: (vbase + g, b, 0, 0)),
                  const(w1k), const(w2k), const(pk), const(w1v), const(w2v), const(pv),
                  const(cos_c), const(sin_c)],
        out_specs=[out_spec, out_spec],
        out_shape=[out_sds, out_sds],
        compiler_params=_params("parallel", "parallel"),
        name="compress",
    )(chunks, chunks, w1k, w2k, pk, w1v, w2v, pv, cos_c, sin_c)


_Branch = collections.namedtuple("_Branch", "s p m acc bias")


def _attn_kernel(q_ref, kc_ref, vc_ref, ksx_ref, vsx_ref, kw_ref, vwx_ref, gate_ref, mselt_ref,
                 o_ref, qx_scr, qw_scr, sc_scr, pc_scr, sd_scr, pd_scr, sw_scr, pw_scr, sa_scr, sb_scr,
                 pa_scr, pb_scr, bc_scr, bd_scr, bw_scr, mc_scr, mw_scr, m_scr, a_scr,
                 accc_scr, accw_scr, acc_scr, psum_scr, imp_scr, rank_scr, *, tq, tk, rb):
    i = pl.program_id(2)
    q0 = pl.multiple_of(i * tq, tq)
    r = GROUP
    rows = r * tq
    ncp = kc_ref.shape[0]
    ns = mselt_ref.shape[0]
    wlen = WINDOW + tq
    n_chunks = rows // rb
    t_col = q0 + lax.broadcasted_iota(jnp.int32, (tq, 1), 0)
    q = q_ref[...].reshape(rows, HEAD_DIM)
    gates = gate_ref[...]
    head = lambda h: slice(h * tq, (h + 1) * tq)
    gate = lambda h, br: gates[:, h * N_NSA_BRANCH + br:h * N_NSA_BRANCH + br + 1]
    half = rows // 2
    halves = [slice(0, half), slice(half, rows)]
    lane = lambda kk: slice(kk * LANES, (kk + 1) * LANES)

    def load_scores(br, rs, bs, width):
        xs = [br.s[rs, lane(kk)] for kk in range(width // LANES)]
        return [x if b is None else x + b[bs, :] for x, b in zip(xs, br.bias)]

    def softmax_rows(br, width, chunks, first, cmp):
        def slices(c):
            r0 = c * rb
            return slice(r0, r0 + rb), slice(r0 % tq, r0 % tq + rb)

        for c in chunks:
            rs, bs = slices(c)
            xs = load_scores(br, rs, bs, width)
            mx = jnp.max(functools.reduce(jnp.maximum, xs), axis=-1, keepdims=True)
            if first:
                br.m[rs, :] = jnp.broadcast_to(mx, (rb, LANES))
            else:
                m_old = br.m[rs, :]
                m_new = jnp.maximum(m_old, mx)
                a_scr[rs, :] = jnp.exp2(m_old - m_new)
                br.m[rs, :] = m_new

        for c in chunks:
            rs, bs = slices(c)
            xs = load_scores(br, rs, bs, width)
            m = br.m[rs, :]
            ps = [jnp.exp2(x - m) for x in xs]
            if cmp:
                ps = [jnp.where(x > 0.5 * NEG, p, 0.0) for x, p in zip(xs, ps)]
                lsum = jnp.sum(functools.reduce(jnp.add, ps), axis=-1, keepdims=True)
                inv = 1.0 / jnp.where(lsum > 0.0, lsum, 1.0)
                ps = [p * inv for p in ps]
                for kk, p in enumerate(ps):
                    psum_scr[bs, lane(kk)] += p
            elif not first:
                a = a_scr[rs, :]
                br.acc[rs, :LANES] = a * br.acc[rs, :LANES]
                br.acc[rs, LANES:] = a * br.acc[rs, LANES:]
            for kk, p in enumerate(ps):
                br.p[rs, lane(kk)] = p.astype(br.p.dtype)

    def scores(br, width, q_rows, k):
        for hs in halves:
            br.s[hs, :width] = lax.dot_general(q_rows(hs), k, NT_DIMS, preferred_element_type=F32)

    def weighted_values(br, width, v, *, first, cmp=False):
        for hi, hs in enumerate(halves):
            chunks = range(hi * n_chunks // 2, (hi + 1) * n_chunks // 2)
            softmax_rows(br, width, chunks, first, cmp)
            pv = _dot(br.p[hs, :width], v)
            cols = slice(0, v.shape[1])
            if first:
                br.acc[hs, cols] = pv
            else:
                br.acc[hs, cols] += pv

    q_plain = lambda hs: q[hs]
    q_ext = lambda hs: qx_scr[hs, :]
    row_id = lax.broadcasted_iota(jnp.int32, (tq, LANES), 0)
    col_id = lax.broadcasted_iota(jnp.int32, (tq, LANES), 1)

    n_cl = ncp // LANES
    for kk in range(n_cl):
        cmp_end = (col_id + kk * LANES) * CMP_STRIDE + (CMP_LEN - 1)
        bc_scr[:, lane(kk)] = jnp.where(cmp_end <= t_col, 0.0, NEG)
    n_dl = tq // LANES
    for kk in range(n_dl):
        own = (col_id <= row_id - kk * LANES) & (row_id < (kk + 1) * LANES)
        bd_scr[:, lane(kk)] = jnp.where(own, 0.0, NEG)
    n_wl = wlen // LANES
    for kk in range(n_dl):
        bw_scr[:, lane(kk)] = jnp.where(col_id + kk * LANES > row_id, 0.0, NEG)
        bw_scr[:, lane(n_dl + kk)] = jnp.where(col_id + kk * LANES <= row_id, 0.0, NEG)

    cmp_br = _Branch(sc_scr, pc_scr, mc_scr, accc_scr, [bc_scr.at[:, lane(kk)] for kk in range(n_cl)])
    diag_br = _Branch(sd_scr, pd_scr, m_scr, acc_scr, [bd_scr.at[:, lane(kk)] for kk in range(n_dl)])
    win_bias = ([bw_scr.at[:, lane(kk)] for kk in range(n_dl)] + [None] * (n_wl - 2 * n_dl)
                + [bw_scr.at[:, lane(n_dl + kk)] for kk in range(n_dl)])
    win_br = _Branch(sw_scr, pw_scr, mw_scr, accw_scr, win_bias)
    sel_a = _Branch(sa_scr, pa_scr, m_scr, acc_scr, [None] * (tk // LANES))
    sel_b = _Branch(sb_scr, pb_scr, m_scr, acc_scr, [None] * (tk // LANES))

    psum_scr[...] = jnp.zeros(psum_scr.shape, F32)
    pad_col = jnp.where(lax.broadcasted_iota(jnp.int32, (rows, LANES), 1) == 0, NEG, 0.0)
    qw_scr[:, :LANES] = q
    qw_scr[:, LANES:] = pad_col.astype(qw_scr.dtype)
    scores(cmp_br, ncp, q_plain, kc_ref[...])
    w_rows = pl.ds(q0 + (kw_ref.shape[0] - ksx_ref.shape[0] - WINDOW), wlen)
    scores(win_br, wlen, lambda hs: qw_scr[hs, :], kw_ref[w_rows, :])

    weighted_values(cmp_br, ncp, vc_ref[...], first=True, cmp=True)
    scores(diag_br, tq, q_plain, ksx_ref[pl.ds(q0, tq), :LANES])
    mselt = mselt_ref[...]
    imp = jnp.zeros((ns, tq), F32)
    rem = psum_scr[...]
    for _ in range(3):
        piece = rem.astype(MXU_DTYPE)
        imp = imp + lax.dot_general(mselt, piece, NT_DIMS, preferred_element_type=F32)
        rem = rem - piece.astype(F32)
    blk = lax.broadcasted_iota(jnp.int32, (ns, tq), 0)
    tb = (q0 + lax.broadcasted_iota(jnp.int32, (ns, tq), 1)) // SEL_LEN
    forced = (blk == 0) | (blk == tb) | (blk == tb - 1)
    imp_scr[...] = jnp.where(blk <= tb, jnp.where(forced, jnp.inf, imp), -jnp.inf)
    rank_scr[...] = jnp.zeros(rank_scr.shape, F32)

    weighted_values(win_br, wlen, vwx_ref[w_rows, :], first=True)
    weighted_values(diag_br, tq, vsx_ref[pl.ds(q0, tq), :], first=True)

    sub = 8
    n_grp = ns // sub
    sub_id = lax.broadcasted_iota(jnp.int32, (sub, tq), 0)
    grp = lambda v: slice(sub * v, sub * (v + 1))
    for gm in range(n_grp):
        @pl.when(sub * gm * SEL_LEN <= q0 + tq - 1)
        def _():
            xs = [imp_scr[grp(v), :] for v in range(n_grp)]
            ranks = [rank_scr[grp(v), :] for v in range(n_grp)]
            for mp in range(sub * gm, sub * (gm + 1)):
                row = jnp.broadcast_to(xs[gm][mp % sub:mp % sub + 1, :], (sub, tq))
                for v, x in enumerate(xs):
                    if sub * v > mp:
                        ahead = row >= x
                    elif sub * v + sub - 1 <= mp:
                        ahead = row > x
                    else:
                        ahead = (row > x) | ((row == x) & (sub_id > mp % sub))
                    ranks[v] = ranks[v] + jnp.where(ahead, 1.0, 0.0)
            for v in range(n_grp):
                rank_scr[grp(v), :] = ranks[v]

    own_start = (q0 + lax.broadcasted_iota(jnp.int32, (ns, tq), 1)) // LANES * (LANES // SEL_LEN)
    off = jnp.where((rank_scr[...] < float(min(N_SEL, ns))) & (blk < own_start), 0.0, NEG)
    off = off.T.astype(MXU_DTYPE)
    if ns < LANES:
        off = jnp.concatenate([off, jnp.zeros((tq, LANES - ns), MXU_DTYPE)], axis=1)
    for h in range(r):
        qx_scr[head(h), :LANES] = q_ref[h]
        qx_scr[head(h), LANES:] = off

    last_tile = ksx_ref.shape[0] // tk - 1

    def tile_rows(t):
        return pl.ds(pl.multiple_of(jnp.minimum(t, last_tile) * tk, tk), tk)

    def sel_pair(jj, carry):
        t = 2 * jj
        scores(sel_b, tk, q_ext, ksx_ref[tile_rows(t + 1), :])
        weighted_values(sel_a, tk, vsx_ref[tile_rows(t), :], first=False)
        scores(sel_a, tk, q_ext, ksx_ref[tile_rows(t + 2), :])
        weighted_values(sel_b, tk, vsx_ref[tile_rows(t + 1), :], first=False)
        return carry

    n_tiles = (q0 + tq - LANES + tk - 1) // tk
    scores(sel_a, tk, q_ext, ksx_ref[tile_rows(0), :])
    lax.fori_loop(0, (n_tiles + 1) // 2, sel_pair, 0)

    o_sel = acc_scr[:, :LANES] * (1.0 / acc_scr[:, LANES:])
    o_win = accw_scr[:, :LANES] * (1.0 / accw_scr[:, LANES:])
    for h in range(r):
        o = (gate(h, 0) * accc_scr[head(h), :] + gate(h, 1) * o_sel[head(h), :]
             + gate(h, 2) * o_win[head(h), :])
        o_ref[:, h * HEAD_DIM:(h + 1) * HEAD_DIM] = o.astype(o_ref.dtype)


def _attention(qkv, ksx, vsx, kwp, vwx, kc, vc, gates, mselt, batch, seq, q_base, tq, tk):
    nq = seq // tq
    ncp = kc.shape[2]
    ns = mselt.shape[0]
    assert ns <= LANES and seq % (2 * tk) == 0 and tk % tq == 0 and WINDOW % tq == 0
    assert tq % LANES == 0 and ncp % LANES == 0 and kwp.shape[2] >= seq + WINDOW
    rows = GROUP * tq
    wlen = WINDOW + tq
    ext_spec = pl.BlockSpec((None, seq, 2 * LANES), lambda b, g, i: (g, b, 0))
    cmp_spec = pl.BlockSpec((None, None, ncp, HEAD_DIM), lambda b, g, i: (b, g, 0, 0))
    f32 = lambda *shape: pltpu.VMEM(shape, F32)
    mxu = lambda *shape: pltpu.VMEM(shape, MXU_DTYPE)
    return pl.pallas_call(
        functools.partial(_attn_kernel, tq=tq, tk=tk, rb=16),
        grid=(batch, N_KV, nq),
        in_specs=[pl.BlockSpec((GROUP, tq, HEAD_DIM), lambda b, g, i: (q_base // GROUP + g, b * nq + i, 0)),
                  cmp_spec, cmp_spec, ext_spec, ext_spec,
                  pl.BlockSpec((None, None) + kwp.shape[2:], lambda b, g, i: (g, b, 0, 0)),
                  pl.BlockSpec((None, None) + vwx.shape[2:], lambda b, g, i: (g, b, 0, 0)),
                  pl.BlockSpec((None, tq, LANES), lambda b, g, i: (g, b * nq + i, 0)),
                  pl.BlockSpec((ns, ncp), lambda b, g, i: (0, 0))],
        out_specs=pl.BlockSpec((tq, GROUP * HEAD_DIM), lambda b, g, i: (b * nq + i, g)),
        out_shape=jax.ShapeDtypeStruct((batch * seq, N_HEADS * HEAD_DIM), MXU_DTYPE),
        scratch_shapes=[mxu(rows, 2 * LANES), mxu(rows, 2 * LANES),
                        f32(rows, ncp), mxu(rows, ncp),
                        f32(rows, tq), mxu(rows, tq),
                        f32(rows, wlen), mxu(rows, wlen),
                        f32(rows, tk), f32(rows, tk),
                        mxu(rows, tk), mxu(rows, tk),
                        f32(tq, ncp), f32(tq, tq), f32(tq, 2 * tq),
                        f32(rows, LANES), f32(rows, LANES), f32(rows, LANES),
                        f32(rows, LANES),
                        f32(rows, LANES),
                        f32(rows, 2 * LANES), f32(rows, 2 * LANES),
                        f32(tq, ncp),
                        f32(ns, tq), f32(ns, tq)],
        compiler_params=_params("parallel", "parallel", "arbitrary"),
        name="nsa_attention",
    )(qkv, kc, vc, ksx, vsx, kwp, vwx, gates, mselt)


def _conv_kernel(h_ref, wx_ref, wb_ref, wc_ref, cw_ref, o_ref, ubuf, *, tiles_per_seq):
    i = pl.program_id(1)
    tm = h_ref.shape[0]
    h = h_ref[...]
    x_in = _dot(h, wx_ref[...])
    gate_b = _dot(h, wb_ref[...])
    gate_c = _dot(h, wc_ref[...])

    @pl.when(i % tiles_per_seq == 0)
    def _():
        ubuf[0:8, :] = jnp.zeros((8, ubuf.shape[1]), F32)

    ubuf[8:tm + 8, :] = gate_c * x_in
    w = cw_ref[...]
    conv = (w[2:3, :] * ubuf[8:tm + 8, :] + w[1:2, :] * ubuf[7:tm + 7, :]
            + w[0:1, :] * ubuf[6:tm + 6, :])
    o_ref[...] = (gate_b * conv).astype(o_ref.dtype)
    ubuf[0:8, :] = ubuf[tm:tm + 8, :]


def _conv_mixer(h, wx, wb, wc, cw, seq, tm, tn):
    m, d = h.shape
    n = wx.shape[1]
    wspec = pl.BlockSpec((d, tn), lambda j, i: (0, j))
    return pl.pallas_call(
        functools.partial(_conv_kernel, tiles_per_seq=seq // tm),
        grid=(n // tn, m // tm),
        in_specs=[pl.BlockSpec((tm, d), lambda j, i: (i, 0)), wspec, wspec, wspec,
                  pl.BlockSpec((8, tn), lambda j, i: (0, j))],
        out_specs=pl.BlockSpec((tm, tn), lambda j, i: (i, j)),
        out_shape=jax.ShapeDtypeStruct((m, n), MXU_DTYPE),
        scratch_shapes=[pltpu.VMEM((tm + 8, tn), F32)],
        compiler_params=_params("parallel", "arbitrary"),
        name="conv_mixer",
    )(h, wx, wb, wc, cw)


def _merge_kernel(oa_ref, v_ref, h_ref, wap_ref, wco_ref, wga_ref, wgc_ref, o_ref):
    h = h_ref[...]
    y_attn = _dot(oa_ref[...], wap_ref[...])
    y_conv = _dot(v_ref[...], wco_ref[...])
    g_attn = jax.nn.sigmoid(_dot(h, wga_ref[...]))
    g_conv = jax.nn.sigmoid(_dot(h, wgc_ref[...]))
    o_ref[...] = (g_attn * y_attn + g_conv * y_conv).astype(o_ref.dtype)


def _merge(oa, v, h, wap, wco, wga, wgc, tm, tn):
    m, d = h.shape
    n = wap.shape[1]
    aspec = pl.BlockSpec((tm, d), lambda i, j: (i, 0))
    wspec = pl.BlockSpec((d, tn), lambda i, j: (0, j))
    return pl.pallas_call(
        _merge_kernel,
        grid=(m // tm, n // tn),
        in_specs=[aspec, aspec, aspec, wspec, wspec, wspec, wspec],
        out_specs=pl.BlockSpec((tm, tn), lambda i, j: (i, j)),
        out_shape=jax.ShapeDtypeStruct((m, n), MXU_DTYPE),
        compiler_params=_params("parallel", "arbitrary"),
        name="gated_merge",
    )(oa, v, h, wap, wco, wga, wgc)


def _outproj_kernel(a_ref, w_ref, x_ref, g_ref, xo_ref, ho_ref):
    x = x_ref[...] + _dot(a_ref[...], w_ref[...])
    xo_ref[...] = x
    ho_ref[...] = _rms(x, g_ref[...]).astype(ho_ref.dtype)


def _outproj(a, w, x, g, tm):
    m, d = x.shape
    row = pl.BlockSpec((tm, d), lambda i: (i, 0))
    return pl.pallas_call(
        _outproj_kernel,
        grid=(m // tm,),
        in_specs=[row, pl.BlockSpec((d, d), lambda i: (0, 0)), row,
                  pl.BlockSpec((1, d), lambda i: (0, 0))],
        out_specs=[row, row],
        out_shape=[jax.ShapeDtypeStruct((m, d), F32), jax.ShapeDtypeStruct((m, d), MXU_DTYPE)],
        compiler_params=_params("parallel"),
        name="out_proj",
    )(a, w, x, g.reshape(1, d))


def _mlp_kernel(h_ref, wu_ref, wd_ref, x_ref, g_ref, *refs, last):
    acc = refs[-1]
    f = pl.program_id(1)

    @pl.when(f == 0)
    def _():
        acc[...] = x_ref[...]

    a = jnp.maximum(_dot(h_ref[...], wu_ref[...]), 0.0)
    acc[...] += _dot((a * a).astype(MXU_DTYPE), wd_ref[...])

    @pl.when(f == pl.num_programs(1) - 1)
    def _():
        x = acc[...]
        normed = _rms(x, g_ref[...])
        if last:
            refs[0][...] = normed
        else:
            refs[0][...] = x
            refs[1][...] = normed.astype(refs[1].dtype)


def _mlp(h, wu, wd, x, g, tm, tf, last):
    m, d = x.shape
    ff = wu.shape[1]
    row = pl.BlockSpec((tm, d), lambda i, f: (i, 0))
    if last:
        out_specs, out_shape = [row], [jax.ShapeDtypeStruct((m, d), F32)]
    else:
        out_specs = [row, row]
        out_shape = [jax.ShapeDtypeStruct((m, d), F32), jax.ShapeDtypeStruct((m, d), MXU_DTYPE)]
    return pl.pallas_call(
        functools.partial(_mlp_kernel, last=last),
        grid=(m // tm, ff // tf),
        in_specs=[row, pl.BlockSpec((d, tf), lambda i, f: (0, f)),
                  pl.BlockSpec((tf, d), lambda i, f: (f, 0)), row,
                  pl.BlockSpec((1, d), lambda i, f: (0, 0))],
        out_specs=out_specs,
        out_shape=out_shape,
        scratch_shapes=[pltpu.VMEM((tm, d), F32)],
        compiler_params=_params("parallel", "arbitrary"),
        name="relu2_mlp",
    )(h, wu, wd, x, g.reshape(1, d))


def _rope_tables(pos):
    half = HEAD_DIM // 2
    inv_freq = jnp.exp(-math.log(ROPE_THETA) * jnp.arange(half, dtype=F32) / half)
    ang = pos.astype(F32)[:, None] * inv_freq[None, :]
    cos, sin = jnp.cos(ang), jnp.sin(ang)
    return jnp.concatenate([cos, cos], axis=-1), jnp.concatenate([-sin, sin], axis=-1)


def _cmp_to_sel_t(ncp, ns):
    nc = ncp - 1
    cs = np.arange(nc) * CMP_STRIDE
    ss = np.arange(ns) * SEL_LEN
    ov = np.minimum(cs[:, None] + CMP_LEN, ss[None, :] + SEL_LEN) - np.maximum(cs[:, None], ss[None, :])
    m = np.zeros((ncp, ns), np.float32)
    m[:nc] = np.clip(ov, 0, None) / CMP_LEN
    return jnp.asarray(m.T, dtype=MXU_DTYPE)


def _block_onehot(seq):
    e = (np.arange(seq)[:, None] // SEL_LEN == np.arange(LANES)[None, :]).astype(np.float32)
    return jnp.asarray(e, dtype=MXU_DTYPE)


def kernel(x, norm1_g, w_in, cmp_pos_k, cmp_w1_k, cmp_w2_k, cmp_pos_v, cmp_w1_v, cmp_w2_v,
           conv_w, w_attn_proj, w_conv_out, w_o, norm2_g, w_up, w_down, final_g):
    batch, seq, d = x.shape
    depth = w_in.shape[0]
    m = batch * seq
    attn_dim = N_HEADS * HEAD_DIM
    kv_dim = N_KV * HEAD_DIM
    n_gate = N_HEADS * N_NSA_BRANCH
    ncp = seq // CMP_STRIDE
    ns = seq // SEL_LEN
    cast = lambda a: a.astype(MXU_DTYPE)

    tm = min(1024, seq)
    tm_small = min(512, seq)
    tq = 256
    tk = 512

    cos_t, sin_t = _rope_tables(jnp.arange(seq))
    cos_c, sin_c = _rope_tables(jnp.arange(ncp) * CMP_STRIDE + CMP_LEN - 1)
    mselt = _cmp_to_sel_t(ncp, ns)
    onehot = jnp.broadcast_to(jnp.tile(_block_onehot(seq), (batch, 1))[None], (N_KV, m, LANES))
    zeros = jnp.zeros((N_KV, m, LANES), MXU_DTYPE)
    ksx0 = jnp.concatenate([zeros, onehot], axis=-1)
    vsx0 = jnp.concatenate([zeros, jnp.ones_like(zeros)], axis=-1)
    before_start = jnp.asarray(np.arange(2 * LANES) == LANES, dtype=MXU_DTYPE)
    front = lambda a, rows: jnp.concatenate(
        [jnp.broadcast_to(rows, (N_KV, batch, tm, 2 * LANES)),
         a.reshape(N_KV, batch, seq, 2 * LANES)], axis=2)
    kwx0 = front(jnp.zeros_like(vsx0), before_start)
    vwx0 = front(vsx0, jnp.zeros((), MXU_DTYPE))

    q_base, kc_base = 0, 16

    x2 = x.reshape(m, d)
    h = _rmsnorm(x2, norm1_g[0], tm)
    out = None
    for l in range(depth):
        wl = w_in[l]
        o_kv = attn_dim
        kv = [wl[:, o_kv + j * kv_dim:o_kv + (j + 1) * kv_dim] for j in range(6)]
        w_qkv = cast(jnp.concatenate([wl[:, :attn_dim], kv[2], kv[4], kv[0], kv[1], kv[3], kv[5]], axis=1))
        o_ng = o_kv + 6 * kv_dim
        w_ng = wl[:, o_ng:o_ng + n_gate].reshape(d, N_KV, GROUP * N_NSA_BRANCH)
        w_ng = cast(jnp.pad(w_ng, ((0, 0), (0, 0), (0, LANES - GROUP * N_NSA_BRANCH))).reshape(d, N_KV * LANES))
        o_cv = o_ng + n_gate
        w_x, w_b, w_c = [cast(wl[:, o_cv + j * d:o_cv + (j + 1) * d]) for j in range(3)]
        o_mg = o_cv + 3 * d
        w_ga, w_gc = [cast(wl[:, o_mg + j * d:o_mg + (j + 1) * d]) for j in range(2)]

        qkv, ksx, vsx, kwx, vwx = _qkv_proj(h, w_qkv, cos_t, sin_t, ksx0, vsx0, kwx0, vwx0, seq, tm)
        gates = _gate_proj(h, w_ng, tm)

        chunks = qkv[kc_base:kc_base + 2 * N_KV].reshape(2 * N_KV, batch, ncp, CMP_STRIDE * HEAD_DIM)
        pad_pos = lambda p: cast(jnp.pad(p.reshape(1, CMP_LEN * HEAD_DIM), ((0, 7), (0, 0))))
        kc, vc = _compress(chunks, 0, N_KV, cast(cmp_w1_k[l]), cast(cmp_w2_k[l]), pad_pos(cmp_pos_k[l]),
                           cast(cmp_w1_v[l]), cast(cmp_w2_v[l]), pad_pos(cmp_pos_v[l]), cos_c, sin_c, batch)

        o_attn = _attention(qkv, ksx, vsx, kwx, vwx, kc, vc, gates, mselt, batch, seq, q_base, tq, tk)

        cw = jnp.pad(conv_w[l], ((0, 8 - CONV_WIDTH), (0, 0)))
        v_conv = _conv_mixer(h, w_x, w_b, w_c, cw, seq, tm_small, 512)

        merged = _merge(o_attn, v_conv, h, cast(w_attn_proj[l]), cast(w_conv_out[l]), w_ga, w_gc,
                        tm_small, 512)
        x2, h2 = _outproj(merged, cast(w_o[l]), x2, norm2_g[l], tm_small)

        last = l == depth - 1
        g_next = final_g if last else norm1_g[l + 1]
        res = _mlp(h2, cast(w_up[l]), cast(w_down[l]), x2, g_next, tm_small, 1024, last)
        if last:
            out = res[0]
        else:
            x2, h = res
    return out.reshape(batch, seq, d)
```

```python
import collections
import functools
import math

import numpy as np
import jax
import jax.numpy as jnp
from jax import lax
from jax.experimental import pallas as pl
from jax.experimental.pallas import tpu as pltpu

N_HEADS = 16
HEAD_DIM = 128
N_KV = 4
GROUP = N_HEADS // N_KV
CMP_LEN = 32
CMP_STRIDE = 16
SEL_LEN = 64
N_SEL = 16
WINDOW = 512
N_NSA_BRANCH = 3
CONV_WIDTH = 3
ROPE_THETA = 10000.0
EPS = 1e-6

MXU_DTYPE = jnp.bfloat16
F32 = jnp.float32
NEG = -1e30
VMEM_LIMIT = 56 * 1024 * 1024
LANES = 128
MXU_WIDTH = 256
NT_DIMS = (((1,), (1,)), ((), ()))


def _params(*sem):
    return pltpu.CompilerParams(dimension_semantics=sem, vmem_limit_bytes=VMEM_LIMIT)


def _dot(a, b):
    return jnp.dot(a, b, preferred_element_type=F32)


def _rms(x, g):
    return x * lax.rsqrt(jnp.mean(x * x, axis=-1, keepdims=True) + EPS) * g


def _column_parts(n):
    return [slice(c, c + MXU_WIDTH) for c in range(0, n, MXU_WIDTH)]


def _norm_kernel(x_ref, g_ref, o_ref):
    o_ref[...] = _rms(x_ref[...], g_ref[...]).astype(o_ref.dtype)


def _rmsnorm(x2d, g, tm):
    m, d = x2d.shape
    return pl.pallas_call(
        _norm_kernel,
        grid=(m // tm,),
        in_specs=[pl.BlockSpec((tm, d), lambda i: (i, 0)), pl.BlockSpec((1, d), lambda i: (0, 0))],
        out_specs=pl.BlockSpec((tm, d), lambda i: (i, 0)),
        out_shape=jax.ShapeDtypeStruct((m, d), MXU_DTYPE),
        compiler_params=_params("parallel"),
        name="rmsnorm",
    )(x2d, g.reshape(1, d))


QKV_TILES = ("q", "q", "q", "q", "k_sel", "k_win", "k_cmp", "v_cmp", "v_sel", "v_win")


def _qkv_kernel(h_ref, w_ref, cos_ref, sin_ref, ksx_in, vsx_in, kwx_in, vwx_in,
                o_ref, ksx_ref, vsx_ref, kwx_ref, vwx_ref, *, q_scale):
    del ksx_in, vsx_in, kwx_in, vwx_in
    j = pl.program_id(1)
    dest = {"q": o_ref, "k_cmp": o_ref, "v_cmp": o_ref, "k_sel": ksx_ref, "v_sel": vsx_ref,
            "k_win": kwx_ref, "v_win": vwx_ref}

    def put(ref, rope, scale):
        h = h_ref[...]
        for cs in _column_parts(w_ref.shape[1]):
            acc = _dot(h, w_ref[:, cs])
            for c in range(MXU_WIDTH // LANES):
                xc = acc[:, c * LANES:(c + 1) * LANES]
                if rope:
                    xc = (xc * (cos_ref[...] * scale)
                          + pltpu.roll(xc, HEAD_DIM // 2, 1) * (sin_ref[...] * scale))
                ref[cs.start // LANES + c] = xc.astype(ref.dtype)

    for name in dict.fromkeys(QKV_TILES):
        tiles = [t for t, n in enumerate(QKV_TILES) if n == name]

        @pl.when((j >= tiles[0]) & (j <= tiles[-1]))
        def _():
            put(dest[name], rope=name in ("q", "k_sel", "k_win"), scale=q_scale if name == "q" else 1.0)


def _qkv_proj(h, w, cos, sin, ksx, vsx, kwx, vwx, seq, tm):
    m, d = h.shape
    tn = N_KV * LANES
    assert w.shape[1] == tn * len(QKV_TILES) and seq % tm == 0 and kwx.shape[2] == seq + tm
    tps = seq // tm
    main_block = lambda j: jnp.minimum(j, 3) + (j >= 6).astype(jnp.int32) + (j >= 7).astype(jnp.int32)
    ext_spec = pl.BlockSpec((N_KV, tm, LANES), lambda i, j: (0, i, 0))
    win_spec = pl.BlockSpec((N_KV, None, tm, LANES), lambda i, j: (0, i // tps, 1 + i % tps, 0))
    any_spec = pl.BlockSpec(memory_space=pl.ANY)
    sds = lambda a: jax.ShapeDtypeStruct(a.shape, a.dtype)
    kern = functools.partial(_qkv_kernel, q_scale=HEAD_DIM ** -0.5 * math.log2(math.e))
    return pl.pallas_call(
        kern,
        grid=(m // tm, len(QKV_TILES)),
        in_specs=[pl.BlockSpec((tm, d), lambda i, j: (i, 0)),
                  pl.BlockSpec((d, tn), lambda i, j: (0, j)),
                  pl.BlockSpec((tm, LANES), lambda i, j: (i % tps, 0)),
                  pl.BlockSpec((tm, LANES), lambda i, j: (i % tps, 0)),
                  any_spec, any_spec, any_spec, any_spec],
        out_specs=[pl.BlockSpec((N_KV, tm, LANES), lambda i, j: (main_block(j), i, 0)),
                   ext_spec, ext_spec, win_spec, win_spec],
        out_shape=[jax.ShapeDtypeStruct((6 * N_KV, m, LANES), MXU_DTYPE),
                   sds(ksx), sds(vsx), sds(kwx), sds(vwx)],
        input_output_aliases={4: 1, 5: 2, 6: 3, 7: 4},
        compiler_params=_params("parallel", "arbitrary"),
        name="qkv_proj",
    )(h, w, cos, sin, ksx, vsx, kwx, vwx)


def _gate_kernel(h_ref, w_ref, o_ref):
    acc = _dot(h_ref[...], w_ref[...])
    for c in range(o_ref.shape[0]):
        o_ref[c] = jax.nn.sigmoid(acc[:, c * LANES:(c + 1) * LANES])


def _gate_proj(h, w, tm):
    m, d = h.shape
    n = w.shape[1]
    return pl.pallas_call(
        _gate_kernel,
        grid=(m // tm,),
        in_specs=[pl.BlockSpec((tm, d), lambda i: (i, 0)), pl.BlockSpec((d, n), lambda i: (0, 0))],
        out_specs=pl.BlockSpec((n // LANES, tm, LANES), lambda i: (0, i, 0)),
        out_shape=jax.ShapeDtypeStruct((n // LANES, m, LANES), F32),
        compiler_params=_params("parallel"),
        name="nsa_gate_proj",
    )(h, w)


def _compress_kernel(ck_ref, cv_ref, w1k_ref, w2k_ref, pk_ref, w1v_ref, w2v_ref, pv_ref,
                     cos_ref, sin_ref, kc_ref, vc_ref):
    nch = ck_ref.shape[0]
    half = w1k_ref.shape[0] // 2

    def phi(c_ref, w1_ref, w2_ref, p_ref):
        c = c_ref[...]
        first = _dot(c, w1_ref[:half, :])
        second = _dot(c, w1_ref[half:, :])
        pos = _dot(p_ref[...], w1_ref[...])[0:1, :]
        hid = first + pltpu.roll(second, nch - 1, 0) + pos
        act = hid * jax.nn.sigmoid(hid)
        return _dot(act.astype(MXU_DTYPE), w2_ref[...])

    kc = phi(ck_ref, w1k_ref, w2k_ref, pk_ref)
    kc = kc * cos_ref[...] + pltpu.roll(kc, HEAD_DIM // 2, 1) * sin_ref[...]
    kc_ref[...] = kc.astype(kc_ref.dtype)
    vc_ref[...] = phi(cv_ref, w1v_ref, w2v_ref, pv_ref).astype(vc_ref.dtype)


def _compress(chunks, kbase, vbase, w1k, w2k, pk, w1v, w2v, pv, cos_c, sin_c, batch):
    _, _, nch, cw = chunks.shape
    const = lambda a: pl.BlockSpec(a.shape, lambda b, g: (0,) * a.ndim)
    out_spec = pl.BlockSpec((None, None, nch, HEAD_DIM), lambda b, g: (b, g, 0, 0))
    out_sds = jax.ShapeDtypeStruct((batch, N_KV, nch, HEAD_DIM), MXU_DTYPE)
    return pl.pallas_call(
        _compress_kernel,
        grid=(batch, N_KV),
        in_specs=[pl.BlockSpec((None, None, nch, cw), lambda b, g: (kbase + g, b, 0, 0)),
                  pl.BlockSpec((None, None, nch, cw), lambda b, g: (vbase + g, b, 0, 0)),
                  const(w1k), const(w2k), const(pk), const(w1v), const(w2v), const(pv),
                  const(cos_c), const(sin_c)],
        out_specs=[out_spec, out_spec],
        out_shape=[out_sds, out_sds],
        compiler_params=_params("parallel", "parallel"),
        name="compress",
    )(chunks, chunks, w1k, w2k, pk, w1v, w2v, pv, cos_c, sin_c)


_Branch = collections.namedtuple("_Branch", "s p m acc bias")


def _attn_kernel(q_ref, kc_ref, vc_ref, ksx_ref, vsx_ref, kw_ref, vwx_ref, gate_ref, mselt_ref,
                 o_ref, qx_scr, qw_scr, sc_scr, pc_scr, sd_scr, pd_scr, sw_scr, pw_scr, sa_scr, sb_scr,
                 pa_scr, pb_scr, bc_scr, bd_scr, bw_scr, mc_scr, mw_scr, m_scr, a_scr,
                 accc_scr, accw_scr, acc_scr, psum_scr, imp_scr, rank_scr, *, tq, tk, rb):
    i = pl.program_id(2)
    q0 = pl.multiple_of(i * tq, tq)
    r = GROUP
    rows = r * tq
    ncp = kc_ref.shape[0]
    ns = mselt_ref.shape[0]
    wlen = WINDOW + tq
    n_chunks = rows // rb
    t_col = q0 + lax.broadcasted_iota(jnp.int32, (tq, 1), 0)
    q = q_ref[...].reshape(rows, HEAD_DIM)
    gates = gate_ref[...]
    head = lambda h: slice(h * tq, (h + 1) * tq)
    gate = lambda h, br: gates[:, h * N_NSA_BRANCH + br:h * N_NSA_BRANCH + br + 1]
    half = rows // 2
    halves = [slice(0, half), slice(half, rows)]
    lane = lambda kk: slice(kk * LANES, (kk + 1) * LANES)

    def load_scores(br, rs, bs, width):
        xs = [br.s[rs, lane(kk)] for kk in range(width // LANES)]
        return [x if b is None else x + b[bs, :] for x, b in zip(xs, br.bias)]

    def softmax_rows(br, width, chunks, first, cmp):
        def slices(c):
            r0 = c * rb
            return slice(r0, r0 + rb), slice(r0 % tq, r0 % tq + rb)

        for c in chunks:
            rs, bs = slices(c)
            xs = load_scores(br, rs, bs, width)
            mx = jnp.max(functools.reduce(jnp.maximum, xs), axis=-1, keepdims=True)
            if first:
                br.m[rs, :] = jnp.broadcast_to(mx, (rb, LANES))
            else:
                m_old = br.m[rs, :]
                m_new = jnp.maximum(m_old, mx)
                a_scr[rs, :] = jnp.exp2(m_old - m_new)
                br.m[rs, :] = m_new

        for c in chunks:
            rs, bs = slices(c)
            xs = load_scores(br, rs, bs, width)
            m = br.m[rs, :]
            ps = [jnp.exp2(x - m) for x in xs]
            if cmp:
                ps = [jnp.where(x > 0.5 * NEG, p, 0.0) for x, p in zip(xs, ps)]
                lsum = jnp.sum(functools.reduce(jnp.add, ps), axis=-1, keepdims=True)
                inv = 1.0 / jnp.where(lsum > 0.0, lsum, 1.0)
                ps = [p * inv for p in ps]
                for kk, p in enumerate(ps):
                    psum_scr[bs, lane(kk)] += p
            elif not first:
                a = a_scr[rs, :]
                br.acc[rs, :LANES] = a * br.acc[rs, :LANES]
                br.acc[rs, LANES:] = a * br.acc[rs, LANES:]
            for kk, p in enumerate(ps):
                br.p[rs, lane(kk)] = p.astype(br.p.dtype)

    def scores(br, width, q_rows, k):
        for hs in halves:
            br.s[hs, :width] = lax.dot_general(q_rows(hs), k, NT_DIMS, preferred_element_type=F32)

    def weighted_values(br, width, v, *, first, cmp=False):
        for hi, hs in enumerate(halves):
            chunks = range(hi * n_chunks // 2, (hi + 1) * n_chunks // 2)
            softmax_rows(br, width, chunks, first, cmp)
            pv = _dot(br.p[hs, :width], v)
            cols = slice(0, v.shape[1])
            if first:
                br.acc[hs, cols] = pv
            else:
                br.acc[hs, cols] += pv

    q_plain = lambda hs: q[hs]
    q_ext = lambda hs: qx_scr[hs, :]
    row_id = lax.broadcasted_iota(jnp.int32, (tq, LANES), 0)
    col_id = lax.broadcasted_iota(jnp.int32, (tq, LANES), 1)

    n_cl = ncp // LANES
    for kk in range(n_cl):
        cmp_end = (col_id + kk * LANES) * CMP_STRIDE + (CMP_LEN - 1)
        bc_scr[:, lane(kk)] = jnp.where(cmp_end <= t_col, 0.0, NEG)
    n_dl = tq // LANES
    for kk in range(n_dl):
        own = (col_id <= row_id - kk * LANES) & (row_id < (kk + 1) * LANES)
        bd_scr[:, lane(kk)] = jnp.where(own, 0.0, NEG)
    n_wl = wlen // LANES
    for kk in range(n_dl):
        bw_scr[:, lane(kk)] = jnp.where(col_id + kk * LANES > row_id, 0.0, NEG)
        bw_scr[:, lane(n_dl + kk)] = jnp.where(col_id + kk * LANES <= row_id, 0.0, NEG)

    cmp_br = _Branch(sc_scr, pc_scr, mc_scr, accc_scr, [bc_scr.at[:, lane(kk)] for kk in range(n_cl)])
    diag_br = _Branch(sd_scr, pd_scr, m_scr, acc_scr, [bd_scr.at[:, lane(kk)] for kk in range(n_dl)])
    win_bias = ([bw_scr.at[:, lane(kk)] for kk in range(n_dl)] + [None] * (n_wl - 2 * n_dl)
                + [bw_scr.at[:, lane(n_dl + kk)] for kk in range(n_dl)])
    win_br = _Branch(sw_scr, pw_scr, mw_scr, accw_scr, win_bias)
    sel_a = _Branch(sa_scr, pa_scr, m_scr, acc_scr, [None] * (tk // LANES))
    sel_b = _Branch(sb_scr, pb_scr, m_scr, acc_scr, [None] * (tk // LANES))

    psum_scr[...] = jnp.zeros(psum_scr.shape, F32)
    pad_col = jnp.where(lax.broadcasted_iota(jnp.int32, (rows, LANES), 1) == 0, NEG, 0.0)
    qw_scr[:, :LANES] = q
    qw_scr[:, LANES:] = pad_col.astype(qw_scr.dtype)
    scores(cmp_br, ncp, q_plain, kc_ref[...])
    w_rows = pl.ds(q0 + (kw_ref.shape[0] - ksx_ref.shape[0] - WINDOW), wlen)
    scores(win_br, wlen, lambda hs: qw_scr[hs, :], kw_ref[w_rows, :])

    weighted_values(cmp_br, ncp, vc_ref[...], first=True, cmp=True)
    scores(diag_br, tq, q_plain, ksx_ref[pl.ds(q0, tq), :LANES])
    mselt = mselt_ref[...]
    imp = jnp.zeros((ns, tq), F32)
    rem = psum_scr[...]
    for _ in range(3):
        piece = rem.astype(MXU_DTYPE)
        imp = imp + lax.dot_general(mselt, piece, NT_DIMS, preferred_element_type=F32)
        rem = rem - piece.astype(F32)
    blk = lax.broadcasted_iota(jnp.int32, (ns, tq), 0)
    tb = (q0 + lax.broadcasted_iota(jnp.int32, (ns, tq), 1)) // SEL_LEN
    forced = (blk == 0) | (blk == tb) | (blk == tb - 1)
    imp_scr[...] = jnp.where(blk <= tb, jnp.where(forced, jnp.inf, imp), -jnp.inf)
    rank_scr[...] = jnp.zeros(rank_scr.shape, F32)

    weighted_values(win_br, wlen, vwx_ref[w_rows, :], first=True)
    weighted_values(diag_br, tq, vsx_ref[pl.ds(q0, tq), :], first=True)

    sub = 8
    n_grp = ns // sub
    sub_id = lax.broadcasted_iota(jnp.int32, (sub, tq), 0)
    grp = lambda v: slice(sub * v, sub * (v + 1))
    for gm in range(n_grp):
        @pl.when(sub * gm * SEL_LEN <= q0 + tq - 1)
        def _():
            xs = [imp_scr[grp(v), :] for v in range(n_grp)]
            ranks = [rank_scr[grp(v), :] for v in range(n_grp)]
            for mp in range(sub * gm, sub * (gm + 1)):
                row = jnp.broadcast_to(xs[gm][mp % sub:mp % sub + 1, :], (sub, tq))
                for v, x in enumerate(xs):
                    if sub * v > mp:
                        ahead = row >= x
                    elif sub * v + sub - 1 <= mp:
                        ahead = row > x
                    else:
                        ahead = (row > x) | ((row == x) & (sub_id > mp % sub))
                    ranks[v] = ranks[v] + jnp.where(ahead, 1.0, 0.0)
            for v in range(n_grp):
                rank_scr[grp(v), :] = ranks[v]

    own_start = (q0 + lax.broadcasted_iota(jnp.int32, (ns, tq), 1)) // LANES * (LANES // SEL_LEN)
    off = jnp.where((rank_scr[...] < float(min(N_SEL, ns))) & (blk < own_start), 0.0, NEG)
    off = off.T.astype(MXU_DTYPE)
    if ns < LANES:
        off = jnp.concatenate([off, jnp.zeros((tq, LANES - ns), MXU_DTYPE)], axis=1)
    for h in range(r):
        qx_scr[head(h), :LANES] = q_ref[h]
        qx_scr[head(h), LANES:] = off

    last_tile = ksx_ref.shape[0] // tk - 1

    def tile_rows(t):
        return pl.ds(pl.multiple_of(jnp.minimum(t, last_tile) * tk, tk), tk)

    def sel_pair(jj, carry):
        t = 2 * jj
        scores(sel_b, tk, q_ext, ksx_ref[tile_rows(t + 1), :])
        weighted_values(sel_a, tk, vsx_ref[tile_rows(t), :], first=False)
        scores(sel_a, tk, q_ext, ksx_ref[tile_rows(t + 2), :])
        weighted_values(sel_b, tk, vsx_ref[tile_rows(t + 1), :], first=False)
        return carry

    n_tiles = (q0 + tq - LANES + tk - 1) // tk
    scores(sel_a, tk, q_ext, ksx_ref[tile_rows(0), :])
    lax.fori_loop(0, n_tiles // 2, sel_pair, 0)

    @pl.when(n_tiles % 2 == 1)
    def _():
        weighted_values(sel_a, tk, vsx_ref[tile_rows(n_tiles - 1), :], first=False)

    o_sel = acc_scr[:, :LANES] * (1.0 / acc_scr[:, LANES:])
    o_win = accw_scr[:, :LANES] * (1.0 / accw_scr[:, LANES:])
    for h in range(r):
        o = (gate(h, 0) * accc_scr[head(h), :] + gate(h, 1) * o_sel[head(h), :]
             + gate(h, 2) * o_win[head(h), :])
        o_ref[:, h * HEAD_DIM:(h + 1) * HEAD_DIM] = o.astype(o_ref.dtype)


def _attention(qkv, ksx, vsx, kwp, vwx, kc, vc, gates, mselt, batch, seq, q_base, tq, tk):
    nq = seq // tq
    ncp = kc.shape[2]
    ns = mselt.shape[0]
    assert ns <= LANES and seq % (2 * tk) == 0 and tk % tq == 0 and WINDOW % tq == 0
    assert tq % LANES == 0 and ncp % LANES == 0 and kwp.shape[2] >= seq + WINDOW
    rows = GROUP * tq
    wlen = WINDOW + tq
    ext_spec = pl.BlockSpec((None, seq, 2 * LANES), lambda b, g, i: (g, b, 0))
    cmp_spec = pl.BlockSpec((None, None, ncp, HEAD_DIM), lambda b, g, i: (b, g, 0, 0))
    f32 = lambda *shape: pltpu.VMEM(shape, F32)
    mxu = lambda *shape: pltpu.VMEM(shape, MXU_DTYPE)
    return pl.pallas_call(
        functools.partial(_attn_kernel, tq=tq, tk=tk, rb=16),
        grid=(batch, N_KV, nq),
        in_specs=[pl.BlockSpec((GROUP, tq, HEAD_DIM), lambda b, g, i: (q_base // GROUP + g, b * nq + i, 0)),
                  cmp_spec, cmp_spec, ext_spec, ext_spec,
                  pl.BlockSpec((None, None) + kwp.shape[2:], lambda b, g, i: (g, b, 0, 0)),
                  pl.BlockSpec((None, None) + vwx.shape[2:], lambda b, g, i: (g, b, 0, 0)),
                  pl.BlockSpec((None, tq, LANES), lambda b, g, i: (g, b * nq + i, 0)),
                  pl.BlockSpec((ns, ncp), lambda b, g, i: (0, 0))],
        out_specs=pl.BlockSpec((tq, GROUP * HEAD_DIM), lambda b, g, i: (b * nq + i, g)),
        out_shape=jax.ShapeDtypeStruct((batch * seq, N_HEADS * HEAD_DIM), MXU_DTYPE),
        scratch_shapes=[mxu(rows, 2 * LANES), mxu(rows, 2 * LANES),
                        f32(rows, ncp), mxu(rows, ncp),
                        f32(rows, tq), mxu(rows, tq),
                        f32(rows, wlen), mxu(rows, wlen),
                        f32(rows, tk), f32(rows, tk),
                        mxu(rows, tk), mxu(rows, tk),
                        f32(tq, ncp), f32(tq, tq), f32(tq, 2 * tq),
                        f32(rows, LANES), f32(rows, LANES), f32(rows, LANES),
                        f32(rows, LANES),
                        f32(rows, LANES),
                        f32(rows, 2 * LANES), f32(rows, 2 * LANES),
                        f32(tq, ncp),
                        f32(ns, tq), f32(ns, tq)],
        compiler_params=_params("parallel", "parallel", "arbitrary"),
        name="nsa_attention",
    )(qkv, kc, vc, ksx, vsx, kwp, vwx, gates, mselt)


def _conv_kernel(h_ref, wx_ref, wb_ref, wc_ref, cw_ref, o_ref, ubuf, *, tiles_per_seq):
    i = pl.program_id(1)
    tm = h_ref.shape[0]
    h = h_ref[...]

    @pl.when(i % tiles_per_seq == 0)
    def _():
        ubuf[0:8, :] = jnp.zeros((8, ubuf.shape[1]), F32)

    w = cw_ref[...]
    for cs in _column_parts(o_ref.shape[1]):
        x_in = _dot(h, wx_ref[:, cs])
        gate_b = _dot(h, wb_ref[:, cs])
        gate_c = _dot(h, wc_ref[:, cs])
        ubuf[8:tm + 8, cs] = gate_c * x_in
        conv = (w[2:3, cs] * ubuf[8:tm + 8, cs] + w[1:2, cs] * ubuf[7:tm + 7, cs]
                + w[0:1, cs] * ubuf[6:tm + 6, cs])
        o_ref[:, cs] = (gate_b * conv).astype(o_ref.dtype)
        ubuf[0:8, cs] = ubuf[tm:tm + 8, cs]


def _conv_mixer(h, wx, wb, wc, cw, seq, tm, tn):
    m, d = h.shape
    n = wx.shape[1]
    wspec = pl.BlockSpec((d, tn), lambda j, i: (0, j))
    return pl.pallas_call(
        functools.partial(_conv_kernel, tiles_per_seq=seq // tm),
        grid=(n // tn, m // tm),
        in_specs=[pl.BlockSpec((tm, d), lambda j, i: (i, 0)), wspec, wspec, wspec,
                  pl.BlockSpec((8, tn), lambda j, i: (0, j))],
        out_specs=pl.BlockSpec((tm, tn), lambda j, i: (i, j)),
        out_shape=jax.ShapeDtypeStruct((m, n), MXU_DTYPE),
        scratch_shapes=[pltpu.VMEM((tm + 8, tn), F32)],
        compiler_params=_params("parallel", "arbitrary"),
        name="conv_mixer",
    )(h, wx, wb, wc, cw)


def _merge_kernel(oa_ref, v_ref, h_ref, wap_ref, wco_ref, wga_ref, wgc_ref, o_ref):
    h = h_ref[...]
    oa = oa_ref[...]
    v = v_ref[...]
    for cs in _column_parts(o_ref.shape[1]):
        y_attn = _dot(oa, wap_ref[:, cs])
        y_conv = _dot(v, wco_ref[:, cs])
        g_attn = jax.nn.sigmoid(_dot(h, wga_ref[:, cs]))
        g_conv = jax.nn.sigmoid(_dot(h, wgc_ref[:, cs]))
        o_ref[:, cs] = (g_attn * y_attn + g_conv * y_conv).astype(o_ref.dtype)


def _merge(oa, v, h, wap, wco, wga, wgc, tm, tn):
    m, d = h.shape
    n = wap.shape[1]
    aspec = pl.BlockSpec((tm, d), lambda i, j: (i, 0))
    wspec = pl.BlockSpec((d, tn), lambda i, j: (0, j))
    return pl.pallas_call(
        _merge_kernel,
        grid=(m // tm, n // tn),
        in_specs=[aspec, aspec, aspec, wspec, wspec, wspec, wspec],
        out_specs=pl.BlockSpec((tm, tn), lambda i, j: (i, j)),
        out_shape=jax.ShapeDtypeStruct((m, n), MXU_DTYPE),
        compiler_params=_params("parallel", "arbitrary"),
        name="gated_merge",
    )(oa, v, h, wap, wco, wga, wgc)


def _outproj_kernel(a_ref, w_ref, x_ref, g_ref, xo_ref, ho_ref):
    x = x_ref[...] + _dot(a_ref[...], w_ref[...])
    xo_ref[...] = x
    ho_ref[...] = _rms(x, g_ref[...]).astype(ho_ref.dtype)


def _outproj(a, w, x, g, tm):
    m, d = x.shape
    row = pl.BlockSpec((tm, d), lambda i: (i, 0))
    return pl.pallas_call(
        _outproj_kernel,
        grid=(m // tm,),
        in_specs=[row, pl.BlockSpec((d, d), lambda i: (0, 0)), row,
                  pl.BlockSpec((1, d), lambda i: (0, 0))],
        out_specs=[row, row],
        out_shape=[jax.ShapeDtypeStruct((m, d), F32), jax.ShapeDtypeStruct((m, d), MXU_DTYPE)],
        compiler_params=_params("parallel"),
        name="out_proj",
    )(a, w, x, g.reshape(1, d))


def _mlp_kernel(h_ref, wu_ref, wd_ref, x_ref, g_ref, *refs, last):
    acc = refs[-1]
    f = pl.program_id(1)

    @pl.when(f == 0)
    def _():
        acc[...] = x_ref[...]

    a = jnp.maximum(_dot(h_ref[...], wu_ref[...]), 0.0)
    acc[...] += _dot((a * a).astype(MXU_DTYPE), wd_ref[...])

    @pl.when(f == pl.num_programs(1) - 1)
    def _():
        x = acc[...]
        normed = _rms(x, g_ref[...])
        if last:
            refs[0][...] = normed
        else:
            refs[0][...] = x
            refs[1][...] = normed.astype(refs[1].dtype)


def _mlp(h, wu, wd, x, g, tm, tf, last):
    m, d = x.shape
    ff = wu.shape[1]
    row = pl.BlockSpec((tm, d), lambda i, f: (i, 0))
    if last:
        out_specs, out_shape = [row], [jax.ShapeDtypeStruct((m, d), F32)]
    else:
        out_specs = [row, row]
        out_shape = [jax.ShapeDtypeStruct((m, d), F32), jax.ShapeDtypeStruct((m, d), MXU_DTYPE)]
    return pl.pallas_call(
        functools.partial(_mlp_kernel, last=last),
        grid=(m // tm, ff // tf),
        in_specs=[row, pl.BlockSpec((d, tf), lambda i, f: (0, f)),
                  pl.BlockSpec((tf, d), lambda i, f: (f, 0)), row,
                  pl.BlockSpec((1, d), lambda i, f: (0, 0))],
        out_specs=out_specs,
        out_shape=out_shape,
        scratch_shapes=[pltpu.VMEM((tm, d), F32)],
        compiler_params=_params("parallel", "arbitrary"),
        name="relu2_mlp",
    )(h, wu, wd, x, g.reshape(1, d))


def _rope_tables(pos):
    half = HEAD_DIM // 2
    inv_freq = jnp.exp(-math.log(ROPE_THETA) * jnp.arange(half, dtype=F32) / half)
    ang = pos.astype(F32)[:, None] * inv_freq[None, :]
    cos, sin = jnp.cos(ang), jnp.sin(ang)
    return jnp.concatenate([cos, cos], axis=-1), jnp.concatenate([-sin, sin], axis=-1)


def _cmp_to_sel_t(ncp, ns):
    nc = ncp - 1
    cs = np.arange(nc) * CMP_STRIDE
    ss = np.arange(ns) * SEL_LEN
    ov = np.minimum(cs[:, None] + CMP_LEN, ss[None, :] + SEL_LEN) - np.maximum(cs[:, None], ss[None, :])
    m = np.zeros((ncp, ns), np.float32)
    m[:nc] = np.clip(ov, 0, None) / CMP_LEN
    return jnp.asarray(m.T, dtype=MXU_DTYPE)


def _block_onehot(seq):
    e = (np.arange(seq)[:, None] // SEL_LEN == np.arange(LANES)[None, :]).astype(np.float32)
    return jnp.asarray(e, dtype=MXU_DTYPE)


def kernel(x, norm1_g, w_in, cmp_pos_k, cmp_w1_k, cmp_w2_k, cmp_pos_v, cmp_w1_v, cmp_w2_v,
           conv_w, w_attn_proj, w_conv_out, w_o, norm2_g, w_up, w_down, final_g):
    batch, seq, d = x.shape
    depth = w_in.shape[0]
    m = batch * seq
    attn_dim = N_HEADS * HEAD_DIM
    kv_dim = N_KV * HEAD_DIM
    n_gate = N_HEADS * N_NSA_BRANCH
    ncp = seq // CMP_STRIDE
    ns = seq // SEL_LEN
    cast = lambda a: a.astype(MXU_DTYPE)

    tm = min(1024, seq)
    tm_small = min(512, seq)
    tq = 256
    tk = 512

    cos_t, sin_t = _rope_tables(jnp.arange(seq))
    cos_c, sin_c = _rope_tables(jnp.arange(ncp) * CMP_STRIDE + CMP_LEN - 1)
    mselt = _cmp_to_sel_t(ncp, ns)
    onehot = jnp.broadcast_to(jnp.tile(_block_onehot(seq), (batch, 1))[None], (N_KV, m, LANES))
    zeros = jnp.zeros((N_KV, m, LANES), MXU_DTYPE)
    ksx0 = jnp.concatenate([zeros, onehot], axis=-1)
    vsx0 = jnp.concatenate([zeros, jnp.ones_like(zeros)], axis=-1)
    before_start = jnp.asarray(np.arange(2 * LANES) == LANES, dtype=MXU_DTYPE)
    front = lambda a, rows: jnp.concatenate(
        [jnp.broadcast_to(rows, (N_KV, batch, tm, 2 * LANES)),
         a.reshape(N_KV, batch, seq, 2 * LANES)], axis=2)
    kwx0 = front(jnp.zeros_like(vsx0), before_start)
    vwx0 = front(vsx0, jnp.zeros((), MXU_DTYPE))

    q_base, kc_base = 0, 16

    x2 = x.reshape(m, d)
    h = _rmsnorm(x2, norm1_g[0], tm)
    out = None
    for l in range(depth):
        wl = w_in[l]
        o_kv = attn_dim
        kv = [wl[:, o_kv + j * kv_dim:o_kv + (j + 1) * kv_dim] for j in range(6)]
        w_qkv = cast(jnp.concatenate([wl[:, :attn_dim], kv[2], kv[4], kv[0], kv[1], kv[3], kv[5]], axis=1))
        o_ng = o_kv + 6 * kv_dim
        w_ng = wl[:, o_ng:o_ng + n_gate].reshape(d, N_KV, GROUP * N_NSA_BRANCH)
        w_ng = cast(jnp.pad(w_ng, ((0, 0), (0, 0), (0, LANES - GROUP * N_NSA_BRANCH))).reshape(d, N_KV * LANES))
        o_cv = o_ng + n_gate
        w_x, w_b, w_c = [cast(wl[:, o_cv + j * d:o_cv + (j + 1) * d]) for j in range(3)]
        o_mg = o_cv + 3 * d
        w_ga, w_gc = [cast(wl[:, o_mg + j * d:o_mg + (j + 1) * d]) for j in range(2)]

        qkv, ksx, vsx, kwx, vwx = _qkv_proj(h, w_qkv, cos_t, sin_t, ksx0, vsx0, kwx0, vwx0, seq, tm)
        gates = _gate_proj(h, w_ng, tm)

        chunks = qkv[kc_base:kc_base + 2 * N_KV].reshape(2 * N_KV, batch, ncp, CMP_STRIDE * HEAD_DIM)
        pad_pos = lambda p: cast(jnp.pad(p.reshape(1, CMP_LEN * HEAD_DIM), ((0, 7), (0, 0))))
        kc, vc = _compress(chunks, 0, N_KV, cast(cmp_w1_k[l]), cast(cmp_w2_k[l]), pad_pos(cmp_pos_k[l]),
                           cast(cmp_w1_v[l]), cast(cmp_w2_v[l]), pad_pos(cmp_pos_v[l]), cos_c, sin_c, batch)

        o_attn = _attention(qkv, ksx, vsx, kwx, vwx, kc, vc, gates, mselt, batch, seq, q_base, tq, tk)

        cw = jnp.pad(conv_w[l], ((0, 8 - CONV_WIDTH), (0, 0)))
        v_conv = _conv_mixer(h, w_x, w_b, w_c, cw, seq, tm_small, 512)

        merged = _merge(o_attn, v_conv, h, cast(w_attn_proj[l]), cast(w_conv_out[l]), w_ga, w_gc,
                        tm_small, 512)
        x2, h2 = _outproj(merged, cast(w_o[l]), x2, norm2_g[l], tm_small)

        last = l == depth - 1
        g_next = final_g if last else norm1_g[l + 1]
        res = _mlp(h2, cast(w_up[l]), cast(w_down[l]), x2, g_next, tm_small, 1024, last)
        if last:
            out = res[0]
        else:
            x2, h = res
    return out.reshape(batch, seq, d)
```

```python
import collections
import functools
import math

import numpy as np
import jax
import jax.numpy as jnp
from jax import lax
from jax.experimental import pallas as pl
from jax.experimental.pallas import tpu as pltpu

N_HEADS = 16
HEAD_DIM = 128
N_KV = 4
GROUP = N_HEADS // N_KV
CMP_LEN = 32
CMP_STRIDE = 16
SEL_LEN = 64
N_SEL = 16
WINDOW = 512
N_NSA_BRANCH = 3
CONV_WIDTH = 3
ROPE_THETA = 10000.0
EPS = 1e-6

MXU_DTYPE = jnp.bfloat16
F32 = jnp.float32
NEG = -1e30
VMEM_LIMIT = 56 * 1024 * 1024
LANES = 128
MXU_WIDTH = 256
NT_DIMS = (((1,), (1,)), ((), ()))


def _params(*sem):
    return pltpu.CompilerParams(dimension_semantics=sem, vmem_limit_bytes=VMEM_LIMIT)


def _dot(a, b):
    return jnp.dot(a, b, preferred_element_type=F32)


def _rms(x, g):
    return x * lax.rsqrt(jnp.mean(x * x, axis=-1, keepdims=True) + EPS) * g


def _column_parts(n):
    return [slice(c, c + MXU_WIDTH) for c in range(0, n, MXU_WIDTH)]


def _norm_kernel(x_ref, g_ref, o_ref):
    o_ref[...] = _rms(x_ref[...], g_ref[...]).astype(o_ref.dtype)


def _rmsnorm(x2d, g, tm):
    m, d = x2d.shape
    return pl.pallas_call(
        _norm_kernel,
        grid=(m // tm,),
        in_specs=[pl.BlockSpec((tm, d), lambda i: (i, 0)), pl.BlockSpec((1, d), lambda i: (0, 0))],
        out_specs=pl.BlockSpec((tm, d), lambda i: (i, 0)),
        out_shape=jax.ShapeDtypeStruct((m, d), MXU_DTYPE),
        compiler_params=_params("parallel"),
        name="rmsnorm",
    )(x2d, g.reshape(1, d))


QKV_TILES = ("q", "q", "q", "q", "k_sel", "k_win", "k_cmp", "v_cmp", "v_sel", "v_win")


def _qkv_kernel(h_ref, w_ref, cos_ref, sin_ref, ksx_in, vsx_in, kwx_in, vwx_in,
                o_ref, cmp_ref, ksx_ref, vsx_ref, kwx_ref, vwx_ref, stage, *, q_scale):
    del ksx_in, vsx_in, kwx_in, vwx_in
    j = pl.program_id(1)
    dest = {"q": o_ref, "k_cmp": cmp_ref, "v_cmp": cmp_ref, "k_sel": ksx_ref, "v_sel": vsx_ref,
            "k_win": kwx_ref, "v_win": vwx_ref}

    def put(ref, rope, scale, chunked):
        h = h_ref[...]
        for cs in _column_parts(w_ref.shape[1]):
            acc = _dot(h, w_ref[:, cs])
            for c in range(MXU_WIDTH // LANES):
                xc = acc[:, c * LANES:(c + 1) * LANES]
                head = cs.start // LANES + c
                if rope:
                    xc = (xc * (cos_ref[...] * scale)
                          + pltpu.roll(xc, HEAD_DIM // 2, 1) * (sin_ref[...] * scale))
                if chunked:
                    stage[...] = xc
                    n_rows = stage.shape[0] // CMP_STRIDE
                    for l in range(CMP_STRIDE):
                        ref[head, :, l * LANES:(l + 1) * LANES] = (
                            stage[pl.ds(l, n_rows, stride=CMP_STRIDE), :].astype(ref.dtype))
                else:
                    ref[head] = xc.astype(ref.dtype)

    for name in dict.fromkeys(QKV_TILES):
        tiles = [t for t, n in enumerate(QKV_TILES) if n == name]

        @pl.when((j >= tiles[0]) & (j <= tiles[-1]))
        def _():
            put(dest[name], rope=name in ("q", "k_sel", "k_win"), scale=q_scale if name == "q" else 1.0,
                chunked=name in ("k_cmp", "v_cmp"))


def _qkv_proj(h, w, cos, sin, ksx, vsx, kwx, vwx, seq, tm):
    m, d = h.shape
    tn = N_KV * LANES
    assert w.shape[1] == tn * len(QKV_TILES) and seq % tm == 0 and kwx.shape[2] == seq + tm
    tps = seq // tm
    n_q = QKV_TILES.count("q")
    cmp_first = QKV_TILES.index("k_cmp")
    assert QKV_TILES[cmp_first + 1] == "v_cmp" and tm % (16 * CMP_STRIDE) == 0
    ext_spec = pl.BlockSpec((N_KV, tm, LANES), lambda i, j: (0, i, 0))
    win_spec = pl.BlockSpec((N_KV, None, tm, LANES), lambda i, j: (0, i // tps, 1 + i % tps, 0))
    any_spec = pl.BlockSpec(memory_space=pl.ANY)
    sds = lambda a: jax.ShapeDtypeStruct(a.shape, a.dtype)
    kern = functools.partial(_qkv_kernel, q_scale=HEAD_DIM ** -0.5 * math.log2(math.e))
    return pl.pallas_call(
        kern,
        grid=(m // tm, len(QKV_TILES)),
        in_specs=[pl.BlockSpec((tm, d), lambda i, j: (i, 0)),
                  pl.BlockSpec((d, tn), lambda i, j: (0, j)),
                  pl.BlockSpec((tm, LANES), lambda i, j: (i % tps, 0)),
                  pl.BlockSpec((tm, LANES), lambda i, j: (i % tps, 0)),
                  any_spec, any_spec, any_spec, any_spec],
        out_specs=[pl.BlockSpec((N_KV, tm, LANES), lambda i, j: (jnp.minimum(j, n_q - 1), i, 0)),
                   pl.BlockSpec((N_KV, tm // CMP_STRIDE, CMP_STRIDE * LANES),
                                lambda i, j: (jnp.clip(j - cmp_first, 0, 1), i, 0)),
                   ext_spec, ext_spec, win_spec, win_spec],
        out_shape=[jax.ShapeDtypeStruct((n_q * N_KV, m, LANES), MXU_DTYPE),
                   jax.ShapeDtypeStruct((2 * N_KV, m // CMP_STRIDE, CMP_STRIDE * LANES), MXU_DTYPE),
                   sds(ksx), sds(vsx), sds(kwx), sds(vwx)],
        scratch_shapes=[pltpu.VMEM((tm, LANES), F32)],
        input_output_aliases={4: 2, 5: 3, 6: 4, 7: 5},
        compiler_params=_params("parallel", "arbitrary"),
        name="qkv_proj",
    )(h, w, cos, sin, ksx, vsx, kwx, vwx)


def _gate_kernel(h_ref, w_ref, o_ref):
    acc = _dot(h_ref[...], w_ref[...])
    for c in range(o_ref.shape[0]):
        o_ref[c] = jax.nn.sigmoid(acc[:, c * LANES:(c + 1) * LANES])


def _gate_proj(h, w, tm):
    m, d = h.shape
    n = w.shape[1]
    return pl.pallas_call(
        _gate_kernel,
        grid=(m // tm,),
        in_specs=[pl.BlockSpec((tm, d), lambda i: (i, 0)), pl.BlockSpec((d, n), lambda i: (0, 0))],
        out_specs=pl.BlockSpec((n // LANES, tm, LANES), lambda i: (0, i, 0)),
        out_shape=jax.ShapeDtypeStruct((n // LANES, m, LANES), F32),
        compiler_params=_params("parallel"),
        name="nsa_gate_proj",
    )(h, w)


def _compress_kernel(ck_ref, cv_ref, w1k_ref, w2k_ref, pk_ref, w1v_ref, w2v_ref, pv_ref,
                     cos_ref, sin_ref, kc_ref, vc_ref):
    nch = ck_ref.shape[0]
    half = w1k_ref.shape[0] // 2

    def phi(c_ref, w1_ref, w2_ref, p_ref):
        c = c_ref[...]
        first = _dot(c, w1_ref[:half, :])
        second = _dot(c, w1_ref[half:, :])
        pos = _dot(p_ref[...], w1_ref[...])[0:1, :]
        hid = first + pltpu.roll(second, nch - 1, 0) + pos
        act = hid * jax.nn.sigmoid(hid)
        return _dot(act.astype(MXU_DTYPE), w2_ref[...])

    kc = phi(ck_ref, w1k_ref, w2k_ref, pk_ref)
    kc = kc * cos_ref[...] + pltpu.roll(kc, HEAD_DIM // 2, 1) * sin_ref[...]
    kc_ref[...] = kc.astype(kc_ref.dtype)
    vc_ref[...] = phi(cv_ref, w1v_ref, w2v_ref, pv_ref).astype(vc_ref.dtype)


def _compress(chunks, kbase, vbase, w1k, w2k, pk, w1v, w2v, pv, cos_c, sin_c, batch):
    _, _, nch, cw = chunks.shape
    const = lambda a: pl.BlockSpec(a.shape, lambda b, g: (0,) * a.ndim)
    out_spec = pl.BlockSpec((None, None, nch, HEAD_DIM), lambda b, g: (b, g, 0, 0))
    out_sds = jax.ShapeDtypeStruct((batch, N_KV, nch, HEAD_DIM), MXU_DTYPE)
    return pl.pallas_call(
        _compress_kernel,
        grid=(batch, N_KV),
        in_specs=[pl.BlockSpec((None, None, nch, cw), lambda b, g: (kbase + g, b, 0, 0)),
                  pl.BlockSpec((None, None, nch, cw), lambda b, g: (vbase + g, b, 0, 0)),
                  const(w1k), const(w2k), const(pk), const(w1v), const(w2v), const(pv),
                  const(cos_c), const(sin_c)],
        out_specs=[out_spec, out_spec],
        out_shape=[out_sds, out_sds],
        compiler_params=_params("parallel", "parallel"),
        name="compress",
    )(chunks, chunks, w1k, w2k, pk, w1v, w2v, pv, cos_c, sin_c)


_Branch = collections.namedtuple("_Branch", "s p m acc bias")


def _attn_kernel(q_ref, kc_ref, vc_ref, ksx_ref, vsx_ref, kw_ref, vwx_ref, gate_ref, mselt_ref,
                 o_ref, qx_scr, qw_scr, sc_scr, pc_scr, sd_scr, pd_scr, sw_scr, pw_scr, sa_scr, sb_scr,
                 pa_scr, pb_scr, bc_scr, bd_scr, bw_scr, mc_scr, mw_scr, m_scr, a_scr,
                 accc_scr, accw_scr, acc_scr, psum_scr, imp_scr, rank_scr, *, tq, tk, rb):
    i = pl.program_id(2)
    q0 = pl.multiple_of(i * tq, tq)
    r = GROUP
    rows = r * tq
    ncp = kc_ref.shape[0]
    ns = mselt_ref.shape[0]
    wlen = WINDOW + tq
    n_chunks = rows // rb
    t_col = q0 + lax.broadcasted_iota(jnp.int32, (tq, 1), 0)
    q = q_ref[...].reshape(rows, HEAD_DIM)
    gates = gate_ref[...]
    head = lambda h: slice(h * tq, (h + 1) * tq)
    gate = lambda h, br: gates[:, h * N_NSA_BRANCH + br:h * N_NSA_BRANCH + br + 1]
    half = rows // 2
    halves = [slice(0, half), slice(half, rows)]
    lane = lambda kk: slice(kk * LANES, (kk + 1) * LANES)

    def load_scores(br, rs, bs, width):
        xs = [br.s[rs, lane(kk)] for kk in range(width // LANES)]
        return [x if b is None else x + b[bs, :] for x, b in zip(xs, br.bias)]

    def softmax_rows(br, width, chunks, first, cmp):
        def slices(c):
            r0 = c * rb
            return slice(r0, r0 + rb), slice(r0 % tq, r0 % tq + rb)

        for c in chunks:
            rs, bs = slices(c)
            xs = load_scores(br, rs, bs, width)
            mx = jnp.max(functools.reduce(jnp.maximum, xs), axis=-1, keepdims=True)
            if first:
                br.m[rs, :] = jnp.broadcast_to(mx, (rb, LANES))
            else:
                m_old = br.m[rs, :]
                m_new = jnp.maximum(m_old, mx)
                a_scr[rs, :] = jnp.exp2(m_old - m_new)
                br.m[rs, :] = m_new

        for c in chunks:
            rs, bs = slices(c)
            xs = load_scores(br, rs, bs, width)
            m = br.m[rs, :]
            ps = [jnp.exp2(x - m) for x in xs]
            if cmp:
                ps = [jnp.where(x > 0.5 * NEG, p, 0.0) for x, p in zip(xs, ps)]
                lsum = jnp.sum(functools.reduce(jnp.add, ps), axis=-1, keepdims=True)
                inv = 1.0 / jnp.where(lsum > 0.0, lsum, 1.0)
                ps = [p * inv for p in ps]
                for kk, p in enumerate(ps):
                    psum_scr[bs, lane(kk)] += p
            for kk, p in enumerate(ps):
                br.p[rs, lane(kk)] = p.astype(br.p.dtype)

    def scores(br, width, q_rows, k):
        for hs in halves:
            br.s[hs, :width] = lax.dot_general(q_rows(hs), k, NT_DIMS, preferred_element_type=F32)

    def weighted_values(br, width, v, *, first, cmp=False):
        for hi, hs in enumerate(halves):
            chunks = range(hi * n_chunks // 2, (hi + 1) * n_chunks // 2)
            softmax_rows(br, width, chunks, first, cmp)
            pv = _dot(br.p[hs, :width], v)
            cols = slice(0, v.shape[1])
            if first:
                br.acc[hs, cols] = pv
            else:
                a = a_scr[hs, :]
                for kk in range(v.shape[1] // LANES):
                    br.acc[hs, lane(kk)] = a * br.acc[hs, lane(kk)] + pv[:, lane(kk)]

    q_plain = lambda hs: q[hs]
    q_ext = lambda hs: qx_scr[hs, :]
    row_id = lax.broadcasted_iota(jnp.int32, (tq, LANES), 0)
    col_id = lax.broadcasted_iota(jnp.int32, (tq, LANES), 1)

    n_cl = ncp // LANES
    for kk in range(n_cl):
        cmp_end = (col_id + kk * LANES) * CMP_STRIDE + (CMP_LEN - 1)
        bc_scr[:, lane(kk)] = jnp.where(cmp_end <= t_col, 0.0, NEG)
    n_dl = tq // LANES
    for kk in range(n_dl):
        own = (col_id <= row_id - kk * LANES) & (row_id < (kk + 1) * LANES)
        bd_scr[:, lane(kk)] = jnp.where(own, 0.0, NEG)
    n_wl = wlen // LANES
    for kk in range(n_dl):
        bw_scr[:, lane(kk)] = jnp.where(col_id + kk * LANES > row_id, 0.0, NEG)
        bw_scr[:, lane(n_dl + kk)] = jnp.where(col_id + kk * LANES <= row_id, 0.0, NEG)

    cmp_br = _Branch(sc_scr, pc_scr, mc_scr, accc_scr, [bc_scr.at[:, lane(kk)] for kk in range(n_cl)])
    diag_br = _Branch(sd_scr, pd_scr, m_scr, acc_scr, [bd_scr.at[:, lane(kk)] for kk in range(n_dl)])
    win_bias = ([bw_scr.at[:, lane(kk)] for kk in range(n_dl)] + [None] * (n_wl - 2 * n_dl)
                + [bw_scr.at[:, lane(n_dl + kk)] for kk in range(n_dl)])
    win_br = _Branch(sw_scr, pw_scr, mw_scr, accw_scr, win_bias)
    sel_a = _Branch(sa_scr, pa_scr, m_scr, acc_scr, [None] * (tk // LANES))
    sel_b = _Branch(sb_scr, pb_scr, m_scr, acc_scr, [None] * (tk // LANES))

    psum_scr[...] = jnp.zeros(psum_scr.shape, F32)
    pad_col = jnp.where(lax.broadcasted_iota(jnp.int32, (rows, LANES), 1) == 0, NEG, 0.0)
    qw_scr[:, :LANES] = q
    qw_scr[:, LANES:] = pad_col.astype(qw_scr.dtype)
    scores(cmp_br, ncp, q_plain, kc_ref[...])
    w_rows = pl.ds(q0 + (kw_ref.shape[0] - ksx_ref.shape[0] - WINDOW), wlen)
    scores(win_br, wlen, lambda hs: qw_scr[hs, :], kw_ref[w_rows, :])

    weighted_values(cmp_br, ncp, vc_ref[...], first=True, cmp=True)
    scores(diag_br, tq, q_plain, ksx_ref[pl.ds(q0, tq), :LANES])
    mselt = mselt_ref[...]
    imp = jnp.zeros((ns, tq), F32)
    rem = psum_scr[...]
    for _ in range(3):
        piece = rem.astype(MXU_DTYPE)
        imp = imp + lax.dot_general(mselt, piece, NT_DIMS, preferred_element_type=F32)
        rem = rem - piece.astype(F32)
    blk = lax.broadcasted_iota(jnp.int32, (ns, tq), 0)
    tb = (q0 + lax.broadcasted_iota(jnp.int32, (ns, tq), 1)) // SEL_LEN
    forced = (blk == 0) | (blk == tb) | (blk == tb - 1)
    imp_scr[...] = jnp.where(blk <= tb, jnp.where(forced, jnp.inf, imp), -jnp.inf)
    rank_scr[...] = jnp.zeros(rank_scr.shape, F32)

    weighted_values(win_br, wlen, vwx_ref[w_rows, :], first=True)
    weighted_values(diag_br, tq, vsx_ref[pl.ds(q0, tq), :], first=True)

    sub = 8
    n_grp = ns // sub
    sub_id = lax.broadcasted_iota(jnp.int32, (sub, tq), 0)
    grp = lambda v: slice(sub * v, sub * (v + 1))
    for gm in range(n_grp):
        @pl.when(sub * gm * SEL_LEN <= q0 + tq - 1)
        def _():
            xs = [imp_scr[grp(v), :] for v in range(n_grp)]
            ranks = [rank_scr[grp(v), :] for v in range(n_grp)]
            for mp in range(sub * gm, sub * (gm + 1)):
                row = jnp.broadcast_to(xs[gm][mp % sub:mp % sub + 1, :], (sub, tq))
                for v, x in enumerate(xs):
                    if sub * v > mp:
                        ahead = row >= x
                    elif sub * v + sub - 1 <= mp:
                        ahead = row > x
                    else:
                        ahead = (row > x) | ((row == x) & (sub_id > mp % sub))
                    ranks[v] = ranks[v] + jnp.where(ahead, 1.0, 0.0)
            for v in range(n_grp):
                rank_scr[grp(v), :] = ranks[v]

    own_start = (q0 + lax.broadcasted_iota(jnp.int32, (ns, tq), 1)) // LANES * (LANES // SEL_LEN)
    off = jnp.where((rank_scr[...] < float(min(N_SEL, ns))) & (blk < own_start), 0.0, NEG)
    off = off.T.astype(MXU_DTYPE)
    if ns < LANES:
        off = jnp.concatenate([off, jnp.zeros((tq, LANES - ns), MXU_DTYPE)], axis=1)
    for h in range(r):
        qx_scr[head(h), :LANES] = q_ref[h]
        qx_scr[head(h), LANES:] = off

    last_tile = ksx_ref.shape[0] // tk - 1

    def tile_rows(t):
        return pl.ds(pl.multiple_of(jnp.minimum(t, last_tile) * tk, tk), tk)

    def sel_pair(jj, carry):
        t = 2 * jj
        scores(sel_b, tk, q_ext, ksx_ref[tile_rows(t + 1), :])
        weighted_values(sel_a, tk, vsx_ref[tile_rows(t), :], first=False)
        scores(sel_a, tk, q_ext, ksx_ref[tile_rows(t + 2), :])
        weighted_values(sel_b, tk, vsx_ref[tile_rows(t + 1), :], first=False)
        return carry

    n_tiles = (q0 + tq - LANES + tk - 1) // tk
    scores(sel_a, tk, q_ext, ksx_ref[tile_rows(0), :])
    lax.fori_loop(0, n_tiles // 2, sel_pair, 0)

    @pl.when(n_tiles % 2 == 1)
    def _():
        weighted_values(sel_a, tk, vsx_ref[tile_rows(n_tiles - 1), :], first=False)

    o_sel = acc_scr[:, :LANES] * (1.0 / acc_scr[:, LANES:])
    o_win = accw_scr[:, :LANES] * (1.0 / accw_scr[:, LANES:])
    for h in range(r):
        o = (gate(h, 0) * accc_scr[head(h), :] + gate(h, 1) * o_sel[head(h), :]
             + gate(h, 2) * o_win[head(h), :])
        o_ref[:, h * HEAD_DIM:(h + 1) * HEAD_DIM] = o.astype(o_ref.dtype)


def _attention(qkv, ksx, vsx, kwp, vwx, kc, vc, gates, mselt, batch, seq, q_base, tq, tk):
    nq = seq // tq
    ncp = kc.shape[2]
    ns = mselt.shape[0]
    assert ns <= LANES and seq % (2 * tk) == 0 and tk % tq == 0 and WINDOW % tq == 0
    assert tq % LANES == 0 and ncp % LANES == 0 and kwp.shape[2] >= seq + WINDOW
    rows = GROUP * tq
    wlen = WINDOW + tq
    ext_spec = pl.BlockSpec((None, seq, 2 * LANES), lambda b, g, i: (g, b, 0))
    cmp_spec = pl.BlockSpec((None, None, ncp, HEAD_DIM), lambda b, g, i: (b, g, 0, 0))
    f32 = lambda *shape: pltpu.VMEM(shape, F32)
    mxu = lambda *shape: pltpu.VMEM(shape, MXU_DTYPE)
    return pl.pallas_call(
        functools.partial(_attn_kernel, tq=tq, tk=tk, rb=16),
        grid=(batch, N_KV, nq),
        in_specs=[pl.BlockSpec((GROUP, tq, HEAD_DIM), lambda b, g, i: (q_base // GROUP + g, b * nq + i, 0)),
                  cmp_spec, cmp_spec, ext_spec, ext_spec,
                  pl.BlockSpec((None, None) + kwp.shape[2:], lambda b, g, i: (g, b, 0, 0)),
                  pl.BlockSpec((None, None) + vwx.shape[2:], lambda b, g, i: (g, b, 0, 0)),
                  pl.BlockSpec((None, tq, LANES), lambda b, g, i: (g, b * nq + i, 0)),
                  pl.BlockSpec((ns, ncp), lambda b, g, i: (0, 0))],
        out_specs=pl.BlockSpec((tq, GROUP * HEAD_DIM), lambda b, g, i: (b * nq + i, g)),
        out_shape=jax.ShapeDtypeStruct((batch * seq, N_HEADS * HEAD_DIM), MXU_DTYPE),
        scratch_shapes=[mxu(rows, 2 * LANES), mxu(rows, 2 * LANES),
                        f32(rows, ncp), mxu(rows, ncp),
                        f32(rows, tq), mxu(rows, tq),
                        f32(rows, wlen), mxu(rows, wlen),
                        f32(rows, tk), f32(rows, tk),
                        mxu(rows, tk), mxu(rows, tk),
                        f32(tq, ncp), f32(tq, tq), f32(tq, 2 * tq),
                        f32(rows, LANES), f32(rows, LANES), f32(rows, LANES),
                        f32(rows, LANES),
                        f32(rows, LANES),
                        f32(rows, 2 * LANES), f32(rows, 2 * LANES),
                        f32(tq, ncp),
                        f32(ns, tq), f32(ns, tq)],
        compiler_params=_params("parallel", "parallel", "arbitrary"),
        name="nsa_attention",
    )(qkv, kc, vc, ksx, vsx, kwp, vwx, gates, mselt)


def _conv_kernel(h_ref, wx_ref, wb_ref, wc_ref, cw_ref, o_ref, ubuf, *, tiles_per_seq):
    i = pl.program_id(1)
    tm = h_ref.shape[0]
    h = h_ref[...]

    @pl.when(i % tiles_per_seq == 0)
    def _():
        ubuf[0:8, :] = jnp.zeros((8, ubuf.shape[1]), F32)

    w = cw_ref[...]
    for cs in _column_parts(o_ref.shape[1]):
        x_in = _dot(h, wx_ref[:, cs])
        gate_b = _dot(h, wb_ref[:, cs])
        gate_c = _dot(h, wc_ref[:, cs])
        ubuf[8:tm + 8, cs] = gate_c * x_in
        conv = (w[2:3, cs] * ubuf[8:tm + 8, cs] + w[1:2, cs] * ubuf[7:tm + 7, cs]
                + w[0:1, cs] * ubuf[6:tm + 6, cs])
        o_ref[:, cs] = (gate_b * conv).astype(o_ref.dtype)
        ubuf[0:8, cs] = ubuf[tm:tm + 8, cs]


def _conv_mixer(h, wx, wb, wc, cw, seq, tm, tn):
    m, d = h.shape
    n = wx.shape[1]
    wspec = pl.BlockSpec((d, tn), lambda j, i: (0, j))
    return pl.pallas_call(
        functools.partial(_conv_kernel, tiles_per_seq=seq // tm),
        grid=(n // tn, m // tm),
        in_specs=[pl.BlockSpec((tm, d), lambda j, i: (i, 0)), wspec, wspec, wspec,
                  pl.BlockSpec((8, tn), lambda j, i: (0, j))],
        out_specs=pl.BlockSpec((tm, tn), lambda j, i: (i, j)),
        out_shape=jax.ShapeDtypeStruct((m, n), MXU_DTYPE),
        scratch_shapes=[pltpu.VMEM((tm + 8, tn), F32)],
        compiler_params=_params("parallel", "arbitrary"),
        name="conv_mixer",
    )(h, wx, wb, wc, cw)


def _merge_kernel(oa_ref, v_ref, h_ref, wap_ref, wco_ref, wga_ref, wgc_ref, o_ref):
    h = h_ref[...]
    oa = oa_ref[...]
    v = v_ref[...]
    for cs in _column_parts(o_ref.shape[1]):
        y_attn = _dot(oa, wap_ref[:, cs])
        y_conv = _dot(v, wco_ref[:, cs])
        g_attn = jax.nn.sigmoid(_dot(h, wga_ref[:, cs]))
        g_conv = jax.nn.sigmoid(_dot(h, wgc_ref[:, cs]))
        o_ref[:, cs] = (g_attn * y_attn + g_conv * y_conv).astype(o_ref.dtype)


def _merge(oa, v, h, wap, wco, wga, wgc, tm, tn):
    m, d = h.shape
    n = wap.shape[1]
    aspec = pl.BlockSpec((tm, d), lambda i, j: (i, 0))
    wspec = pl.BlockSpec((d, tn), lambda i, j: (0, j))
    return pl.pallas_call(
        _merge_kernel,
        grid=(m // tm, n // tn),
        in_specs=[aspec, aspec, aspec, wspec, wspec, wspec, wspec],
        out_specs=pl.BlockSpec((tm, tn), lambda i, j: (i, j)),
        out_shape=jax.ShapeDtypeStruct((m, n), MXU_DTYPE),
        compiler_params=_params("parallel", "arbitrary"),
        name="gated_merge",
    )(oa, v, h, wap, wco, wga, wgc)


def _outproj_kernel(a_ref, w_ref, x_ref, g_ref, xo_ref, ho_ref):
    x = x_ref[...] + _dot(a_ref[...], w_ref[...])
    xo_ref[...] = x
    ho_ref[...] = _rms(x, g_ref[...]).astype(ho_ref.dtype)


def _outproj(a, w, x, g, tm):
    m, d = x.shape
    row = pl.BlockSpec((tm, d), lambda i: (i, 0))
    return pl.pallas_call(
        _outproj_kernel,
        grid=(m // tm,),
        in_specs=[row, pl.BlockSpec((d, d), lambda i: (0, 0)), row,
                  pl.BlockSpec((1, d), lambda i: (0, 0))],
        out_specs=[row, row],
        out_shape=[jax.ShapeDtypeStruct((m, d), F32), jax.ShapeDtypeStruct((m, d), MXU_DTYPE)],
        compiler_params=_params("parallel"),
        name="out_proj",
    )(a, w, x, g.reshape(1, d))


def _mlp_kernel(h_ref, wu_ref, wd_ref, x_ref, g_ref, *refs, last):
    acc = refs[-1]
    f = pl.program_id(1)

    @pl.when(f == 0)
    def _():
        acc[...] = x_ref[...]

    a = jnp.maximum(_dot(h_ref[...], wu_ref[...]), 0.0)
    acc[...] += _dot((a * a).astype(MXU_DTYPE), wd_ref[...])

    @pl.when(f == pl.num_programs(1) - 1)
    def _():
        x = acc[...]
        normed = _rms(x, g_ref[...])
        if last:
            refs[0][...] = normed
        else:
            refs[0][...] = x
            refs[1][...] = normed.astype(refs[1].dtype)


def _mlp(h, wu, wd, x, g, tm, tf, last):
    m, d = x.shape
    ff = wu.shape[1]
    row = pl.BlockSpec((tm, d), lambda i, f: (i, 0))
    if last:
        out_specs, out_shape = [row], [jax.ShapeDtypeStruct((m, d), F32)]
    else:
        out_specs = [row, row]
        out_shape = [jax.ShapeDtypeStruct((m, d), F32), jax.ShapeDtypeStruct((m, d), MXU_DTYPE)]
    return pl.pallas_call(
        functools.partial(_mlp_kernel, last=last),
        grid=(m // tm, ff // tf),
        in_specs=[row, pl.BlockSpec((d, tf), lambda i, f: (0, f)),
                  pl.BlockSpec((tf, d), lambda i, f: (f, 0)), row,
                  pl.BlockSpec((1, d), lambda i, f: (0, 0))],
        out_specs=out_specs,
        out_shape=out_shape,
        scratch_shapes=[pltpu.VMEM((tm, d), F32)],
        compiler_params=_params("parallel", "arbitrary"),
        name="relu2_mlp",
    )(h, wu, wd, x, g.reshape(1, d))


def _rope_tables(pos):
    half = HEAD_DIM // 2
    inv_freq = jnp.exp(-math.log(ROPE_THETA) * jnp.arange(half, dtype=F32) / half)
    ang = pos.astype(F32)[:, None] * inv_freq[None, :]
    cos, sin = jnp.cos(ang), jnp.sin(ang)
    return jnp.concatenate([cos, cos], axis=-1), jnp.concatenate([-sin, sin], axis=-1)


def _cmp_to_sel_t(ncp, ns):
    nc = ncp - 1
    cs = np.arange(nc) * CMP_STRIDE
    ss = np.arange(ns) * SEL_LEN
    ov = np.minimum(cs[:, None] + CMP_LEN, ss[None, :] + SEL_LEN) - np.maximum(cs[:, None], ss[None, :])
    m = np.zeros((ncp, ns), np.float32)
    m[:nc] = np.clip(ov, 0, None) / CMP_LEN
    return jnp.asarray(m.T, dtype=MXU_DTYPE)


def _block_onehot(seq):
    e = (np.arange(seq)[:, None] // SEL_LEN == np.arange(LANES)[None, :]).astype(np.float32)
    return jnp.asarray(e, dtype=MXU_DTYPE)


def kernel(x, norm1_g, w_in, cmp_pos_k, cmp_w1_k, cmp_w2_k, cmp_pos_v, cmp_w1_v, cmp_w2_v,
           conv_w, w_attn_proj, w_conv_out, w_o, norm2_g, w_up, w_down, final_g):
    batch, seq, d = x.shape
    depth = w_in.shape[0]
    m = batch * seq
    attn_dim = N_HEADS * HEAD_DIM
    kv_dim = N_KV * HEAD_DIM
    n_gate = N_HEADS * N_NSA_BRANCH
    ncp = seq // CMP_STRIDE
    ns = seq // SEL_LEN
    cast = lambda a: a.astype(MXU_DTYPE)

    tm = min(1024, seq)
    tm_small = min(512, seq)
    tq = 256
    tk = 512

    cos_t, sin_t = _rope_tables(jnp.arange(seq))
    cos_c, sin_c = _rope_tables(jnp.arange(ncp) * CMP_STRIDE + CMP_LEN - 1)
    mselt = _cmp_to_sel_t(ncp, ns)
    onehot = jnp.broadcast_to(jnp.tile(_block_onehot(seq), (batch, 1))[None], (N_KV, m, LANES))
    zeros = jnp.zeros((N_KV, m, LANES), MXU_DTYPE)
    ksx0 = jnp.concatenate([zeros, onehot], axis=-1)
    vsx0 = jnp.concatenate([zeros, jnp.ones_like(zeros)], axis=-1)
    before_start = jnp.asarray(np.arange(2 * LANES) == LANES, dtype=MXU_DTYPE)
    front = lambda a, rows: jnp.concatenate(
        [jnp.broadcast_to(rows, (N_KV, batch, tm, 2 * LANES)),
         a.reshape(N_KV, batch, seq, 2 * LANES)], axis=2)
    kwx0 = front(jnp.zeros_like(vsx0), before_start)
    vwx0 = front(vsx0, jnp.zeros((), MXU_DTYPE))

    q_base = 0

    x2 = x.reshape(m, d)
    h = _rmsnorm(x2, norm1_g[0], tm)
    out = None
    for l in range(depth):
        wl = w_in[l]
        o_kv = attn_dim
        kv = [wl[:, o_kv + j * kv_dim:o_kv + (j + 1) * kv_dim] for j in range(6)]
        w_qkv = cast(jnp.concatenate([wl[:, :attn_dim], kv[2], kv[4], kv[0], kv[1], kv[3], kv[5]], axis=1))
        o_ng = o_kv + 6 * kv_dim
        w_ng = wl[:, o_ng:o_ng + n_gate].reshape(d, N_KV, GROUP * N_NSA_BRANCH)
        w_ng = cast(jnp.pad(w_ng, ((0, 0), (0, 0), (0, LANES - GROUP * N_NSA_BRANCH))).reshape(d, N_KV * LANES))
        o_cv = o_ng + n_gate
        w_x, w_b, w_c = [cast(wl[:, o_cv + j * d:o_cv + (j + 1) * d]) for j in range(3)]
        o_mg = o_cv + 3 * d
        w_ga, w_gc = [cast(wl[:, o_mg + j * d:o_mg + (j + 1) * d]) for j in range(2)]

        qkv, cmp_kv, ksx, vsx, kwx, vwx = _qkv_proj(h, w_qkv, cos_t, sin_t, ksx0, vsx0, kwx0, vwx0, seq, tm)
        ksx0, vsx0, kwx0, vwx0 = ksx, vsx, kwx, vwx
        gates = _gate_proj(h, w_ng, tm)

        chunks = cmp_kv.reshape(2 * N_KV, batch, ncp, CMP_STRIDE * HEAD_DIM)
        pad_pos = lambda p: cast(jnp.pad(p.reshape(1, CMP_LEN * HEAD_DIM), ((0, 7), (0, 0))))
        kc, vc = _compress(chunks, 0, N_KV, cast(cmp_w1_k[l]), cast(cmp_w2_k[l]), pad_pos(cmp_pos_k[l]),
                           cast(cmp_w1_v[l]), cast(cmp_w2_v[l]), pad_pos(cmp_pos_v[l]), cos_c, sin_c, batch)

        o_attn = _attention(qkv, ksx, vsx, kwx, vwx, kc, vc, gates, mselt, batch, seq, q_base, tq, tk)

        cw = jnp.pad(conv_w[l], ((0, 8 - CONV_WIDTH), (0, 0)))
        v_conv = _conv_mixer(h, w_x, w_b, w_c, cw, seq, tm_small, 512)

        merged = _merge(o_attn, v_conv, h, cast(w_attn_proj[l]), cast(w_conv_out[l]), w_ga, w_gc,
                        tm_small, 512)
        x2, h2 = _outproj(merged, cast(w_o[l]), x2, norm2_g[l], tm_small)

        last = l == depth - 1
        g_next = final_g if last else norm1_g[l + 1]
        res = _mlp(h2, cast(w_up[l]), cast(w_down[l]), x2, g_next, tm_small, 1024, last)
        if last:
            out = res[0]
        else:
            x2, h = res
    return out.reshape(batch, seq, d)
```

```python
import collections
import functools
import math

import numpy as np
import jax
import jax.numpy as jnp
from jax import lax
from jax.experimental import pallas as pl
from jax.experimental.pallas import tpu as pltpu

N_HEADS = 16
HEAD_DIM = 128
N_KV = 4
GROUP = N_HEADS // N_KV
CMP_LEN = 32
CMP_STRIDE = 16
SEL_LEN = 64
N_SEL = 16
WINDOW = 512
N_NSA_BRANCH = 3
CONV_WIDTH = 3
ROPE_THETA = 10000.0
EPS = 1e-6

MXU_DTYPE = jnp.bfloat16
F32 = jnp.float32
NEG = -1e30
VMEM_LIMIT = 56 * 1024 * 1024
LANES = 128
MXU_WIDTH = 256
NT_DIMS = (((1,), (1,)), ((), ()))


def _params(*sem):
    return pltpu.CompilerParams(dimension_semantics=sem, vmem_limit_bytes=VMEM_LIMIT)


def _dot(a, b):
    return jnp.dot(a, b, preferred_element_type=F32)


def _rms(x, g):
    return x * lax.rsqrt(jnp.mean(x * x, axis=-1, keepdims=True) + EPS) * g


def _column_parts(n):
    return [slice(c, c + MXU_WIDTH) for c in range(0, n, MXU_WIDTH)]


def _norm_kernel(x_ref, g_ref, o_ref):
    o_ref[...] = _rms(x_ref[...], g_ref[...]).astype(o_ref.dtype)


def _rmsnorm(x2d, g, tm):
    m, d = x2d.shape
    return pl.pallas_call(
        _norm_kernel,
        grid=(m // tm,),
        in_specs=[pl.BlockSpec((tm, d), lambda i: (i, 0)), pl.BlockSpec((1, d), lambda i: (0, 0))],
        out_specs=pl.BlockSpec((tm, d), lambda i: (i, 0)),
        out_shape=jax.ShapeDtypeStruct((m, d), MXU_DTYPE),
        compiler_params=_params("parallel"),
        name="rmsnorm",
    )(x2d, g.reshape(1, d))


QKV_TILES = ("q", "q", "q", "q", "k_sel", "k_win", "k_cmp", "v_cmp", "v_sel", "v_win")


def _qkv_kernel(h_ref, w_ref, cos_ref, sin_ref, ksx_in, vsx_in, kwx_in, vwx_in,
                o_ref, cmp_ref, ksx_ref, vsx_ref, kwx_ref, vwx_ref, stage, *, q_scale):
    del ksx_in, vsx_in, kwx_in, vwx_in
    j = pl.program_id(1)
    dest = {"q": o_ref, "k_cmp": cmp_ref, "v_cmp": cmp_ref, "k_sel": ksx_ref, "v_sel": vsx_ref,
            "k_win": kwx_ref, "v_win": vwx_ref}

    def put(ref, rope, scale, chunked):
        h = h_ref[...]
        for cs in _column_parts(w_ref.shape[1]):
            acc = _dot(h, w_ref[:, cs])
            for c in range(MXU_WIDTH // LANES):
                xc = acc[:, c * LANES:(c + 1) * LANES]
                head = cs.start // LANES + c
                if rope:
                    xc = (xc * (cos_ref[...] * scale)
                          + pltpu.roll(xc, HEAD_DIM // 2, 1) * (sin_ref[...] * scale))
                if chunked:
                    stage[...] = xc
                    n_rows = stage.shape[0] // CMP_STRIDE
                    for l in range(CMP_STRIDE):
                        ref[head, :, l * LANES:(l + 1) * LANES] = (
                            stage[pl.ds(l, n_rows, stride=CMP_STRIDE), :].astype(ref.dtype))
                else:
                    ref[head] = xc.astype(ref.dtype)

    for name in dict.fromkeys(QKV_TILES):
        tiles = [t for t, n in enumerate(QKV_TILES) if n == name]

        @pl.when((j >= tiles[0]) & (j <= tiles[-1]))
        def _():
            put(dest[name], rope=name in ("q", "k_sel", "k_win"), scale=q_scale if name == "q" else 1.0,
                chunked=name in ("k_cmp", "v_cmp"))


def _qkv_proj(h, w, cos, sin, ksx, vsx, kwx, vwx, seq, tm):
    m, d = h.shape
    tn = N_KV * LANES
    assert w.shape[1] == tn * len(QKV_TILES) and seq % tm == 0 and kwx.shape[2] == seq + tm
    tps = seq // tm
    n_q = QKV_TILES.count("q")
    cmp_first = QKV_TILES.index("k_cmp")
    assert QKV_TILES[cmp_first + 1] == "v_cmp" and tm % (16 * CMP_STRIDE) == 0
    ext_spec = pl.BlockSpec((N_KV, tm, LANES), lambda i, j: (0, i, 0))
    win_spec = pl.BlockSpec((N_KV, None, tm, LANES), lambda i, j: (0, i // tps, 1 + i % tps, 0))
    any_spec = pl.BlockSpec(memory_space=pl.ANY)
    sds = lambda a: jax.ShapeDtypeStruct(a.shape, a.dtype)
    kern = functools.partial(_qkv_kernel, q_scale=HEAD_DIM ** -0.5 * math.log2(math.e))
    return pl.pallas_call(
        kern,
        grid=(m // tm, len(QKV_TILES)),
        in_specs=[pl.BlockSpec((tm, d), lambda i, j: (i, 0)),
                  pl.BlockSpec((d, tn), lambda i, j: (0, j)),
                  pl.BlockSpec((tm, LANES), lambda i, j: (i % tps, 0)),
                  pl.BlockSpec((tm, LANES), lambda i, j: (i % tps, 0)),
                  any_spec, any_spec, any_spec, any_spec],
        out_specs=[pl.BlockSpec((N_KV, tm, LANES), lambda i, j: (jnp.minimum(j, n_q - 1), i, 0)),
                   pl.BlockSpec((N_KV, tm // CMP_STRIDE, CMP_STRIDE * LANES),
                                lambda i, j: (jnp.clip(j - cmp_first, 0, 1), i, 0)),
                   ext_spec, ext_spec, win_spec, win_spec],
        out_shape=[jax.ShapeDtypeStruct((n_q * N_KV, m, LANES), MXU_DTYPE),
                   jax.ShapeDtypeStruct((2 * N_KV, m // CMP_STRIDE, CMP_STRIDE * LANES), MXU_DTYPE),
                   sds(ksx), sds(vsx), sds(kwx), sds(vwx)],
        scratch_shapes=[pltpu.VMEM((tm, LANES), F32)],
        input_output_aliases={4: 2, 5: 3, 6: 4, 7: 5},
        compiler_params=_params("parallel", "arbitrary"),
        name="qkv_proj",
    )(h, w, cos, sin, ksx, vsx, kwx, vwx)


def _gate_kernel(h_ref, w_ref, o_ref):
    acc = _dot(h_ref[...], w_ref[...])
    for c in range(o_ref.shape[0]):
        o_ref[c] = jax.nn.sigmoid(acc[:, c * LANES:(c + 1) * LANES])


def _gate_proj(h, w, tm):
    m, d = h.shape
    n = w.shape[1]
    return pl.pallas_call(
        _gate_kernel,
        grid=(m // tm,),
        in_specs=[pl.BlockSpec((tm, d), lambda i: (i, 0)), pl.BlockSpec((d, n), lambda i: (0, 0))],
        out_specs=pl.BlockSpec((n // LANES, tm, LANES), lambda i: (0, i, 0)),
        out_shape=jax.ShapeDtypeStruct((n // LANES, m, LANES), F32),
        compiler_params=_params("parallel"),
        name="nsa_gate_proj",
    )(h, w)


def _compress_kernel(ck_ref, cv_ref, w1k_ref, w2k_ref, pk_ref, w1v_ref, w2v_ref, pv_ref,
                     cos_ref, sin_ref, kc_ref, vc_ref):
    nch = ck_ref.shape[0]
    half = w1k_ref.shape[0] // 2

    def phi(c_ref, w1_ref, w2_ref, p_ref):
        c = c_ref[...]
        first = _dot(c, w1_ref[:half, :])
        second = _dot(c, w1_ref[half:, :])
        pos = _dot(p_ref[...], w1_ref[...])[0:1, :]
        hid = first + pltpu.roll(second, nch - 1, 0) + pos
        act = hid * jax.nn.sigmoid(hid)
        return _dot(act.astype(MXU_DTYPE), w2_ref[...])

    kc = phi(ck_ref, w1k_ref, w2k_ref, pk_ref)
    kc = kc * cos_ref[...] + pltpu.roll(kc, HEAD_DIM // 2, 1) * sin_ref[...]
    kc_ref[...] = kc.astype(kc_ref.dtype)
    vc_ref[...] = phi(cv_ref, w1v_ref, w2v_ref, pv_ref).astype(vc_ref.dtype)


def _compress(chunks, kbase, vbase, w1k, w2k, pk, w1v, w2v, pv, cos_c, sin_c, batch):
    _, _, nch, cw = chunks.shape
    const = lambda a: pl.BlockSpec(a.shape, lambda b, g: (0,) * a.ndim)
    out_spec = pl.BlockSpec((None, None, nch, HEAD_DIM), lambda b, g: (b, g, 0, 0))
    out_sds = jax.ShapeDtypeStruct((batch, N_KV, nch, HEAD_DIM), MXU_DTYPE)
    return pl.pallas_call(
        _compress_kernel,
        grid=(batch, N_KV),
        in_specs=[pl.BlockSpec((None, None, nch, cw), lambda b, g: (kbase + g, b, 0, 0)),
                  pl.BlockSpec((None, None, nch, cw), lambda b, g: (vbase + g, b, 0, 0)),
                  const(w1k), const(w2k), const(pk), const(w1v), const(w2v), const(pv),
                  const(cos_c), const(sin_c)],
        out_specs=[out_spec, out_spec],
        out_shape=[out_sds, out_sds],
        compiler_params=_params("parallel", "parallel"),
        name="compress",
    )(chunks, chunks, w1k, w2k, pk, w1v, w2v, pv, cos_c, sin_c)


_Branch = collections.namedtuple("_Branch", "s p m acc bias")


def _attn_kernel(q_ref, kc_ref, vc_ref, ksx_ref, vsx_ref, kw_ref, vwx_ref, gate_ref, mselt_ref,
                 o_ref, qx_scr, qw_scr, sc_scr, pc_scr, sd_scr, pd_scr, sw_scr, pw_scr, sa_scr, sb_scr,
                 pa_scr, pb_scr, bc_scr, bd_scr, bw_scr, mc_scr, mw_scr, m_scr, a_scr,
                 accc_scr, accw_scr, acc_scr, psum_scr, imp_scr, rank_scr, *, tq, tk, rb):
    i = pl.program_id(2)
    q0 = pl.multiple_of(i * tq, tq)
    r = GROUP
    rows = r * tq
    ncp = kc_ref.shape[0]
    ns = mselt_ref.shape[0]
    wlen = WINDOW + tq
    n_chunks = rows // rb
    t_col = q0 + lax.broadcasted_iota(jnp.int32, (tq, 1), 0)
    q = q_ref[...].reshape(rows, HEAD_DIM)
    gates = gate_ref[...]
    head = lambda h: slice(h * tq, (h + 1) * tq)
    gate = lambda h, br: gates[:, h * N_NSA_BRANCH + br:h * N_NSA_BRANCH + br + 1]
    half = rows // 2
    halves = [slice(0, half), slice(half, rows)]
    lane = lambda kk: slice(kk * LANES, (kk + 1) * LANES)

    def load_scores(br, rs, bs, width):
        xs = [br.s[rs, lane(kk)] for kk in range(width // LANES)]
        return [x if b is None else x + b[bs, :] for x, b in zip(xs, br.bias)]

    def softmax_rows(br, width, chunks, first, cmp):
        def slices(c):
            r0 = c * rb
            return slice(r0, r0 + rb), slice(r0 % tq, r0 % tq + rb)

        for c in chunks:
            rs, bs = slices(c)
            xs = load_scores(br, rs, bs, width)
            mx = jnp.max(functools.reduce(jnp.maximum, xs), axis=-1, keepdims=True)
            if first:
                br.m[rs, :] = jnp.broadcast_to(mx, (rb, LANES))
            else:
                m_old = br.m[rs, :]
                m_new = jnp.maximum(m_old, mx)
                a_scr[rs, :] = jnp.exp2(m_old - m_new)
                br.m[rs, :] = m_new

        for c in chunks:
            rs, bs = slices(c)
            xs = load_scores(br, rs, bs, width)
            m = br.m[rs, :]
            ps = [jnp.exp2(x - m) for x in xs]
            if cmp:
                ps = [jnp.where(x > 0.5 * NEG, p, 0.0) for x, p in zip(xs, ps)]
                lsum = jnp.sum(functools.reduce(jnp.add, ps), axis=-1, keepdims=True)
                inv = 1.0 / jnp.where(lsum > 0.0, lsum, 1.0)
                ps = [p * inv for p in ps]
                for kk, p in enumerate(ps):
                    psum_scr[bs, lane(kk)] += p
            for kk, p in enumerate(ps):
                br.p[rs, lane(kk)] = p.astype(br.p.dtype)

    def scores(br, width, q_rows, k):
        for hs in halves:
            br.s[hs, :width] = lax.dot_general(q_rows(hs), k, NT_DIMS, preferred_element_type=F32)

    def weighted_values(br, width, v, *, first, cmp=False):
        for hi, hs in enumerate(halves):
            chunks = range(hi * n_chunks // 2, (hi + 1) * n_chunks // 2)
            softmax_rows(br, width, chunks, first, cmp)
            pv = _dot(br.p[hs, :width], v)
            cols = slice(0, v.shape[1])
            if first:
                br.acc[hs, cols] = pv
            else:
                a = a_scr[hs, :]
                for kk in range(v.shape[1] // LANES):
                    br.acc[hs, lane(kk)] = a * br.acc[hs, lane(kk)] + pv[:, lane(kk)]

    q_plain = lambda hs: q[hs]
    q_ext = lambda hs: qx_scr[hs, :]
    row_id = lax.broadcasted_iota(jnp.int32, (tq, LANES), 0)
    col_id = lax.broadcasted_iota(jnp.int32, (tq, LANES), 1)

    n_cl = ncp // LANES
    for kk in range(n_cl):
        cmp_end = (col_id + kk * LANES) * CMP_STRIDE + (CMP_LEN - 1)
        bc_scr[:, lane(kk)] = jnp.where(cmp_end <= t_col, 0.0, NEG)
    n_dl = tq // LANES
    for kk in range(n_dl):
        own = (col_id <= row_id - kk * LANES) & (row_id < (kk + 1) * LANES)
        bd_scr[:, lane(kk)] = jnp.where(own, 0.0, NEG)
    n_wl = wlen // LANES
    for kk in range(n_dl):
        bw_scr[:, lane(kk)] = jnp.where(col_id + kk * LANES > row_id, 0.0, NEG)
        bw_scr[:, lane(n_dl + kk)] = jnp.where(col_id + kk * LANES <= row_id, 0.0, NEG)

    cmp_br = _Branch(sc_scr, pc_scr, mc_scr, accc_scr, [bc_scr.at[:, lane(kk)] for kk in range(n_cl)])
    diag_br = _Branch(sd_scr, pd_scr, m_scr, acc_scr, [bd_scr.at[:, lane(kk)] for kk in range(n_dl)])
    win_bias = ([bw_scr.at[:, lane(kk)] for kk in range(n_dl)] + [None] * (n_wl - 2 * n_dl)
                + [bw_scr.at[:, lane(n_dl + kk)] for kk in range(n_dl)])
    win_br = _Branch(sw_scr, pw_scr, mw_scr, accw_scr, win_bias)
    sel_a = _Branch(sa_scr, pa_scr, m_scr, acc_scr, [None] * (tk // LANES))
    sel_b = _Branch(sb_scr, pb_scr, m_scr, acc_scr, [None] * (tk // LANES))

    psum_scr[...] = jnp.zeros(psum_scr.shape, F32)
    pad_col = jnp.where(lax.broadcasted_iota(jnp.int32, (rows, LANES), 1) == 0, NEG, 0.0)
    qw_scr[:, :LANES] = q
    qw_scr[:, LANES:] = pad_col.astype(qw_scr.dtype)
    scores(cmp_br, ncp, q_plain, kc_ref[...])
    w_rows = pl.ds(q0 + (kw_ref.shape[0] - ksx_ref.shape[0] - WINDOW), wlen)
    scores(win_br, wlen, lambda hs: qw_scr[hs, :], kw_ref[w_rows, :])

    weighted_values(cmp_br, ncp, vc_ref[...], first=True, cmp=True)
    scores(diag_br, tq, q_plain, ksx_ref[pl.ds(q0, tq), :LANES])
    mselt = mselt_ref[...]
    imp = jnp.zeros((ns, tq), F32)
    rem = psum_scr[...]
    for _ in range(3):
        piece = rem.astype(MXU_DTYPE)
        imp = imp + lax.dot_general(mselt, piece, NT_DIMS, preferred_element_type=F32)
        rem = rem - piece.astype(F32)
    blk = lax.broadcasted_iota(jnp.int32, (ns, tq), 0)
    tb = (q0 + lax.broadcasted_iota(jnp.int32, (ns, tq), 1)) // SEL_LEN
    forced = (blk == 0) | (blk == tb) | (blk == tb - 1)
    imp_scr[...] = jnp.where(blk <= tb, jnp.where(forced, jnp.inf, imp), -jnp.inf)
    rank_scr[...] = jnp.zeros(rank_scr.shape, F32)

    weighted_values(win_br, wlen, vwx_ref[w_rows, :], first=True)
    weighted_values(diag_br, tq, vsx_ref[pl.ds(q0, tq), :], first=True)

    sub = 8
    n_grp = ns // sub
    sub_id = lax.broadcasted_iota(jnp.int32, (sub, tq), 0)
    grp = lambda v: slice(sub * v, sub * (v + 1))
    for gm in range(n_grp):
        @pl.when(sub * gm * SEL_LEN <= q0 + tq - 1)
        def _():
            xs = [imp_scr[grp(v), :] for v in range(n_grp)]
            ranks = [rank_scr[grp(v), :] for v in range(n_grp)]
            for mp in range(sub * gm, sub * (gm + 1)):
                row = jnp.broadcast_to(xs[gm][mp % sub:mp % sub + 1, :], (sub, tq))
                for v, x in enumerate(xs):
                    if sub * v > mp:
                        ahead = row >= x
                    elif sub * v + sub - 1 <= mp:
                        ahead = row > x
                    else:
                        ahead = (row > x) | ((row == x) & (sub_id > mp % sub))
                    ranks[v] = ranks[v] + jnp.where(ahead, 1.0, 0.0)
            for v in range(n_grp):
                rank_scr[grp(v), :] = ranks[v]

    own_start = (q0 + lax.broadcasted_iota(jnp.int32, (ns, tq), 1)) // LANES * (LANES // SEL_LEN)
    off = jnp.where((rank_scr[...] < float(min(N_SEL, ns))) & (blk < own_start), 0.0, NEG)
    off = off.T.astype(MXU_DTYPE)
    if ns < LANES:
        off = jnp.concatenate([off, jnp.zeros((tq, LANES - ns), MXU_DTYPE)], axis=1)
    for h in range(r):
        qx_scr[head(h), :LANES] = q_ref[h]
        qx_scr[head(h), LANES:] = off

    last_tile = ksx_ref.shape[0] // tk - 1

    def tile_rows(t):
        return pl.ds(pl.multiple_of(jnp.minimum(t, last_tile) * tk, tk), tk)

    def sel_pair(jj, carry):
        t = 2 * jj
        scores(sel_b, tk, q_ext, ksx_ref[tile_rows(t + 1), :])
        weighted_values(sel_a, tk, vsx_ref[tile_rows(t), :], first=False)
        scores(sel_a, tk, q_ext, ksx_ref[tile_rows(t + 2), :])
        weighted_values(sel_b, tk, vsx_ref[tile_rows(t + 1), :], first=False)
        return carry

    n_tiles = (q0 + tq - LANES + tk - 1) // tk
    scores(sel_a, tk, q_ext, ksx_ref[tile_rows(0), :])
    lax.fori_loop(0, n_tiles // 2, sel_pair, 0)

    @pl.when(n_tiles % 2 == 1)
    def _():
        weighted_values(sel_a, tk, vsx_ref[tile_rows(n_tiles - 1), :], first=False)

    o_sel = acc_scr[:, :LANES] * (1.0 / acc_scr[:, LANES:])
    o_win = accw_scr[:, :LANES] * (1.0 / accw_scr[:, LANES:])
    for h in range(r):
        o = (gate(h, 0) * accc_scr[head(h), :] + gate(h, 1) * o_sel[head(h), :]
             + gate(h, 2) * o_win[head(h), :])
        o_ref[:, h * HEAD_DIM:(h + 1) * HEAD_DIM] = o.astype(o_ref.dtype)


def _attention(qkv, ksx, vsx, kwp, vwx, kc, vc, gates, mselt, batch, seq, q_base, tq, tk):
    nq = seq // tq
    ncp = kc.shape[2]
    ns = mselt.shape[0]
    assert ns <= LANES and seq % (2 * tk) == 0 and tk % tq == 0 and WINDOW % tq == 0
    assert tq % LANES == 0 and ncp % LANES == 0 and kwp.shape[2] >= seq + WINDOW
    rows = GROUP * tq
    wlen = WINDOW + tq
    ext_spec = pl.BlockSpec((None, seq, 2 * LANES), lambda b, g, i: (g, b, 0))
    cmp_spec = pl.BlockSpec((None, None, ncp, HEAD_DIM), lambda b, g, i: (b, g, 0, 0))
    f32 = lambda *shape: pltpu.VMEM(shape, F32)
    mxu = lambda *shape: pltpu.VMEM(shape, MXU_DTYPE)
    return pl.pallas_call(
        functools.partial(_attn_kernel, tq=tq, tk=tk, rb=16),
        grid=(batch, N_KV, nq),
        in_specs=[pl.BlockSpec((GROUP, tq, HEAD_DIM), lambda b, g, i: (q_base // GROUP + g, b * nq + i, 0)),
                  cmp_spec, cmp_spec, ext_spec, ext_spec,
                  pl.BlockSpec((None, None) + kwp.shape[2:], lambda b, g, i: (g, b, 0, 0)),
                  pl.BlockSpec((None, None) + vwx.shape[2:], lambda b, g, i: (g, b, 0, 0)),
                  pl.BlockSpec((None, tq, LANES), lambda b, g, i: (g, b * nq + i, 0)),
                  pl.BlockSpec((ns, ncp), lambda b, g, i: (0, 0))],
        out_specs=pl.BlockSpec((tq, GROUP * HEAD_DIM), lambda b, g, i: (b * nq + i, g)),
        out_shape=jax.ShapeDtypeStruct((batch * seq, N_HEADS * HEAD_DIM), MXU_DTYPE),
        scratch_shapes=[mxu(rows, 2 * LANES), mxu(rows, 2 * LANES),
                        f32(rows, ncp), mxu(rows, ncp),
                        f32(rows, tq), mxu(rows, tq),
                        f32(rows, wlen), mxu(rows, wlen),
                        f32(rows, tk), f32(rows, tk),
                        mxu(rows, tk), mxu(rows, tk),
                        f32(tq, ncp), f32(tq, tq), f32(tq, 2 * tq),
                        f32(rows, LANES), f32(rows, LANES), f32(rows, LANES),
                        f32(rows, LANES),
                        f32(rows, LANES),
                        f32(rows, 2 * LANES), f32(rows, 2 * LANES),
                        f32(tq, ncp),
                        f32(ns, tq), f32(ns, tq)],
        compiler_params=_params("parallel", "parallel", "arbitrary"),
        name="nsa_attention",
    )(qkv, kc, vc, ksx, vsx, kwp, vwx, gates, mselt)


def _conv_kernel(h_ref, wx_ref, wb_ref, wc_ref, cw_ref, o_ref, ubuf, *, tiles_per_seq):
    i = pl.program_id(1)
    tm = h_ref.shape[0]
    h = h_ref[...]

    @pl.when(i % tiles_per_seq == 0)
    def _():
        ubuf[0:8, :] = jnp.zeros((8, ubuf.shape[1]), F32)

    w = cw_ref[...]
    for cs in _column_parts(o_ref.shape[1]):
        x_in = _dot(h, wx_ref[:, cs])
        gate_b = _dot(h, wb_ref[:, cs])
        gate_c = _dot(h, wc_ref[:, cs])
        ubuf[8:tm + 8, cs] = gate_c * x_in
        conv = (w[2:3, cs] * ubuf[8:tm + 8, cs] + w[1:2, cs] * ubuf[7:tm + 7, cs]
                + w[0:1, cs] * ubuf[6:tm + 6, cs])
        o_ref[:, cs] = (gate_b * conv).astype(o_ref.dtype)
        ubuf[0:8, cs] = ubuf[tm:tm + 8, cs]


def _conv_mixer(h, wx, wb, wc, cw, seq, tm, tn):
    m, d = h.shape
    n = wx.shape[1]
    wspec = pl.BlockSpec((d, tn), lambda j, i: (0, j))
    return pl.pallas_call(
        functools.partial(_conv_kernel, tiles_per_seq=seq // tm),
        grid=(n // tn, m // tm),
        in_specs=[pl.BlockSpec((tm, d), lambda j, i: (i, 0)), wspec, wspec, wspec,
                  pl.BlockSpec((8, tn), lambda j, i: (0, j))],
        out_specs=pl.BlockSpec((tm, tn), lambda j, i: (i, j)),
        out_shape=jax.ShapeDtypeStruct((m, n), MXU_DTYPE),
        scratch_shapes=[pltpu.VMEM((tm + 8, tn), F32)],
        compiler_params=_params("parallel", "arbitrary"),
        name="conv_mixer",
    )(h, wx, wb, wc, cw)


def _merge_kernel(oa_ref, v_ref, h_ref, wap_ref, wco_ref, wga_ref, wgc_ref, o_ref):
    h = h_ref[...]
    oa = oa_ref[...]
    v = v_ref[...]
    for cs in _column_parts(o_ref.shape[1]):
        y_attn = _dot(oa, wap_ref[:, cs])
        y_conv = _dot(v, wco_ref[:, cs])
        g_attn = jax.nn.sigmoid(_dot(h, wga_ref[:, cs]))
        g_conv = jax.nn.sigmoid(_dot(h, wgc_ref[:, cs]))
        o_ref[:, cs] = (g_attn * y_attn + g_conv * y_conv).astype(o_ref.dtype)


def _merge(oa, v, h, wap, wco, wga, wgc, tm, tn):
    m, d = h.shape
    n = wap.shape[1]
    aspec = pl.BlockSpec((tm, d), lambda i, j: (i, 0))
    wspec = pl.BlockSpec((d, tn), lambda i, j: (0, j))
    return pl.pallas_call(
        _merge_kernel,
        grid=(m // tm, n // tn),
        in_specs=[aspec, aspec, aspec, wspec, wspec, wspec, wspec],
        out_specs=pl.BlockSpec((tm, tn), lambda i, j: (i, j)),
        out_shape=jax.ShapeDtypeStruct((m, n), MXU_DTYPE),
        compiler_params=_params("parallel", "arbitrary"),
        name="gated_merge",
    )(oa, v, h, wap, wco, wga, wgc)


def _outproj_kernel(a_ref, w_ref, x_ref, g_ref, xo_ref, ho_ref):
    x = x_ref[...] + _dot(a_ref[...], w_ref[...])
    xo_ref[...] = x
    ho_ref[...] = _rms(x, g_ref[...]).astype(ho_ref.dtype)


def _outproj(a, w, x, g, tm):
    m, d = x.shape
    row = pl.BlockSpec((tm, d), lambda i: (i, 0))
    return pl.pallas_call(
        _outproj_kernel,
        grid=(m // tm,),
        in_specs=[row, pl.BlockSpec((d, d), lambda i: (0, 0)), row,
                  pl.BlockSpec((1, d), lambda i: (0, 0))],
        out_specs=[row, row],
        out_shape=[jax.ShapeDtypeStruct((m, d), F32), jax.ShapeDtypeStruct((m, d), MXU_DTYPE)],
        compiler_params=_params("parallel"),
        name="out_proj",
    )(a, w, x, g.reshape(1, d))


def _mlp_kernel(h_ref, wu_ref, wd_ref, x_ref, g_ref, *refs, last):
    acc = refs[-1]
    f = pl.program_id(1)

    @pl.when(f == 0)
    def _():
        acc[...] = x_ref[...]

    a = jnp.maximum(_dot(h_ref[...], wu_ref[...]), 0.0)
    acc[...] += _dot((a * a).astype(MXU_DTYPE), wd_ref[...])

    @pl.when(f == pl.num_programs(1) - 1)
    def _():
        x = acc[...]
        normed = _rms(x, g_ref[...])
        if last:
            refs[0][...] = normed
        else:
            refs[0][...] = x
            refs[1][...] = normed.astype(refs[1].dtype)


def _mlp(h, wu, wd, x, g, tm, tf, last):
    m, d = x.shape
    ff = wu.shape[1]
    row = pl.BlockSpec((tm, d), lambda i, f: (i, 0))
    if last:
        out_specs, out_shape = [row], [jax.ShapeDtypeStruct((m, d), F32)]
    else:
        out_specs = [row, row]
        out_shape = [jax.ShapeDtypeStruct((m, d), F32), jax.ShapeDtypeStruct((m, d), MXU_DTYPE)]
    return pl.pallas_call(
        functools.partial(_mlp_kernel, last=last),
        grid=(m // tm, ff // tf),
        in_specs=[row, pl.BlockSpec((d, tf), lambda i, f: (0, f)),
                  pl.BlockSpec((tf, d), lambda i, f: (f, 0)), row,
                  pl.BlockSpec((1, d), lambda i, f: (0, 0))],
        out_specs=out_specs,
        out_shape=out_shape,
        scratch_shapes=[pltpu.VMEM((tm, d), F32)],
        compiler_params=_params("parallel", "arbitrary"),
        name="relu2_mlp",
    )(h, wu, wd, x, g.reshape(1, d))


def _rope_tables(pos):
    half = HEAD_DIM // 2
    inv_freq = jnp.exp(-math.log(ROPE_THETA) * jnp.arange(half, dtype=F32) / half)
    ang = pos.astype(F32)[:, None] * inv_freq[None, :]
    cos, sin = jnp.cos(ang), jnp.sin(ang)
    return jnp.concatenate([cos, cos], axis=-1), jnp.concatenate([-sin, sin], axis=-1)


def _cmp_to_sel_t(ncp, ns):
    nc = ncp - 1
    cs = np.arange(nc) * CMP_STRIDE
    ss = np.arange(ns) * SEL_LEN
    ov = np.minimum(cs[:, None] + CMP_LEN, ss[None, :] + SEL_LEN) - np.maximum(cs[:, None], ss[None, :])
    m = np.zeros((ncp, ns), np.float32)
    m[:nc] = np.clip(ov, 0, None) / CMP_LEN
    return jnp.asarray(m.T, dtype=MXU_DTYPE)


def _block_onehot(seq):
    e = (np.arange(seq)[:, None] // SEL_LEN == np.arange(LANES)[None, :]).astype(np.float32)
    return jnp.asarray(e, dtype=MXU_DTYPE)


def kernel(x, norm1_g, w_in, cmp_pos_k, cmp_w1_k, cmp_w2_k, cmp_pos_v, cmp_w1_v, cmp_w2_v,
           conv_w, w_attn_proj, w_conv_out, w_o, norm2_g, w_up, w_down, final_g):
    batch, seq, d = x.shape
    depth = w_in.shape[0]
    m = batch * seq
    attn_dim = N_HEADS * HEAD_DIM
    kv_dim = N_KV * HEAD_DIM
    n_gate = N_HEADS * N_NSA_BRANCH
    ncp = seq // CMP_STRIDE
    ns = seq // SEL_LEN
    cast = lambda a: a.astype(MXU_DTYPE)

    tm = min(1024, seq)
    tm_small = min(512, seq)
    tq = 256
    tk = 1024

    cos_t, sin_t = _rope_tables(jnp.arange(seq))
    cos_c, sin_c = _rope_tables(jnp.arange(ncp) * CMP_STRIDE + CMP_LEN - 1)
    mselt = _cmp_to_sel_t(ncp, ns)
    onehot = jnp.broadcast_to(jnp.tile(_block_onehot(seq), (batch, 1))[None], (N_KV, m, LANES))
    zeros = jnp.zeros((N_KV, m, LANES), MXU_DTYPE)
    ksx0 = jnp.concatenate([zeros, onehot], axis=-1)
    vsx0 = jnp.concatenate([zeros, jnp.ones_like(zeros)], axis=-1)
    before_start = jnp.asarray(np.arange(2 * LANES) == LANES, dtype=MXU_DTYPE)
    front = lambda a, rows: jnp.concatenate(
        [jnp.broadcast_to(rows, (N_KV, batch, tm, 2 * LANES)),
         a.reshape(N_KV, batch, seq, 2 * LANES)], axis=2)
    kwx0 = front(jnp.zeros_like(vsx0), before_start)
    vwx0 = front(vsx0, jnp.zeros((), MXU_DTYPE))

    q_base = 0

    x2 = x.reshape(m, d)
    h = _rmsnorm(x2, norm1_g[0], tm)
    out = None
    for l in range(depth):
        wl = w_in[l]
        o_kv = attn_dim
        kv = [wl[:, o_kv + j * kv_dim:o_kv + (j + 1) * kv_dim] for j in range(6)]
        w_qkv = cast(jnp.concatenate([wl[:, :attn_dim], kv[2], kv[4], kv[0], kv[1], kv[3], kv[5]], axis=1))
        o_ng = o_kv + 6 * kv_dim
        w_ng = wl[:, o_ng:o_ng + n_gate].reshape(d, N_KV, GROUP * N_NSA_BRANCH)
        w_ng = cast(jnp.pad(w_ng, ((0, 0), (0, 0), (0, LANES - GROUP * N_NSA_BRANCH))).reshape(d, N_KV * LANES))
        o_cv = o_ng + n_gate
        w_x, w_b, w_c = [cast(wl[:, o_cv + j * d:o_cv + (j + 1) * d]) for j in range(3)]
        o_mg = o_cv + 3 * d
        w_ga, w_gc = [cast(wl[:, o_mg + j * d:o_mg + (j + 1) * d]) for j in range(2)]

        qkv, cmp_kv, ksx, vsx, kwx, vwx = _qkv_proj(h, w_qkv, cos_t, sin_t, ksx0, vsx0, kwx0, vwx0, seq, tm)
        ksx0, vsx0, kwx0, vwx0 = ksx, vsx, kwx, vwx
        gates = _gate_proj(h, w_ng, tm)

        chunks = cmp_kv.reshape(2 * N_KV, batch, ncp, CMP_STRIDE * HEAD_DIM)
        pad_pos = lambda p: cast(jnp.pad(p.reshape(1, CMP_LEN * HEAD_DIM), ((0, 7), (0, 0))))
        kc, vc = _compress(chunks, 0, N_KV, cast(cmp_w1_k[l]), cast(cmp_w2_k[l]), pad_pos(cmp_pos_k[l]),
                           cast(cmp_w1_v[l]), cast(cmp_w2_v[l]), pad_pos(cmp_pos_v[l]), cos_c, sin_c, batch)

        o_attn = _attention(qkv, ksx, vsx, kwx, vwx, kc, vc, gates, mselt, batch, seq, q_base, tq, tk)

        cw = jnp.pad(conv_w[l], ((0, 8 - CONV_WIDTH), (0, 0)))
        v_conv = _conv_mixer(h, w_x, w_b, w_c, cw, seq, tm_small, 512)

        merged = _merge(o_attn, v_conv, h, cast(w_attn_proj[l]), cast(w_conv_out[l]), w_ga, w_gc,
                        tm_small, 512)
        x2, h2 = _outproj(merged, cast(w_o[l]), x2, norm2_g[l], tm_small)

        last = l == depth - 1
        g_next = final_g if last else norm1_g[l + 1]
        res = _mlp(h2, cast(w_up[l]), cast(w_down[l]), x2, g_next, tm_small, 1024, last)
        if last:
            out = res[0]
        else:
            x2, h = res
    return out.reshape(batch, seq, d)
```

```python
import collections
import functools
import math

import numpy as np
import jax
import jax.numpy as jnp
from jax import lax
from jax.experimental import pallas as pl
from jax.experimental.pallas import tpu as pltpu

N_HEADS = 16
HEAD_DIM = 128
N_KV = 4
GROUP = N_HEADS // N_KV
CMP_LEN = 32
CMP_STRIDE = 16
SEL_LEN = 64
N_SEL = 16
WINDOW = 512
N_NSA_BRANCH = 3
CONV_WIDTH = 3
ROPE_THETA = 10000.0
EPS = 1e-6

MXU_DTYPE = jnp.bfloat16
F32 = jnp.float32
NEG = -1e30
VMEM_LIMIT = 56 * 1024 * 1024
LANES = 128
MXU_WIDTH = 256
PACKED_ROWS = 16
NT_DIMS = (((1,), (1,)), ((), ()))


def _params(*sem):
    return pltpu.CompilerParams(dimension_semantics=sem, vmem_limit_bytes=VMEM_LIMIT)


def _dot(a, b):
    return jnp.dot(a, b, preferred_element_type=F32)


def _rms(x, g):
    return x * lax.rsqrt(jnp.mean(x * x, axis=-1, keepdims=True) + EPS) * g


def _column_parts(n):
    return [slice(c, c + MXU_WIDTH) for c in range(0, n, MXU_WIDTH)]


def _norm_kernel(x_ref, g_ref, o_ref):
    o_ref[...] = _rms(x_ref[...], g_ref[...]).astype(o_ref.dtype)


def _rmsnorm(x2d, g, tm):
    m, d = x2d.shape
    return pl.pallas_call(
        _norm_kernel,
        grid=(m // tm,),
        in_specs=[pl.BlockSpec((tm, d), lambda i: (i, 0)), pl.BlockSpec((1, d), lambda i: (0, 0))],
        out_specs=pl.BlockSpec((tm, d), lambda i: (i, 0)),
        out_shape=jax.ShapeDtypeStruct((m, d), MXU_DTYPE),
        compiler_params=_params("parallel"),
        name="rmsnorm",
    )(x2d, g.reshape(1, d))


QKV_TILES = ("q", "q", "q", "q", "k_sel", "k_win", "k_cmp", "v_cmp", "v_sel", "v_win")


def _qkv_kernel(h_ref, w_ref, cos_ref, sin_ref, ksx_in, vsx_in, kwx_in, vwx_in,
                o_ref, cmp_ref, ksx_ref, vsx_ref, kwx_ref, vwx_ref, stage, *, q_scale):
    del ksx_in, vsx_in, kwx_in, vwx_in
    j = pl.program_id(1)
    dest = {"q": o_ref, "k_cmp": cmp_ref, "v_cmp": cmp_ref, "k_sel": ksx_ref, "v_sel": vsx_ref,
            "k_win": kwx_ref, "v_win": vwx_ref}

    def put(ref, rope, scale, chunked):
        h = h_ref[...]
        for cs in _column_parts(w_ref.shape[1]):
            acc = _dot(h, w_ref[:, cs])
            for c in range(MXU_WIDTH // LANES):
                xc = acc[:, c * LANES:(c + 1) * LANES]
                head = cs.start // LANES + c
                if rope:
                    xc = (xc * (cos_ref[...] * scale)
                          + pltpu.roll(xc, HEAD_DIM // 2, 1) * (sin_ref[...] * scale))
                if chunked:
                    stage[...] = xc
                    n_rows = stage.shape[0] // CMP_STRIDE
                    for l in range(CMP_STRIDE):
                        ref[head, :, l * LANES:(l + 1) * LANES] = (
                            stage[pl.ds(l, n_rows, stride=CMP_STRIDE), :].astype(ref.dtype))
                else:
                    ref[head] = xc.astype(ref.dtype)

    for name in dict.fromkeys(QKV_TILES):
        tiles = [t for t, n in enumerate(QKV_TILES) if n == name]

        @pl.when((j >= tiles[0]) & (j <= tiles[-1]))
        def _():
            put(dest[name], rope=name in ("q", "k_sel", "k_win"), scale=q_scale if name == "q" else 1.0,
                chunked=name in ("k_cmp", "v_cmp"))


def _qkv_proj(h, w, cos, sin, ksx, vsx, kwx, vwx, seq, tm):
    m, d = h.shape
    tn = N_KV * LANES
    assert w.shape[1] == tn * len(QKV_TILES) and seq % tm == 0 and kwx.shape[2] == seq + tm
    tps = seq // tm
    n_q = QKV_TILES.count("q")
    cmp_first = QKV_TILES.index("k_cmp")
    assert QKV_TILES[cmp_first + 1] == "v_cmp" and tm % (PACKED_ROWS * CMP_STRIDE) == 0
    ext_spec = pl.BlockSpec((N_KV, tm, LANES), lambda i, j: (0, i, 0))
    win_spec = pl.BlockSpec((N_KV, None, tm, LANES), lambda i, j: (0, i // tps, 1 + i % tps, 0))
    any_spec = pl.BlockSpec(memory_space=pl.ANY)
    sds = lambda a: jax.ShapeDtypeStruct(a.shape, a.dtype)
    kern = functools.partial(_qkv_kernel, q_scale=HEAD_DIM ** -0.5 * math.log2(math.e))
    return pl.pallas_call(
        kern,
        grid=(m // tm, len(QKV_TILES)),
        in_specs=[pl.BlockSpec((tm, d), lambda i, j: (i, 0)),
                  pl.BlockSpec((d, tn), lambda i, j: (0, j)),
                  pl.BlockSpec((tm, LANES), lambda i, j: (i % tps, 0)),
                  pl.BlockSpec((tm, LANES), lambda i, j: (i % tps, 0)),
                  any_spec, any_spec, any_spec, any_spec],
        out_specs=[pl.BlockSpec((N_KV, tm, LANES), lambda i, j: (jnp.minimum(j, n_q - 1), i, 0)),
                   pl.BlockSpec((N_KV, tm // CMP_STRIDE, CMP_STRIDE * LANES),
                                lambda i, j: (jnp.clip(j - cmp_first, 0, 1), i, 0)),
                   ext_spec, ext_spec, win_spec, win_spec],
        out_shape=[jax.ShapeDtypeStruct((n_q * N_KV, m, LANES), MXU_DTYPE),
                   jax.ShapeDtypeStruct((2 * N_KV, m // CMP_STRIDE, CMP_STRIDE * LANES), MXU_DTYPE),
                   sds(ksx), sds(vsx), sds(kwx), sds(vwx)],
        scratch_shapes=[pltpu.VMEM((tm, LANES), F32)],
        input_output_aliases={4: 2, 5: 3, 6: 4, 7: 5},
        compiler_params=_params("parallel", "arbitrary"),
        name="qkv_proj",
    )(h, w, cos, sin, ksx, vsx, kwx, vwx)


def _gate_kernel(h_ref, w_ref, o_ref):
    acc = _dot(h_ref[...], w_ref[...])
    for c in range(o_ref.shape[0]):
        o_ref[c] = jax.nn.sigmoid(acc[:, c * LANES:(c + 1) * LANES])


def _gate_proj(h, w, tm):
    m, d = h.shape
    n = w.shape[1]
    return pl.pallas_call(
        _gate_kernel,
        grid=(m // tm,),
        in_specs=[pl.BlockSpec((tm, d), lambda i: (i, 0)), pl.BlockSpec((d, n), lambda i: (0, 0))],
        out_specs=pl.BlockSpec((n // LANES, tm, LANES), lambda i: (0, i, 0)),
        out_shape=jax.ShapeDtypeStruct((n // LANES, m, LANES), F32),
        compiler_params=_params("parallel"),
        name="nsa_gate_proj",
    )(h, w)


def _compress_kernel(ck_ref, cv_ref, w1k_ref, w2k_ref, pk_ref, w1v_ref, w2v_ref, pv_ref,
                     cos_ref, sin_ref, kc_ref, vc_ref):
    nch = ck_ref.shape[0]
    half = w1k_ref.shape[0] // 2

    def phi(c_ref, w1_ref, w2_ref, p_ref):
        c = c_ref[...]
        first = _dot(c, w1_ref[:half, :])
        second = _dot(c, w1_ref[half:, :])
        pos = _dot(p_ref[...], w1_ref[...])[0:1, :]
        hid = first + pltpu.roll(second, nch - 1, 0) + pos
        act = hid * jax.nn.sigmoid(hid)
        return _dot(act.astype(MXU_DTYPE), w2_ref[...])

    kc = phi(ck_ref, w1k_ref, w2k_ref, pk_ref)
    kc = kc * cos_ref[...] + pltpu.roll(kc, HEAD_DIM // 2, 1) * sin_ref[...]
    kc_ref[...] = kc.astype(kc_ref.dtype)
    vc_ref[...] = phi(cv_ref, w1v_ref, w2v_ref, pv_ref).astype(vc_ref.dtype)


def _compress(chunks, kbase, vbase, w1k, w2k, pk, w1v, w2v, pv, cos_c, sin_c, batch):
    _, _, nch, cw = chunks.shape
    const = lambda a: pl.BlockSpec(a.shape, lambda b, g: (0,) * a.ndim)
    out_spec = pl.BlockSpec((None, None, nch, HEAD_DIM), lambda b, g: (b, g, 0, 0))
    out_sds = jax.ShapeDtypeStruct((batch, N_KV, nch, HEAD_DIM), MXU_DTYPE)
    return pl.pallas_call(
        _compress_kernel,
        grid=(batch, N_KV),
        in_specs=[pl.BlockSpec((None, None, nch, cw), lambda b, g: (kbase + g, b, 0, 0)),
                  pl.BlockSpec((None, None, nch, cw), lambda b, g: (vbase + g, b, 0, 0)),
                  const(w1k), const(w2k), const(pk), const(w1v), const(w2v), const(pv),
                  const(cos_c), const(sin_c)],
        out_specs=[out_spec, out_spec],
        out_shape=[out_sds, out_sds],
        compiler_params=_params("parallel", "parallel"),
        name="compress",
    )(chunks, chunks, w1k, w2k, pk, w1v, w2v, pv, cos_c, sin_c)


_Branch = collections.namedtuple("_Branch", "s p m acc bias")


def _attn_kernel(q_ref, kc_ref, vc_ref, ksx_ref, vsx_ref, kw_ref, vwx_ref, gate_ref, mselt_ref,
                 o_ref, qx_scr, qw_scr, sc_scr, pc_scr, sd_scr, pd_scr, sw_scr, pw_scr, sa_scr, sb_scr,
                 pa_scr, pb_scr, bc_scr, bd_scr, bw_scr, mc_scr, mw_scr, m_scr, a_scr,
                 accc_scr, accw_scr, acc_scr, psum_scr, imp_scr, rank_scr, *, tq, tk, rb):
    i = pl.program_id(2)
    q0 = pl.multiple_of(i * tq, tq)
    r = GROUP
    rows = r * tq
    ncp = kc_ref.shape[0]
    ns = mselt_ref.shape[0]
    wlen = WINDOW + tq
    n_chunks = rows // rb
    t_col = q0 + lax.broadcasted_iota(jnp.int32, (tq, 1), 0)
    q = q_ref[...].reshape(rows, HEAD_DIM)
    gates = gate_ref[...]
    head = lambda h: slice(h * tq, (h + 1) * tq)
    gate = lambda h, br: gates[:, h * N_NSA_BRANCH + br:h * N_NSA_BRANCH + br + 1]
    half = rows // 2
    halves = [slice(0, half), slice(half, rows)]
    lane = lambda kk: slice(kk * LANES, (kk + 1) * LANES)

    def load_scores(br, rs, bs, width):
        xs = [br.s[rs, lane(kk)] for kk in range(width // LANES)]
        return [x if b is None else x + b[bs, :] for x, b in zip(xs, br.bias)]

    def softmax_rows(br, width, chunks, first, cmp):
        def slices(c):
            r0 = c * rb
            return slice(r0, r0 + rb), slice(r0 % tq, r0 % tq + rb)

        for c in chunks:
            rs, bs = slices(c)
            xs = load_scores(br, rs, bs, width)
            mx = jnp.max(functools.reduce(jnp.maximum, xs), axis=-1, keepdims=True)
            if first:
                br.m[rs, :] = jnp.broadcast_to(mx, (rb, LANES))
            else:
                m_old = br.m[rs, :]
                m_new = jnp.maximum(m_old, mx)
                a_scr[rs, :] = jnp.exp2(m_old - m_new)
                br.m[rs, :] = m_new

        for c in chunks:
            rs, bs = slices(c)
            xs = load_scores(br, rs, bs, width)
            m = br.m[rs, :]
            ps = [jnp.exp2(x - m) for x in xs]
            if cmp:
                ps = [jnp.where(x > 0.5 * NEG, p, 0.0) for x, p in zip(xs, ps)]
                lsum = jnp.sum(functools.reduce(jnp.add, ps), axis=-1, keepdims=True)
                inv = 1.0 / jnp.where(lsum > 0.0, lsum, 1.0)
                ps = [p * inv for p in ps]
                for kk, p in enumerate(ps):
                    psum_scr[bs, lane(kk)] += p
            for kk, p in enumerate(ps):
                br.p[rs, lane(kk)] = p.astype(br.p.dtype)

    def scores(br, width, q_rows, k):
        for hs in halves:
            br.s[hs, :width] = lax.dot_general(q_rows(hs), k, NT_DIMS, preferred_element_type=F32)

    def weighted_values(br, width, v, *, first, cmp=False):
        for hi, hs in enumerate(halves):
            chunks = range(hi * n_chunks // 2, (hi + 1) * n_chunks // 2)
            softmax_rows(br, width, chunks, first, cmp)
            pv = _dot(br.p[hs, :width], v)
            cols = slice(0, v.shape[1])
            if first:
                br.acc[hs, cols] = pv
            else:
                a = a_scr[hs, :]
                for kk in range(v.shape[1] // LANES):
                    br.acc[hs, lane(kk)] = a * br.acc[hs, lane(kk)] + pv[:, lane(kk)]

    q_plain = lambda hs: q[hs]
    q_ext = lambda hs: qx_scr[hs, :]
    row_id = lax.broadcasted_iota(jnp.int32, (tq, LANES), 0)
    col_id = lax.broadcasted_iota(jnp.int32, (tq, LANES), 1)

    n_cl = ncp // LANES
    for kk in range(n_cl):
        cmp_end = (col_id + kk * LANES) * CMP_STRIDE + (CMP_LEN - 1)
        bc_scr[:, lane(kk)] = jnp.where(cmp_end <= t_col, 0.0, NEG)
    n_dl = tq // LANES
    for kk in range(n_dl):
        own = (col_id <= row_id - kk * LANES) & (row_id < (kk + 1) * LANES)
        bd_scr[:, lane(kk)] = jnp.where(own, 0.0, NEG)
    n_wl = wlen // LANES
    for kk in range(n_dl):
        bw_scr[:, lane(kk)] = jnp.where(col_id + kk * LANES > row_id, 0.0, NEG)
        bw_scr[:, lane(n_dl + kk)] = jnp.where(col_id + kk * LANES <= row_id, 0.0, NEG)

    cmp_br = _Branch(sc_scr, pc_scr, mc_scr, accc_scr, [bc_scr.at[:, lane(kk)] for kk in range(n_cl)])
    diag_br = _Branch(sd_scr, pd_scr, m_scr, acc_scr, [bd_scr.at[:, lane(kk)] for kk in range(n_dl)])
    win_bias = ([bw_scr.at[:, lane(kk)] for kk in range(n_dl)] + [None] * (n_wl - 2 * n_dl)
                + [bw_scr.at[:, lane(n_dl + kk)] for kk in range(n_dl)])
    win_br = _Branch(sw_scr, pw_scr, mw_scr, accw_scr, win_bias)
    sel_a = _Branch(sa_scr, pa_scr, m_scr, acc_scr, [None] * (tk // LANES))
    sel_b = _Branch(sb_scr, pb_scr, m_scr, acc_scr, [None] * (tk // LANES))

    psum_scr[...] = jnp.zeros(psum_scr.shape, F32)
    pad_col = jnp.where(lax.broadcasted_iota(jnp.int32, (rows, LANES), 1) == 0, NEG, 0.0)
    qw_scr[:, :LANES] = q
    qw_scr[:, LANES:] = pad_col.astype(qw_scr.dtype)
    scores(cmp_br, ncp, q_plain, kc_ref[...])
    w_rows = pl.ds(q0 + (kw_ref.shape[0] - ksx_ref.shape[0] - WINDOW), wlen)
    scores(win_br, wlen, lambda hs: qw_scr[hs, :], kw_ref[w_rows, :])

    weighted_values(cmp_br, ncp, vc_ref[...], first=True, cmp=True)
    scores(diag_br, tq, q_plain, ksx_ref[pl.ds(q0, tq), :LANES])
    mselt = mselt_ref[...]
    imp = jnp.zeros((ns, tq), F32)
    rem = psum_scr[...]
    for _ in range(3):
        piece = rem.astype(MXU_DTYPE)
        imp = imp + lax.dot_general(mselt, piece, NT_DIMS, preferred_element_type=F32)
        rem = rem - piece.astype(F32)
    blk = lax.broadcasted_iota(jnp.int32, (ns, tq), 0)
    tb = (q0 + lax.broadcasted_iota(jnp.int32, (ns, tq), 1)) // SEL_LEN
    forced = (blk == 0) | (blk == tb) | (blk == tb - 1)
    imp_scr[...] = jnp.where(blk <= tb, jnp.where(forced, jnp.inf, imp), -jnp.inf)
    rank_scr[...] = jnp.zeros(rank_scr.shape, F32)

    weighted_values(win_br, wlen, vwx_ref[w_rows, :], first=True)
    weighted_values(diag_br, tq, vsx_ref[pl.ds(q0, tq), :], first=True)

    sub = 8
    n_grp = ns // sub
    sub_id = lax.broadcasted_iota(jnp.int32, (sub, tq), 0)
    grp = lambda v: slice(sub * v, sub * (v + 1))
    for gm in range(n_grp):
        @pl.when(sub * gm * SEL_LEN <= q0 + tq - 1)
        def _():
            xs = [imp_scr[grp(v), :] for v in range(n_grp)]
            ranks = [rank_scr[grp(v), :] for v in range(n_grp)]
            for mp in range(sub * gm, sub * (gm + 1)):
                row = jnp.broadcast_to(xs[gm][mp % sub:mp % sub + 1, :], (sub, tq))
                for v, x in enumerate(xs):
                    if sub * v > mp:
                        ahead = row >= x
                    elif sub * v + sub - 1 <= mp:
                        ahead = row > x
                    else:
                        ahead = (row > x) | ((row == x) & (sub_id > mp % sub))
                    ranks[v] = ranks[v] + jnp.where(ahead, 1.0, 0.0)
            for v in range(n_grp):
                rank_scr[grp(v), :] = ranks[v]

    own_start = (q0 + lax.broadcasted_iota(jnp.int32, (ns, tq), 1)) // LANES * (LANES // SEL_LEN)
    off = jnp.where((rank_scr[...] < float(min(N_SEL, ns))) & (blk < own_start), 0.0, NEG)
    off = off.T.astype(MXU_DTYPE)
    if ns < LANES:
        off = jnp.concatenate([off, jnp.zeros((tq, LANES - ns), MXU_DTYPE)], axis=1)
    for h in range(r):
        qx_scr[head(h), :LANES] = q_ref[h]
        qx_scr[head(h), LANES:] = off

    last_tile = ksx_ref.shape[0] // tk - 1

    def tile_rows(t):
        return pl.ds(pl.multiple_of(jnp.minimum(t, last_tile) * tk, tk), tk)

    def sel_pair(jj, carry):
        t = 2 * jj
        scores(sel_b, tk, q_ext, ksx_ref[tile_rows(t + 1), :])
        weighted_values(sel_a, tk, vsx_ref[tile_rows(t), :], first=False)
        scores(sel_a, tk, q_ext, ksx_ref[tile_rows(t + 2), :])
        weighted_values(sel_b, tk, vsx_ref[tile_rows(t + 1), :], first=False)
        return carry

    n_tiles = (q0 + tq - LANES + tk - 1) // tk
    scores(sel_a, tk, q_ext, ksx_ref[tile_rows(0), :])
    lax.fori_loop(0, n_tiles // 2, sel_pair, 0)

    @pl.when(n_tiles % 2 == 1)
    def _():
        weighted_values(sel_a, tk, vsx_ref[tile_rows(n_tiles - 1), :], first=False)

    o_sel = acc_scr[:, :LANES] * (1.0 / acc_scr[:, LANES:])
    o_win = accw_scr[:, :LANES] * (1.0 / accw_scr[:, LANES:])
    for h in range(r):
        o = (gate(h, 0) * accc_scr[head(h), :] + gate(h, 1) * o_sel[head(h), :]
             + gate(h, 2) * o_win[head(h), :])
        o_ref[:, h * HEAD_DIM:(h + 1) * HEAD_DIM] = o.astype(o_ref.dtype)


def _attention(q, ksx, vsx, kwp, vwx, kc, vc, gates, mselt, batch, seq, tq, tk):
    nq = seq // tq
    ncp = kc.shape[2]
    ns = mselt.shape[0]
    assert ns <= LANES and seq % (2 * tk) == 0 and tk % tq == 0 and WINDOW % tq == 0
    assert tq % LANES == 0 and ncp % LANES == 0 and kwp.shape[2] >= seq + WINDOW
    rows = GROUP * tq
    wlen = WINDOW + tq
    ext_spec = pl.BlockSpec((None, seq, 2 * LANES), lambda b, g, i: (g, b, 0))
    cmp_spec = pl.BlockSpec((None, None, ncp, HEAD_DIM), lambda b, g, i: (b, g, 0, 0))
    f32 = lambda *shape: pltpu.VMEM(shape, F32)
    mxu = lambda *shape: pltpu.VMEM(shape, MXU_DTYPE)
    return pl.pallas_call(
        functools.partial(_attn_kernel, tq=tq, tk=tk, rb=PACKED_ROWS),
        grid=(batch, N_KV, nq),
        in_specs=[pl.BlockSpec((GROUP, tq, HEAD_DIM), lambda b, g, i: (g, b * nq + i, 0)),
                  cmp_spec, cmp_spec, ext_spec, ext_spec,
                  pl.BlockSpec((None, None) + kwp.shape[2:], lambda b, g, i: (g, b, 0, 0)),
                  pl.BlockSpec((None, None) + vwx.shape[2:], lambda b, g, i: (g, b, 0, 0)),
                  pl.BlockSpec((None, tq, LANES), lambda b, g, i: (g, b * nq + i, 0)),
                  pl.BlockSpec((ns, ncp), lambda b, g, i: (0, 0))],
        out_specs=pl.BlockSpec((tq, GROUP * HEAD_DIM), lambda b, g, i: (b * nq + i, g)),
        out_shape=jax.ShapeDtypeStruct((batch * seq, N_HEADS * HEAD_DIM), MXU_DTYPE),
        scratch_shapes=[mxu(rows, 2 * LANES), mxu(rows, 2 * LANES),
                        f32(rows, ncp), mxu(rows, ncp),
                        f32(rows, tq), mxu(rows, tq),
                        f32(rows, wlen), mxu(rows, wlen),
                        f32(rows, tk), f32(rows, tk),
                        mxu(rows, tk), mxu(rows, tk),
                        f32(tq, ncp), f32(tq, tq), f32(tq, 2 * tq),
                        f32(rows, LANES), f32(rows, LANES), f32(rows, LANES),
                        f32(rows, LANES),
                        f32(rows, LANES),
                        f32(rows, 2 * LANES), f32(rows, 2 * LANES),
                        f32(tq, ncp),
                        f32(ns, tq), f32(ns, tq)],
        compiler_params=_params("parallel", "parallel", "arbitrary"),
        name="nsa_attention",
    )(q, kc, vc, ksx, vsx, kwp, vwx, gates, mselt)


def _conv_kernel(h_ref, wx_ref, wb_ref, wc_ref, cw_ref, o_ref, ubuf, *, tiles_per_seq):
    i = pl.program_id(1)
    tm = h_ref.shape[0]
    h = h_ref[...]

    @pl.when(i % tiles_per_seq == 0)
    def _():
        ubuf[0:8, :] = jnp.zeros((8, ubuf.shape[1]), F32)

    w = cw_ref[...]
    for cs in _column_parts(o_ref.shape[1]):
        x_in = _dot(h, wx_ref[:, cs])
        gate_b = _dot(h, wb_ref[:, cs])
        gate_c = _dot(h, wc_ref[:, cs])
        ubuf[8:tm + 8, cs] = gate_c * x_in
        conv = (w[2:3, cs] * ubuf[8:tm + 8, cs] + w[1:2, cs] * ubuf[7:tm + 7, cs]
                + w[0:1, cs] * ubuf[6:tm + 6, cs])
        o_ref[:, cs] = (gate_b * conv).astype(o_ref.dtype)
        ubuf[0:8, cs] = ubuf[tm:tm + 8, cs]


def _conv_mixer(h, wx, wb, wc, cw, seq, tm, tn):
    m, d = h.shape
    n = wx.shape[1]
    wspec = pl.BlockSpec((d, tn), lambda j, i: (0, j))
    return pl.pallas_call(
        functools.partial(_conv_kernel, tiles_per_seq=seq // tm),
        grid=(n // tn, m // tm),
        in_specs=[pl.BlockSpec((tm, d), lambda j, i: (i, 0)), wspec, wspec, wspec,
                  pl.BlockSpec((8, tn), lambda j, i: (0, j))],
        out_specs=pl.BlockSpec((tm, tn), lambda j, i: (i, j)),
        out_shape=jax.ShapeDtypeStruct((m, n), MXU_DTYPE),
        scratch_shapes=[pltpu.VMEM((tm + 8, tn), F32)],
        compiler_params=_params("parallel", "arbitrary"),
        name="conv_mixer",
    )(h, wx, wb, wc, cw)


def _merge_kernel(oa_ref, v_ref, h_ref, wap_ref, wco_ref, wga_ref, wgc_ref, o_ref):
    h = h_ref[...]
    oa = oa_ref[...]
    v = v_ref[...]
    for cs in _column_parts(o_ref.shape[1]):
        y_attn = _dot(oa, wap_ref[:, cs])
        y_conv = _dot(v, wco_ref[:, cs])
        g_attn = jax.nn.sigmoid(_dot(h, wga_ref[:, cs]))
        g_conv = jax.nn.sigmoid(_dot(h, wgc_ref[:, cs]))
        o_ref[:, cs] = (g_attn * y_attn + g_conv * y_conv).astype(o_ref.dtype)


def _merge(oa, v, h, wap, wco, wga, wgc, tm, tn):
    m, d = h.shape
    n = wap.shape[1]
    aspec = pl.BlockSpec((tm, d), lambda j, i: (i, 0))
    wspec = pl.BlockSpec((d, tn), lambda j, i: (0, j))
    return pl.pallas_call(
        _merge_kernel,
        grid=(n // tn, m // tm),
        in_specs=[aspec, aspec, aspec, wspec, wspec, wspec, wspec],
        out_specs=pl.BlockSpec((tm, tn), lambda j, i: (i, j)),
        out_shape=jax.ShapeDtypeStruct((m, n), MXU_DTYPE),
        compiler_params=_params("parallel", "parallel"),
        name="gated_merge",
    )(oa, v, h, wap, wco, wga, wgc)


def _outproj_kernel(a_ref, w_ref, x_ref, g_ref, xo_ref, ho_ref):
    x = x_ref[...] + _dot(a_ref[...], w_ref[...])
    xo_ref[...] = x
    ho_ref[...] = _rms(x, g_ref[...]).astype(ho_ref.dtype)


def _outproj(a, w, x, g, tm):
    m, d = x.shape
    row = pl.BlockSpec((tm, d), lambda i: (i, 0))
    return pl.pallas_call(
        _outproj_kernel,
        grid=(m // tm,),
        in_specs=[row, pl.BlockSpec((d, d), lambda i: (0, 0)), row,
                  pl.BlockSpec((1, d), lambda i: (0, 0))],
        out_specs=[row, row],
        out_shape=[jax.ShapeDtypeStruct((m, d), F32), jax.ShapeDtypeStruct((m, d), MXU_DTYPE)],
        compiler_params=_params("parallel"),
        name="out_proj",
    )(a, w, x, g.reshape(1, d))


def _mlp_kernel(h_ref, wu_ref, wd_ref, x_ref, g_ref, *refs, last):
    acc = refs[-1]
    f = pl.program_id(1)

    @pl.when(f == 0)
    def _():
        acc[...] = x_ref[...]

    a = jnp.maximum(_dot(h_ref[...], wu_ref[...]), 0.0)
    acc[...] += _dot((a * a).astype(MXU_DTYPE), wd_ref[...])

    @pl.when(f == pl.num_programs(1) - 1)
    def _():
        x = acc[...]
        normed = _rms(x, g_ref[...])
        if last:
            refs[0][...] = normed
        else:
            refs[0][...] = x
            refs[1][...] = normed.astype(refs[1].dtype)


def _mlp(h, wu, wd, x, g, tm, tf, last):
    m, d = x.shape
    ff = wu.shape[1]
    row = pl.BlockSpec((tm, d), lambda i, f: (i, 0))
    if last:
        out_specs, out_shape = [row], [jax.ShapeDtypeStruct((m, d), F32)]
    else:
        out_specs = [row, row]
        out_shape = [jax.ShapeDtypeStruct((m, d), F32), jax.ShapeDtypeStruct((m, d), MXU_DTYPE)]
    return pl.pallas_call(
        functools.partial(_mlp_kernel, last=last),
        grid=(m // tm, ff // tf),
        in_specs=[row, pl.BlockSpec((d, tf), lambda i, f: (0, f)),
                  pl.BlockSpec((tf, d), lambda i, f: (f, 0)), row,
                  pl.BlockSpec((1, d), lambda i, f: (0, 0))],
        out_specs=out_specs,
        out_shape=out_shape,
        scratch_shapes=[pltpu.VMEM((tm, d), F32)],
        compiler_params=_params("parallel", "arbitrary"),
        name="relu2_mlp",
    )(h, wu, wd, x, g.reshape(1, d))


def _rope_tables(pos):
    half = HEAD_DIM // 2
    inv_freq = jnp.exp(-math.log(ROPE_THETA) * jnp.arange(half, dtype=F32) / half)
    ang = pos.astype(F32)[:, None] * inv_freq[None, :]
    cos, sin = jnp.cos(ang), jnp.sin(ang)
    return jnp.concatenate([cos, cos], axis=-1), jnp.concatenate([-sin, sin], axis=-1)


def _cmp_to_sel_t(ncp, ns):
    nc = ncp - 1
    cs = np.arange(nc) * CMP_STRIDE
    ss = np.arange(ns) * SEL_LEN
    ov = np.minimum(cs[:, None] + CMP_LEN, ss[None, :] + SEL_LEN) - np.maximum(cs[:, None], ss[None, :])
    m = np.zeros((ncp, ns), np.float32)
    m[:nc] = np.clip(ov, 0, None) / CMP_LEN
    return jnp.asarray(m.T, dtype=MXU_DTYPE)


def _block_onehot(seq):
    e = (np.arange(seq)[:, None] // SEL_LEN == np.arange(LANES)[None, :]).astype(np.float32)
    return jnp.asarray(e, dtype=MXU_DTYPE)


def kernel(x, norm1_g, w_in, cmp_pos_k, cmp_w1_k, cmp_w2_k, cmp_pos_v, cmp_w1_v, cmp_w2_v,
           conv_w, w_attn_proj, w_conv_out, w_o, norm2_g, w_up, w_down, final_g):
    batch, seq, d = x.shape
    depth = w_in.shape[0]
    m = batch * seq
    attn_dim = N_HEADS * HEAD_DIM
    kv_dim = N_KV * HEAD_DIM
    n_gate = N_HEADS * N_NSA_BRANCH
    ncp = seq // CMP_STRIDE
    ns = seq // SEL_LEN
    cast = lambda a: a.astype(MXU_DTYPE)

    tm = min(1024, seq)
    tm_small = min(512, seq)
    tn_wide = 1024
    tf = 1024
    tq = 256
    tk = 1024

    cos_t, sin_t = _rope_tables(jnp.arange(seq))
    cos_c, sin_c = _rope_tables(jnp.arange(ncp) * CMP_STRIDE + CMP_LEN - 1)
    mselt = _cmp_to_sel_t(ncp, ns)
    onehot = jnp.broadcast_to(jnp.tile(_block_onehot(seq), (batch, 1))[None], (N_KV, m, LANES))
    zeros = jnp.zeros((N_KV, m, LANES), MXU_DTYPE)
    ksx0 = jnp.concatenate([zeros, onehot], axis=-1)
    vsx0 = jnp.concatenate([zeros, jnp.ones_like(zeros)], axis=-1)
    before_start = jnp.asarray(np.arange(2 * LANES) == LANES, dtype=MXU_DTYPE)
    front = lambda a, rows: jnp.concatenate(
        [jnp.broadcast_to(rows, (N_KV, batch, tm, 2 * LANES)),
         a.reshape(N_KV, batch, seq, 2 * LANES)], axis=2)
    kwx0 = front(jnp.zeros_like(vsx0), before_start)
    vwx0 = front(vsx0, jnp.zeros((), MXU_DTYPE))

    x2 = x.reshape(m, d)
    h = _rmsnorm(x2, norm1_g[0], tm)
    out = None
    for l in range(depth):
        wl = w_in[l]
        o_kv = attn_dim
        kv = [wl[:, o_kv + j * kv_dim:o_kv + (j + 1) * kv_dim] for j in range(6)]
        w_qkv = cast(jnp.concatenate([wl[:, :attn_dim], kv[2], kv[4], kv[0], kv[1], kv[3], kv[5]], axis=1))
        o_ng = o_kv + 6 * kv_dim
        w_ng = wl[:, o_ng:o_ng + n_gate].reshape(d, N_KV, GROUP * N_NSA_BRANCH)
        w_ng = cast(jnp.pad(w_ng, ((0, 0), (0, 0), (0, LANES - GROUP * N_NSA_BRANCH))).reshape(d, N_KV * LANES))
        o_cv = o_ng + n_gate
        w_x, w_b, w_c = [cast(wl[:, o_cv + j * d:o_cv + (j + 1) * d]) for j in range(3)]
        o_mg = o_cv + 3 * d
        w_ga, w_gc = [cast(wl[:, o_mg + j * d:o_mg + (j + 1) * d]) for j in range(2)]

        qkv, cmp_kv, ksx, vsx, kwx, vwx = _qkv_proj(h, w_qkv, cos_t, sin_t, ksx0, vsx0, kwx0, vwx0, seq, tm)
        ksx0, vsx0, kwx0, vwx0 = ksx, vsx, kwx, vwx
        gates = _gate_proj(h, w_ng, tm)

        chunks = cmp_kv.reshape(2 * N_KV, batch, ncp, CMP_STRIDE * HEAD_DIM)
        pad_pos = lambda p: cast(jnp.pad(p.reshape(1, CMP_LEN * HEAD_DIM), ((0, 7), (0, 0))))
        kc, vc = _compress(chunks, 0, N_KV, cast(cmp_w1_k[l]), cast(cmp_w2_k[l]), pad_pos(cmp_pos_k[l]),
                           cast(cmp_w1_v[l]), cast(cmp_w2_v[l]), pad_pos(cmp_pos_v[l]), cos_c, sin_c, batch)

        o_attn = _attention(qkv, ksx, vsx, kwx, vwx, kc, vc, gates, mselt, batch, seq, tq, tk)

        cw = jnp.pad(conv_w[l], ((0, 8 - CONV_WIDTH), (0, 0)))
        v_conv = _conv_mixer(h, w_x, w_b, w_c, cw, seq, tm_small, tn_wide)

        merged = _merge(o_attn, v_conv, h, cast(w_attn_proj[l]), cast(w_conv_out[l]), w_ga, w_gc,
                        tm_small, tn_wide)
        x2, h2 = _outproj(merged, cast(w_o[l]), x2, norm2_g[l], tm_small)

        last = l == depth - 1
        g_next = final_g if last else norm1_g[l + 1]
        res = _mlp(h2, cast(w_up[l]), cast(w_down[l]), x2, g_next, tm_small, tf, last)
        if last:
            out = res[0]
        else:
            x2, h = res
    return out.reshape(batch, seq, d)
```

```python
import collections
import functools
import math

import numpy as np
import jax
import jax.numpy as jnp
from jax import lax
from jax.experimental import pallas as pl
from jax.experimental.pallas import tpu as pltpu

N_HEADS = 16
HEAD_DIM = 128
N_KV = 4
GROUP = N_HEADS // N_KV
CMP_LEN = 32
CMP_STRIDE = 16
SEL_LEN = 64
N_SEL = 16
WINDOW = 512
N_NSA_BRANCH = 3
CONV_WIDTH = 3
ROPE_THETA = 10000.0
EPS = 1e-6

MXU_DTYPE = jnp.bfloat16
F32 = jnp.float32
NEG = -1e30
VMEM_LIMIT = 56 * 1024 * 1024
LANES = 128
MXU_WIDTH = 256
PACKED_ROWS = 16
NT_DIMS = (((1,), (1,)), ((), ()))


def _params(*sem):
    return pltpu.CompilerParams(dimension_semantics=sem, vmem_limit_bytes=VMEM_LIMIT)


def _dot(a, b):
    return jnp.dot(a, b, preferred_element_type=F32)


def _rms(x, g):
    return x * lax.rsqrt(jnp.mean(x * x, axis=-1, keepdims=True) + EPS) * g


def _column_parts(n):
    return [slice(c, c + MXU_WIDTH) for c in range(0, n, MXU_WIDTH)]


def _norm_kernel(x_ref, g_ref, o_ref):
    o_ref[...] = _rms(x_ref[...], g_ref[...]).astype(o_ref.dtype)


def _rmsnorm(x2d, g, tm):
    m, d = x2d.shape
    return pl.pallas_call(
        _norm_kernel,
        grid=(m // tm,),
        in_specs=[pl.BlockSpec((tm, d), lambda i: (i, 0)), pl.BlockSpec((1, d), lambda i: (0, 0))],
        out_specs=pl.BlockSpec((tm, d), lambda i: (i, 0)),
        out_shape=jax.ShapeDtypeStruct((m, d), MXU_DTYPE),
        compiler_params=_params("parallel"),
        name="rmsnorm",
    )(x2d, g.reshape(1, d))


QKV_TILES = ("q", "q", "q", "q", "k_sel", "k_win", "k_cmp", "v_cmp", "v_sel", "v_win", "gate")


def _qkv_kernel(h_ref, w_ref, cos_ref, sin_ref, ksx_in, vsx_in, kwx_in, vwx_in,
                o_ref, cmp_ref, ksx_ref, vsx_ref, kwx_ref, vwx_ref, gate_ref, stage, *, q_scale):
    del ksx_in, vsx_in, kwx_in, vwx_in
    j = pl.program_id(1)
    dest = {"q": o_ref, "k_cmp": cmp_ref, "v_cmp": cmp_ref, "k_sel": ksx_ref, "v_sel": vsx_ref,
            "k_win": kwx_ref, "v_win": vwx_ref, "gate": gate_ref}

    def put(ref, rope, scale, chunked, gate):
        h = h_ref[...]
        for cs in _column_parts(w_ref.shape[1]):
            acc = _dot(h, w_ref[:, cs])
            for c in range(MXU_WIDTH // LANES):
                xc = acc[:, c * LANES:(c + 1) * LANES]
                head = cs.start // LANES + c
                if rope:
                    xc = (xc * (cos_ref[...] * scale)
                          + pltpu.roll(xc, HEAD_DIM // 2, 1) * (sin_ref[...] * scale))
                if chunked:
                    stage[...] = xc
                    n_rows = stage.shape[0] // CMP_STRIDE
                    for l in range(CMP_STRIDE):
                        ref[head, :, l * LANES:(l + 1) * LANES] = (
                            stage[pl.ds(l, n_rows, stride=CMP_STRIDE), :].astype(ref.dtype))
                elif gate:
                    ref[head] = jax.nn.sigmoid(xc)
                else:
                    ref[head] = xc.astype(ref.dtype)

    for name in dict.fromkeys(QKV_TILES):
        tiles = [t for t, n in enumerate(QKV_TILES) if n == name]

        @pl.when((j >= tiles[0]) & (j <= tiles[-1]))
        def _():
            put(dest[name], rope=name in ("q", "k_sel", "k_win"), scale=q_scale if name == "q" else 1.0,
                chunked=name in ("k_cmp", "v_cmp"), gate=name == "gate")


def _qkv_proj(h, w, cos, sin, ksx, vsx, kwx, vwx, seq, tm):
    m, d = h.shape
    tn = N_KV * LANES
    assert w.shape[1] == tn * len(QKV_TILES) and seq % tm == 0 and kwx.shape[2] == seq + tm
    tps = seq // tm
    n_q = QKV_TILES.count("q")
    cmp_first = QKV_TILES.index("k_cmp")
    assert QKV_TILES[cmp_first + 1] == "v_cmp" and tm % (PACKED_ROWS * CMP_STRIDE) == 0
    ext_spec = pl.BlockSpec((N_KV, tm, LANES), lambda i, j: (0, i, 0))
    win_spec = pl.BlockSpec((N_KV, None, tm, LANES), lambda i, j: (0, i // tps, 1 + i % tps, 0))
    any_spec = pl.BlockSpec(memory_space=pl.ANY)
    sds = lambda a: jax.ShapeDtypeStruct(a.shape, a.dtype)
    kern = functools.partial(_qkv_kernel, q_scale=HEAD_DIM ** -0.5 * math.log2(math.e))
    return pl.pallas_call(
        kern,
        grid=(m // tm, len(QKV_TILES)),
        in_specs=[pl.BlockSpec((tm, d), lambda i, j: (i, 0)),
                  pl.BlockSpec((d, tn), lambda i, j: (0, j)),
                  pl.BlockSpec((tm, LANES), lambda i, j: (i % tps, 0)),
                  pl.BlockSpec((tm, LANES), lambda i, j: (i % tps, 0)),
                  any_spec, any_spec, any_spec, any_spec],
        out_specs=[pl.BlockSpec((N_KV, tm, LANES), lambda i, j: (jnp.minimum(j, n_q - 1), i, 0)),
                   pl.BlockSpec((N_KV, tm // CMP_STRIDE, CMP_STRIDE * LANES),
                                lambda i, j: (jnp.clip(j - cmp_first, 0, 1), i, 0)),
                   ext_spec, ext_spec, win_spec, win_spec, ext_spec],
        out_shape=[jax.ShapeDtypeStruct((n_q * N_KV, m, LANES), MXU_DTYPE),
                   jax.ShapeDtypeStruct((2 * N_KV, m // CMP_STRIDE, CMP_STRIDE * LANES), MXU_DTYPE),
                   sds(ksx), sds(vsx), sds(kwx), sds(vwx),
                   jax.ShapeDtypeStruct((N_KV, m, LANES), F32)],
        scratch_shapes=[pltpu.VMEM((tm, LANES), F32)],
        input_output_aliases={4: 2, 5: 3, 6: 4, 7: 5},
        compiler_params=_params("parallel", "arbitrary"),
        name="qkv_proj",
    )(h, w, cos, sin, ksx, vsx, kwx, vwx)


def _compress_kernel(ck_ref, cv_ref, w1k_ref, w2k_ref, pk_ref, w1v_ref, w2v_ref, pv_ref,
                     cos_ref, sin_ref, kc_ref, vc_ref):
    nch = ck_ref.shape[0]
    half = w1k_ref.shape[0] // 2

    def phi(c_ref, w1_ref, w2_ref, p_ref):
        c = c_ref[...]
        first = _dot(c, w1_ref[:half, :])
        second = _dot(c, w1_ref[half:, :])
        pos = _dot(p_ref[...], w1_ref[...])[0:1, :]
        hid = first + pltpu.roll(second, nch - 1, 0) + pos
        act = hid * jax.nn.sigmoid(hid)
        return _dot(act.astype(MXU_DTYPE), w2_ref[...])

    kc = phi(ck_ref, w1k_ref, w2k_ref, pk_ref)
    kc = kc * cos_ref[...] + pltpu.roll(kc, HEAD_DIM // 2, 1) * sin_ref[...]
    kc_ref[...] = kc.astype(kc_ref.dtype)
    vc_ref[...] = phi(cv_ref, w1v_ref, w2v_ref, pv_ref).astype(vc_ref.dtype)


def _compress(chunks, kbase, vbase, w1k, w2k, pk, w1v, w2v, pv, cos_c, sin_c, batch):
    _, _, nch, cw = chunks.shape
    const = lambda a: pl.BlockSpec(a.shape, lambda b, g: (0,) * a.ndim)
    out_spec = pl.BlockSpec((None, None, nch, HEAD_DIM), lambda b, g: (b, g, 0, 0))
    out_sds = jax.ShapeDtypeStruct((batch, N_KV, nch, HEAD_DIM), MXU_DTYPE)
    return pl.pallas_call(
        _compress_kernel,
        grid=(batch, N_KV),
        in_specs=[pl.BlockSpec((None, None, nch, cw), lambda b, g: (kbase + g, b, 0, 0)),
                  pl.BlockSpec((None, None, nch, cw), lambda b, g: (vbase + g, b, 0, 0)),
                  const(w1k), const(w2k), const(pk), const(w1v), const(w2v), const(pv),
                  const(cos_c), const(sin_c)],
        out_specs=[out_spec, out_spec],
        out_shape=[out_sds, out_sds],
        compiler_params=_params("parallel", "parallel"),
        name="compress",
    )(chunks, chunks, w1k, w2k, pk, w1v, w2v, pv, cos_c, sin_c)


_Branch = collections.namedtuple("_Branch", "s p m acc bias")


def _attn_kernel(q_ref, kc_ref, vc_ref, ksx_ref, vsx_ref, kw_ref, vwx_ref, gate_ref, mselt_ref, spread_ref,
                 o_ref, qx_scr, qw_scr, sc_scr, pc_scr, sd_scr, pd_scr, sw_scr, pw_scr, sa_scr, sb_scr,
                 pa_scr, pb_scr, bc_scr, bd_scr, bw_scr, mc_scr, mw_scr, m_scr, a_scr,
                 accc_scr, accw_scr, acc_scr, psum_scr, imp_scr, rank_scr, *, tq, tk, rb):
    i = pl.program_id(2)
    q0 = pl.multiple_of(i * tq, tq)
    r = GROUP
    rows = r * tq
    ncp = kc_ref.shape[0]
    ns = mselt_ref.shape[0]
    wlen = WINDOW + tq
    n_chunks = rows // rb
    t_col = q0 + lax.broadcasted_iota(jnp.int32, (tq, 1), 0)
    q = q_ref[...].reshape(rows, HEAD_DIM)
    head = lambda h: slice(h * tq, (h + 1) * tq)
    half = rows // 2
    halves = [slice(0, half), slice(half, rows)]
    lane = lambda kk: slice(kk * LANES, (kk + 1) * LANES)

    def load_scores(br, rs, bs, width):
        xs = [br.s[rs, lane(kk)] for kk in range(width // LANES)]
        return [x if b is None else x + b[bs, :] for x, b in zip(xs, br.bias)]

    def softmax_rows(br, width, chunks, first, cmp):
        def slices(c):
            r0 = c * rb
            return slice(r0, r0 + rb), slice(r0 % tq, r0 % tq + rb)

        for c in chunks:
            rs, bs = slices(c)
            xs = load_scores(br, rs, bs, width)
            mx = jnp.max(functools.reduce(jnp.maximum, xs), axis=-1, keepdims=True)
            if first:
                br.m[rs, :] = jnp.broadcast_to(mx, (rb, LANES))
            else:
                m_old = br.m[rs, :]
                m_new = jnp.maximum(m_old, mx)
                a_scr[rs, :] = jnp.exp2(m_old - m_new)
                br.m[rs, :] = m_new

        for c in chunks:
            rs, bs = slices(c)
            xs = load_scores(br, rs, bs, width)
            m = br.m[rs, :]
            ps = [jnp.exp2(x - m) for x in xs]
            if cmp:
                ps = [jnp.where(x > 0.5 * NEG, p, 0.0) for x, p in zip(xs, ps)]
                lsum = jnp.sum(functools.reduce(jnp.add, ps), axis=-1, keepdims=True)
                inv = 1.0 / jnp.where(lsum > 0.0, lsum, 1.0)
                ps = [p * inv for p in ps]
                for kk, p in enumerate(ps):
                    psum_scr[bs, lane(kk)] += p
            for kk, p in enumerate(ps):
                br.p[rs, lane(kk)] = p.astype(br.p.dtype)

    def scores(br, width, q_rows, k):
        for hs in halves:
            br.s[hs, :width] = lax.dot_general(q_rows(hs), k, NT_DIMS, preferred_element_type=F32)

    def weighted_values(br, width, v, *, first, cmp=False):
        for hi, hs in enumerate(halves):
            chunks = range(hi * n_chunks // 2, (hi + 1) * n_chunks // 2)
            softmax_rows(br, width, chunks, first, cmp)
            pv = _dot(br.p[hs, :width], v)
            cols = slice(0, v.shape[1])
            if first:
                br.acc[hs, cols] = pv
            else:
                a = a_scr[hs, :]
                for kk in range(v.shape[1] // LANES):
                    br.acc[hs, lane(kk)] = a * br.acc[hs, lane(kk)] + pv[:, lane(kk)]

    q_plain = lambda hs: q[hs]
    q_ext = lambda hs: qx_scr[hs, :]
    row_id = lax.broadcasted_iota(jnp.int32, (tq, LANES), 0)
    col_id = lax.broadcasted_iota(jnp.int32, (tq, LANES), 1)

    n_cl = ncp // LANES
    for kk in range(n_cl):
        cmp_end = (col_id + kk * LANES) * CMP_STRIDE + (CMP_LEN - 1)
        bc_scr[:, lane(kk)] = jnp.where(cmp_end <= t_col, 0.0, NEG)
    n_dl = tq // LANES
    for kk in range(n_dl):
        own = (col_id <= row_id - kk * LANES) & (row_id < (kk + 1) * LANES)
        bd_scr[:, lane(kk)] = jnp.where(own, 0.0, NEG)
    n_wl = wlen // LANES
    for kk in range(n_dl):
        bw_scr[:, lane(kk)] = jnp.where(col_id + kk * LANES > row_id, 0.0, NEG)
        bw_scr[:, lane(n_dl + kk)] = jnp.where(col_id + kk * LANES <= row_id, 0.0, NEG)

    cmp_br = _Branch(sc_scr, pc_scr, mc_scr, accc_scr, [bc_scr.at[:, lane(kk)] for kk in range(n_cl)])
    diag_br = _Branch(sd_scr, pd_scr, m_scr, acc_scr, [bd_scr.at[:, lane(kk)] for kk in range(n_dl)])
    win_bias = ([bw_scr.at[:, lane(kk)] for kk in range(n_dl)] + [None] * (n_wl - 2 * n_dl)
                + [bw_scr.at[:, lane(n_dl + kk)] for kk in range(n_dl)])
    win_br = _Branch(sw_scr, pw_scr, mw_scr, accw_scr, win_bias)
    sel_a = _Branch(sa_scr, pa_scr, m_scr, acc_scr, [None] * (tk // LANES))
    sel_b = _Branch(sb_scr, pb_scr, m_scr, acc_scr, [None] * (tk // LANES))

    psum_scr[...] = jnp.zeros(psum_scr.shape, F32)
    pad_col = jnp.where(lax.broadcasted_iota(jnp.int32, (rows, LANES), 1) == 0, NEG, 0.0)
    qw_scr[:, :LANES] = q
    qw_scr[:, LANES:] = pad_col.astype(qw_scr.dtype)
    scores(cmp_br, ncp, q_plain, kc_ref[...])
    w_rows = pl.ds(q0 + (kw_ref.shape[0] - ksx_ref.shape[0] - WINDOW), wlen)
    scores(win_br, wlen, lambda hs: qw_scr[hs, :], kw_ref[w_rows, :])

    weighted_values(cmp_br, ncp, vc_ref[...], first=True, cmp=True)
    scores(diag_br, tq, q_plain, ksx_ref[pl.ds(q0, tq), :LANES])
    mselt = mselt_ref[...]
    imp = jnp.zeros((ns, tq), F32)
    rem = psum_scr[...]
    for _ in range(3):
        piece = rem.astype(MXU_DTYPE)
        imp = imp + lax.dot_general(mselt, piece, NT_DIMS, preferred_element_type=F32)
        rem = rem - piece.astype(F32)
    blk = lax.broadcasted_iota(jnp.int32, (ns, tq), 0)
    tb = (q0 + lax.broadcasted_iota(jnp.int32, (ns, tq), 1)) // SEL_LEN
    forced = (blk == 0) | (blk == tb) | (blk == tb - 1)
    imp_scr[...] = jnp.where(blk <= tb, jnp.where(forced, jnp.inf, imp), -jnp.inf)
    rank_scr[...] = jnp.zeros(rank_scr.shape, F32)

    weighted_values(win_br, wlen, vwx_ref[w_rows, :], first=True)
    weighted_values(diag_br, tq, vsx_ref[pl.ds(q0, tq), :], first=True)

    sub = 8
    n_grp = ns // sub
    sub_id = lax.broadcasted_iota(jnp.int32, (sub, tq), 0)
    grp = lambda v: slice(sub * v, sub * (v + 1))
    for gm in range(n_grp):
        @pl.when(sub * gm * SEL_LEN <= q0 + tq - 1)
        def _():
            xs = [imp_scr[grp(v), :] for v in range(n_grp)]
            ranks = [rank_scr[grp(v), :] for v in range(n_grp)]
            for mp in range(sub * gm, sub * (gm + 1)):
                row = jnp.broadcast_to(xs[gm][mp % sub:mp % sub + 1, :], (sub, tq))
                for v, x in enumerate(xs):
                    if sub * v > mp:
                        ahead = row >= x
                    elif sub * v + sub - 1 <= mp:
                        ahead = row > x
                    else:
                        ahead = (row > x) | ((row == x) & (sub_id > mp % sub))
                    ranks[v] = ranks[v] + jnp.where(ahead, 1.0, 0.0)
            for v in range(n_grp):
                rank_scr[grp(v), :] = ranks[v]

    own_start = (q0 + lax.broadcasted_iota(jnp.int32, (ns, tq), 1)) // LANES * (LANES // SEL_LEN)
    off = jnp.where((rank_scr[...] < float(min(N_SEL, ns))) & (blk < own_start), 0.0, NEG)
    off = off.T.astype(MXU_DTYPE)
    if ns < LANES:
        off = jnp.concatenate([off, jnp.zeros((tq, LANES - ns), MXU_DTYPE)], axis=1)
    for h in range(r):
        qx_scr[head(h), :LANES] = q_ref[h]
        qx_scr[head(h), LANES:] = off

    last_tile = ksx_ref.shape[0] // tk - 1

    def tile_rows(t):
        return pl.ds(pl.multiple_of(jnp.minimum(t, last_tile) * tk, tk), tk)

    def sel_pair(jj, carry):
        t = 2 * jj
        scores(sel_b, tk, q_ext, ksx_ref[tile_rows(t + 1), :])
        weighted_values(sel_a, tk, vsx_ref[tile_rows(t), :], first=False)
        scores(sel_a, tk, q_ext, ksx_ref[tile_rows(t + 2), :])
        weighted_values(sel_b, tk, vsx_ref[tile_rows(t + 1), :], first=False)
        return carry

    n_tiles = (q0 + tq - LANES + tk - 1) // tk
    scores(sel_a, tk, q_ext, ksx_ref[tile_rows(0), :])
    lax.fori_loop(0, n_tiles // 2, sel_pair, 0)

    @pl.when(n_tiles % 2 == 1)
    def _():
        weighted_values(sel_a, tk, vsx_ref[tile_rows(n_tiles - 1), :], first=False)

    n_gates = r * N_NSA_BRANCH
    half_g = n_gates // 2
    gb = gate_ref[...].astype(MXU_DTYPE)
    wide = [_dot(gb, spread_ref[:, :half_g * LANES]), _dot(gb, spread_ref[:, half_g * LANES:])]

    def gate(h, br):
        c = h * N_NSA_BRANCH + br
        return wide[c // half_g][:, lane(c % half_g)]

    o_sel = acc_scr[:, :LANES] * (1.0 / acc_scr[:, LANES:])
    o_win = accw_scr[:, :LANES] * (1.0 / accw_scr[:, LANES:])
    for h in range(r):
        o = (gate(h, 0) * accc_scr[head(h), :] + gate(h, 1) * o_sel[head(h), :]
             + gate(h, 2) * o_win[head(h), :])
        o_ref[:, h * HEAD_DIM:(h + 1) * HEAD_DIM] = o.astype(o_ref.dtype)


def _attention(q, ksx, vsx, kwp, vwx, kc, vc, gates, mselt, batch, seq, tq, tk):
    nq = seq // tq
    ncp = kc.shape[2]
    ns = mselt.shape[0]
    assert ns <= LANES and seq % (2 * tk) == 0 and tk % tq == 0 and WINDOW % tq == 0
    assert tq % LANES == 0 and ncp % LANES == 0 and kwp.shape[2] >= seq + WINDOW
    rows = GROUP * tq
    wlen = WINDOW + tq
    ext_spec = pl.BlockSpec((None, seq, 2 * LANES), lambda b, g, i: (g, b, 0))
    cmp_spec = pl.BlockSpec((None, None, ncp, HEAD_DIM), lambda b, g, i: (b, g, 0, 0))
    f32 = lambda *shape: pltpu.VMEM(shape, F32)
    mxu = lambda *shape: pltpu.VMEM(shape, MXU_DTYPE)
    n_gates = GROUP * N_NSA_BRANCH
    spread = jnp.asarray(np.arange(LANES)[:, None] == np.arange(n_gates * LANES)[None, :] // LANES,
                         dtype=MXU_DTYPE)
    return pl.pallas_call(
        functools.partial(_attn_kernel, tq=tq, tk=tk, rb=PACKED_ROWS),
        grid=(batch, N_KV, nq),
        in_specs=[pl.BlockSpec((GROUP, tq, HEAD_DIM), lambda b, g, i: (g, b * nq + i, 0)),
                  cmp_spec, cmp_spec, ext_spec, ext_spec,
                  pl.BlockSpec((None, None) + kwp.shape[2:], lambda b, g, i: (g, b, 0, 0)),
                  pl.BlockSpec((None, None) + vwx.shape[2:], lambda b, g, i: (g, b, 0, 0)),
                  pl.BlockSpec((None, tq, LANES), lambda b, g, i: (g, b * nq + i, 0)),
                  pl.BlockSpec((ns, ncp), lambda b, g, i: (0, 0)),
                  pl.BlockSpec(spread.shape, lambda b, g, i: (0, 0))],
        out_specs=pl.BlockSpec((tq, GROUP * HEAD_DIM), lambda b, g, i: (b * nq + i, g)),
        out_shape=jax.ShapeDtypeStruct((batch * seq, N_HEADS * HEAD_DIM), MXU_DTYPE),
        scratch_shapes=[mxu(rows, 2 * LANES), mxu(rows, 2 * LANES),
                        f32(rows, ncp), mxu(rows, ncp),
                        f32(rows, tq), mxu(rows, tq),
                        f32(rows, wlen), mxu(rows, wlen),
                        f32(rows, tk), f32(rows, tk),
                        mxu(rows, tk), mxu(rows, tk),
                        f32(tq, ncp), f32(tq, tq), f32(tq, 2 * tq),
                        f32(rows, LANES), f32(rows, LANES), f32(rows, LANES),
                        f32(rows, LANES),
                        f32(rows, LANES),
                        f32(rows, 2 * LANES), f32(rows, 2 * LANES),
                        f32(tq, ncp),
                        f32(ns, tq), f32(ns, tq)],
        compiler_params=_params("parallel", "parallel", "arbitrary"),
        name="nsa_attention",
    )(q, kc, vc, ksx, vsx, kwp, vwx, gates, mselt, spread)


def _conv_kernel(h_ref, wx_ref, wb_ref, wc_ref, cw_ref, o_ref, ubuf, *, tiles_per_seq):
    i = pl.program_id(1)
    tm = h_ref.shape[0]
    h = h_ref[...]

    @pl.when(i % tiles_per_seq == 0)
    def _():
        ubuf[0:8, :] = jnp.zeros((8, ubuf.shape[1]), F32)

    w = cw_ref[...]
    for cs in _column_parts(o_ref.shape[1]):
        x_in = _dot(h, wx_ref[:, cs])
        gate_b = _dot(h, wb_ref[:, cs])
        gate_c = _dot(h, wc_ref[:, cs])
        ubuf[8:tm + 8, cs] = gate_c * x_in
        conv = (w[2:3, cs] * ubuf[8:tm + 8, cs] + w[1:2, cs] * ubuf[7:tm + 7, cs]
                + w[0:1, cs] * ubuf[6:tm + 6, cs])
        o_ref[:, cs] = (gate_b * conv).astype(o_ref.dtype)
        ubuf[0:8, cs] = ubuf[tm:tm + 8, cs]


def _conv_mixer(h, wx, wb, wc, cw, seq, tm, tn):
    m, d = h.shape
    n = wx.shape[1]
    wspec = pl.BlockSpec((d, tn), lambda j, i: (0, j))
    return pl.pallas_call(
        functools.partial(_conv_kernel, tiles_per_seq=seq // tm),
        grid=(n // tn, m // tm),
        in_specs=[pl.BlockSpec((tm, d), lambda j, i: (i, 0)), wspec, wspec, wspec,
                  pl.BlockSpec((8, tn), lambda j, i: (0, j))],
        out_specs=pl.BlockSpec((tm, tn), lambda j, i: (i, j)),
        out_shape=jax.ShapeDtypeStruct((m, n), MXU_DTYPE),
        scratch_shapes=[pltpu.VMEM((tm + 8, tn), F32)],
        compiler_params=_params("parallel", "arbitrary"),
        name="conv_mixer",
    )(h, wx, wb, wc, cw)


def _merge_kernel(oa_ref, v_ref, h_ref, wap_ref, wco_ref, wga_ref, wgc_ref, o_ref):
    h = h_ref[...]
    oa = oa_ref[...]
    v = v_ref[...]
    for cs in _column_parts(o_ref.shape[1]):
        y_attn = _dot(oa, wap_ref[:, cs])
        y_conv = _dot(v, wco_ref[:, cs])
        g_attn = jax.nn.sigmoid(_dot(h, wga_ref[:, cs]))
        g_conv = jax.nn.sigmoid(_dot(h, wgc_ref[:, cs]))
        o_ref[:, cs] = (g_attn * y_attn + g_conv * y_conv).astype(o_ref.dtype)


def _merge(oa, v, h, wap, wco, wga, wgc, tm, tn):
    m, d = h.shape
    n = wap.shape[1]
    aspec = pl.BlockSpec((tm, d), lambda i, j: (i, 0))
    wspec = pl.BlockSpec((d, tn), lambda i, j: (0, j))
    return pl.pallas_call(
        _merge_kernel,
        grid=(m // tm, n // tn),
        in_specs=[aspec, aspec, aspec, wspec, wspec, wspec, wspec],
        out_specs=pl.BlockSpec((tm, tn), lambda i, j: (i, j)),
        out_shape=jax.ShapeDtypeStruct((m, n), MXU_DTYPE),
        compiler_params=_params("parallel", "arbitrary"),
        name="gated_merge",
    )(oa, v, h, wap, wco, wga, wgc)


def _outproj_kernel(a_ref, w_ref, x_ref, g_ref, xo_ref, ho_ref):
    x = x_ref[...] + _dot(a_ref[...], w_ref[...])
    xo_ref[...] = x
    ho_ref[...] = _rms(x, g_ref[...]).astype(ho_ref.dtype)


def _outproj(a, w, x, g, tm):
    m, d = x.shape
    row = pl.BlockSpec((tm, d), lambda i: (i, 0))
    return pl.pallas_call(
        _outproj_kernel,
        grid=(m // tm,),
        in_specs=[row, pl.BlockSpec((d, d), lambda i: (0, 0)), row,
                  pl.BlockSpec((1, d), lambda i: (0, 0))],
        out_specs=[row, row],
        out_shape=[jax.ShapeDtypeStruct((m, d), F32), jax.ShapeDtypeStruct((m, d), MXU_DTYPE)],
        compiler_params=_params("parallel"),
        name="out_proj",
    )(a, w, x, g.reshape(1, d))


def _mlp_kernel(h_ref, wu_ref, wd_ref, x_ref, g_ref, *refs, last):
    acc = refs[-1]
    f = pl.program_id(1)

    @pl.when(f == 0)
    def _():
        acc[...] = x_ref[...]

    a = jnp.maximum(_dot(h_ref[...], wu_ref[...]), 0.0)
    acc[...] += _dot((a * a).astype(MXU_DTYPE), wd_ref[...])

    @pl.when(f == pl.num_programs(1) - 1)
    def _():
        x = acc[...]
        normed = _rms(x, g_ref[...])
        if last:
            refs[0][...] = normed
        else:
            refs[0][...] = x
            refs[1][...] = normed.astype(refs[1].dtype)


def _mlp(h, wu, wd, x, g, tm, tf, last):
    m, d = x.shape
    ff = wu.shape[1]
    row = pl.BlockSpec((tm, d), lambda i, f: (i, 0))
    if last:
        out_specs, out_shape = [row], [jax.ShapeDtypeStruct((m, d), F32)]
    else:
        out_specs = [row, row]
        out_shape = [jax.ShapeDtypeStruct((m, d), F32), jax.ShapeDtypeStruct((m, d), MXU_DTYPE)]
    return pl.pallas_call(
        functools.partial(_mlp_kernel, last=last),
        grid=(m // tm, ff // tf),
        in_specs=[row, pl.BlockSpec((d, tf), lambda i, f: (0, f)),
                  pl.BlockSpec((tf, d), lambda i, f: (f, 0)), row,
                  pl.BlockSpec((1, d), lambda i, f: (0, 0))],
        out_specs=out_specs,
        out_shape=out_shape,
        scratch_shapes=[pltpu.VMEM((tm, d), F32)],
        compiler_params=_params("parallel", "arbitrary"),
        name="relu2_mlp",
    )(h, wu, wd, x, g.reshape(1, d))


def _rope_tables(pos):
    half = HEAD_DIM // 2
    inv_freq = jnp.exp(-math.log(ROPE_THETA) * jnp.arange(half, dtype=F32) / half)
    ang = pos.astype(F32)[:, None] * inv_freq[None, :]
    cos, sin = jnp.cos(ang), jnp.sin(ang)
    return jnp.concatenate([cos, cos], axis=-1), jnp.concatenate([-sin, sin], axis=-1)


def _cmp_to_sel_t(ncp, ns):
    nc = ncp - 1
    cs = np.arange(nc) * CMP_STRIDE
    ss = np.arange(ns) * SEL_LEN
    ov = np.minimum(cs[:, None] + CMP_LEN, ss[None, :] + SEL_LEN) - np.maximum(cs[:, None], ss[None, :])
    m = np.zeros((ncp, ns), np.float32)
    m[:nc] = np.clip(ov, 0, None) / CMP_LEN
    return jnp.asarray(m.T, dtype=MXU_DTYPE)


def _block_onehot(seq):
    e = (np.arange(seq)[:, None] // SEL_LEN == np.arange(LANES)[None, :]).astype(np.float32)
    return jnp.asarray(e, dtype=MXU_DTYPE)


def kernel(x, norm1_g, w_in, cmp_pos_k, cmp_w1_k, cmp_w2_k, cmp_pos_v, cmp_w1_v, cmp_w2_v,
           conv_w, w_attn_proj, w_conv_out, w_o, norm2_g, w_up, w_down, final_g):
    batch, seq, d = x.shape
    depth = w_in.shape[0]
    m = batch * seq
    attn_dim = N_HEADS * HEAD_DIM
    kv_dim = N_KV * HEAD_DIM
    n_gate = N_HEADS * N_NSA_BRANCH
    ncp = seq // CMP_STRIDE
    ns = seq // SEL_LEN
    cast = lambda a: a.astype(MXU_DTYPE)

    tm = min(1024, seq)
    tm_small = min(512, seq)
    tn_wide = 1024
    tf = 1024
    tq = 256
    tk = 1024

    cos_t, sin_t = _rope_tables(jnp.arange(seq))
    cos_c, sin_c = _rope_tables(jnp.arange(ncp) * CMP_STRIDE + CMP_LEN - 1)
    mselt = _cmp_to_sel_t(ncp, ns)
    onehot = jnp.broadcast_to(jnp.tile(_block_onehot(seq), (batch, 1))[None], (N_KV, m, LANES))
    zeros = jnp.zeros((N_KV, m, LANES), MXU_DTYPE)
    ksx0 = jnp.concatenate([zeros, onehot], axis=-1)
    vsx0 = jnp.concatenate([zeros, jnp.ones_like(zeros)], axis=-1)
    before_start = jnp.asarray(np.arange(2 * LANES) == LANES, dtype=MXU_DTYPE)
    front = lambda a, rows: jnp.concatenate(
        [jnp.broadcast_to(rows, (N_KV, batch, tm, 2 * LANES)),
         a.reshape(N_KV, batch, seq, 2 * LANES)], axis=2)
    kwx0 = front(jnp.zeros_like(vsx0), before_start)
    vwx0 = front(vsx0, jnp.zeros((), MXU_DTYPE))

    x2 = x.reshape(m, d)
    h = _rmsnorm(x2, norm1_g[0], tm)
    out = None
    for l in range(depth):
        wl = w_in[l]
        o_kv = attn_dim
        kv = [wl[:, o_kv + j * kv_dim:o_kv + (j + 1) * kv_dim] for j in range(6)]
        o_ng = o_kv + 6 * kv_dim
        w_ng = wl[:, o_ng:o_ng + n_gate].reshape(d, N_KV, GROUP * N_NSA_BRANCH)
        w_ng = jnp.pad(w_ng, ((0, 0), (0, 0), (0, LANES - GROUP * N_NSA_BRANCH))).reshape(d, N_KV * LANES)
        w_qkv = cast(jnp.concatenate([wl[:, :attn_dim], kv[2], kv[4], kv[0], kv[1], kv[3], kv[5], w_ng], axis=1))
        o_cv = o_ng + n_gate
        w_x, w_b, w_c = [cast(wl[:, o_cv + j * d:o_cv + (j + 1) * d]) for j in range(3)]
        o_mg = o_cv + 3 * d
        w_ga, w_gc = [cast(wl[:, o_mg + j * d:o_mg + (j + 1) * d]) for j in range(2)]

        qkv, cmp_kv, ksx, vsx, kwx, vwx, gates = _qkv_proj(h, w_qkv, cos_t, sin_t, ksx0, vsx0, kwx0, vwx0,
                                                           seq, tm)
        ksx0, vsx0, kwx0, vwx0 = ksx, vsx, kwx, vwx

        chunks = cmp_kv.reshape(2 * N_KV, batch, ncp, CMP_STRIDE * HEAD_DIM)
        pad_pos = lambda p: cast(jnp.pad(p.reshape(1, CMP_LEN * HEAD_DIM), ((0, 7), (0, 0))))
        kc, vc = _compress(chunks, 0, N_KV, cast(cmp_w1_k[l]), cast(cmp_w2_k[l]), pad_pos(cmp_pos_k[l]),
                           cast(cmp_w1_v[l]), cast(cmp_w2_v[l]), pad_pos(cmp_pos_v[l]), cos_c, sin_c, batch)

        o_attn = _attention(qkv, ksx, vsx, kwx, vwx, kc, vc, gates, mselt, batch, seq, tq, tk)

        cw = jnp.pad(conv_w[l], ((0, 8 - CONV_WIDTH), (0, 0)))
        v_conv = _conv_mixer(h, w_x, w_b, w_c, cw, seq, tm_small, tn_wide)

        merged = _merge(o_attn, v_conv, h, cast(w_attn_proj[l]), cast(w_conv_out[l]), w_ga, w_gc,
                        tm_small, tn_wide)
        x2, h2 = _outproj(merged, cast(w_o[l]), x2, norm2_g[l], tm_small)

        last = l == depth - 1
        g_next = final_g if last else norm1_g[l + 1]
        res = _mlp(h2, cast(w_up[l]), cast(w_down[l]), x2, g_next, tm_small, tf, last)
        if last:
            out = res[0]
        else:
            x2, h = res
    return out.reshape(batch, seq, d)
```

```python
import collections
import functools
import math

import numpy as np
import jax
import jax.numpy as jnp
from jax import lax
from jax.experimental import pallas as pl
from jax.experimental.pallas import tpu as pltpu

N_HEADS = 16
HEAD_DIM = 128
N_KV = 4
GROUP = N_HEADS // N_KV
CMP_LEN = 32
CMP_STRIDE = 16
SEL_LEN = 64
N_SEL = 16
WINDOW = 512
N_NSA_BRANCH = 3
CONV_WIDTH = 3
ROPE_THETA = 10000.0
EPS = 1e-6

MXU_DTYPE = jnp.bfloat16
F32 = jnp.float32
NEG = -1e30
VMEM_LIMIT = 56 * 1024 * 1024
LANES = 128
MXU_WIDTH = 256
PACKED_ROWS = 16
NT_DIMS = (((1,), (1,)), ((), ()))


def _params(*sem):
    return pltpu.CompilerParams(dimension_semantics=sem, vmem_limit_bytes=VMEM_LIMIT)


def _dot(a, b):
    return jnp.dot(a, b, preferred_element_type=F32)


def _rms(x, g):
    return x * lax.rsqrt(jnp.mean(x * x, axis=-1, keepdims=True) + EPS) * g


def _column_parts(n):
    return [slice(c, c + MXU_WIDTH) for c in range(0, n, MXU_WIDTH)]


def _norm_kernel(x_ref, g_ref, o_ref):
    o_ref[...] = _rms(x_ref[...], g_ref[...]).astype(o_ref.dtype)


def _rmsnorm(x2d, g, tm):
    m, d = x2d.shape
    return pl.pallas_call(
        _norm_kernel,
        grid=(m // tm,),
        in_specs=[pl.BlockSpec((tm, d), lambda i: (i, 0)), pl.BlockSpec((1, d), lambda i: (0, 0))],
        out_specs=pl.BlockSpec((tm, d), lambda i: (i, 0)),
        out_shape=jax.ShapeDtypeStruct((m, d), MXU_DTYPE),
        compiler_params=_params("parallel"),
        name="rmsnorm",
    )(x2d, g.reshape(1, d))


QKV_TILES = ("q", "q", "q", "q", "k_sel", "k_win", "k_cmp", "v_cmp", "v_sel", "v_win", "gate")


def _qkv_kernel(h_ref, w_ref, cos_ref, sin_ref, ksx_in, vsx_in, kwx_in, vwx_in,
                o_ref, cmp_ref, ksx_ref, vsx_ref, kwx_ref, vwx_ref, gate_ref, stage, *, q_scale):
    del ksx_in, vsx_in, kwx_in, vwx_in
    j = pl.program_id(1)
    dest = {"q": o_ref, "k_cmp": cmp_ref, "v_cmp": cmp_ref, "k_sel": ksx_ref, "v_sel": vsx_ref,
            "k_win": kwx_ref, "v_win": vwx_ref, "gate": gate_ref}

    def put(ref, rope, scale, chunked, gate):
        h = h_ref[...]
        for cs in _column_parts(w_ref.shape[1]):
            acc = _dot(h, w_ref[:, cs])
            for c in range(MXU_WIDTH // LANES):
                xc = acc[:, c * LANES:(c + 1) * LANES]
                head = cs.start // LANES + c
                if rope:
                    xc = (xc * (cos_ref[...] * scale)
                          + pltpu.roll(xc, HEAD_DIM // 2, 1) * (sin_ref[...] * scale))
                if chunked:
                    stage[...] = xc
                    n_rows = stage.shape[0] // CMP_STRIDE
                    for l in range(CMP_STRIDE):
                        ref[head, :, l * LANES:(l + 1) * LANES] = (
                            stage[pl.ds(l, n_rows, stride=CMP_STRIDE), :].astype(ref.dtype))
                elif gate:
                    ref[head] = jax.nn.sigmoid(xc)
                else:
                    ref[head] = xc.astype(ref.dtype)

    for name in dict.fromkeys(QKV_TILES):
        tiles = [t for t, n in enumerate(QKV_TILES) if n == name]

        @pl.when((j >= tiles[0]) & (j <= tiles[-1]))
        def _():
            put(dest[name], rope=name in ("q", "k_sel", "k_win"), scale=q_scale if name == "q" else 1.0,
                chunked=name in ("k_cmp", "v_cmp"), gate=name == "gate")


def _qkv_proj(h, w, cos, sin, ksx, vsx, kwx, vwx, seq, tm):
    m, d = h.shape
    tn = N_KV * LANES
    assert w.shape[1] == tn * len(QKV_TILES) and seq % tm == 0 and kwx.shape[2] == seq + tm
    tps = seq // tm
    n_q = QKV_TILES.count("q")
    cmp_first = QKV_TILES.index("k_cmp")
    assert QKV_TILES[cmp_first + 1] == "v_cmp" and tm % (PACKED_ROWS * CMP_STRIDE) == 0
    ext_spec = pl.BlockSpec((N_KV, tm, LANES), lambda i, j: (0, i, 0))
    win_spec = pl.BlockSpec((N_KV, None, tm, LANES), lambda i, j: (0, i // tps, 1 + i % tps, 0))
    any_spec = pl.BlockSpec(memory_space=pl.ANY)
    sds = lambda a: jax.ShapeDtypeStruct(a.shape, a.dtype)
    kern = functools.partial(_qkv_kernel, q_scale=HEAD_DIM ** -0.5 * math.log2(math.e))
    return pl.pallas_call(
        kern,
        grid=(m // tm, len(QKV_TILES)),
        in_specs=[pl.BlockSpec((tm, d), lambda i, j: (i, 0)),
                  pl.BlockSpec((d, tn), lambda i, j: (0, j)),
                  pl.BlockSpec((tm, LANES), lambda i, j: (i % tps, 0)),
                  pl.BlockSpec((tm, LANES), lambda i, j: (i % tps, 0)),
                  any_spec, any_spec, any_spec, any_spec],
        out_specs=[pl.BlockSpec((N_KV, tm, LANES), lambda i, j: (jnp.minimum(j, n_q - 1), i, 0)),
                   pl.BlockSpec((N_KV, tm // CMP_STRIDE, CMP_STRIDE * LANES),
                                lambda i, j: (jnp.clip(j - cmp_first, 0, 1), i, 0)),
                   ext_spec, ext_spec, win_spec, win_spec, ext_spec],
        out_shape=[jax.ShapeDtypeStruct((n_q * N_KV, m, LANES), MXU_DTYPE),
                   jax.ShapeDtypeStruct((2 * N_KV, m // CMP_STRIDE, CMP_STRIDE * LANES), MXU_DTYPE),
                   sds(ksx), sds(vsx), sds(kwx), sds(vwx),
                   jax.ShapeDtypeStruct((N_KV, m, LANES), F32)],
        scratch_shapes=[pltpu.VMEM((tm, LANES), F32)],
        input_output_aliases={4: 2, 5: 3, 6: 4, 7: 5},
        compiler_params=_params("parallel", "arbitrary"),
        name="qkv_proj",
    )(h, w, cos, sin, ksx, vsx, kwx, vwx)


def _compress_kernel(ck_ref, cv_ref, w1k_ref, w2k_ref, pk_ref, w1v_ref, w2v_ref, pv_ref,
                     cos_ref, sin_ref, kc_ref, vc_ref):
    nch = ck_ref.shape[0]
    half = w1k_ref.shape[0] // 2

    def phi(c_ref, w1_ref, w2_ref, p_ref):
        c = c_ref[...]
        first = _dot(c, w1_ref[:half, :])
        second = _dot(c, w1_ref[half:, :])
        pos = _dot(p_ref[...], w1_ref[...])[0:1, :]
        hid = first + pltpu.roll(second, nch - 1, 0) + pos
        act = hid * jax.nn.sigmoid(hid)
        return _dot(act.astype(MXU_DTYPE), w2_ref[...])

    kc = phi(ck_ref, w1k_ref, w2k_ref, pk_ref)
    kc = kc * cos_ref[...] + pltpu.roll(kc, HEAD_DIM // 2, 1) * sin_ref[...]
    kc_ref[...] = kc.astype(kc_ref.dtype)
    vc_ref[...] = phi(cv_ref, w1v_ref, w2v_ref, pv_ref).astype(vc_ref.dtype)


def _compress(chunks, kbase, vbase, w1k, w2k, pk, w1v, w2v, pv, cos_c, sin_c, batch):
    _, _, nch, cw = chunks.shape
    const = lambda a: pl.BlockSpec(a.shape, lambda b, g: (0,) * a.ndim)
    out_spec = pl.BlockSpec((None, None, nch, HEAD_DIM), lambda b, g: (b, g, 0, 0))
    out_sds = jax.ShapeDtypeStruct((batch, N_KV, nch, HEAD_DIM), MXU_DTYPE)
    return pl.pallas_call(
        _compress_kernel,
        grid=(batch, N_KV),
        in_specs=[pl.BlockSpec((None, None, nch, cw), lambda b, g: (kbase + g, b, 0, 0)),
                  pl.BlockSpec((None, None, nch, cw), lambda b, g: (vbase + g, b, 0, 0)),
                  const(w1k), const(w2k), const(pk), const(w1v), const(w2v), const(pv),
                  const(cos_c), const(sin_c)],
        out_specs=[out_spec, out_spec],
        out_shape=[out_sds, out_sds],
        compiler_params=_params("parallel", "parallel"),
        name="compress",
    )(chunks, chunks, w1k, w2k, pk, w1v, w2v, pv, cos_c, sin_c)


_Branch = collections.namedtuple("_Branch", "s p m acc bias")


def _attn_kernel(q_ref, kc_ref, vc_ref, ksx_ref, vsx_ref, kw_ref, vwx_ref, gate_ref, mselt_ref, spread_ref,
                 o_ref, qx_scr, qw_scr, sc_scr, pc_scr, sd_scr, pd_scr, sw_scr, pw_scr, sa_scr, sb_scr,
                 pa_scr, pb_scr, bc_scr, bd_scr, bw_scr, mc_scr, mw_scr, m_scr, a_scr,
                 accc_scr, accw_scr, acc_scr, psum_scr, imp_scr, rank_scr, *, tq, tk, rb):
    i = pl.program_id(2)
    q0 = pl.multiple_of(i * tq, tq)
    r = GROUP
    rows = r * tq
    ncp = kc_ref.shape[0]
    ns = mselt_ref.shape[0]
    wlen = WINDOW + tq
    n_chunks = rows // rb
    t_col = q0 + lax.broadcasted_iota(jnp.int32, (tq, 1), 0)
    q = q_ref[...].reshape(rows, HEAD_DIM)
    head = lambda h: slice(h * tq, (h + 1) * tq)
    lane = lambda kk: slice(kk * LANES, (kk + 1) * LANES)

    def load_scores(br, rs, bs, width):
        xs = [br.s[rs, lane(kk)] for kk in range(width // LANES)]
        return [x if b is None else x + b[bs, :] for x, b in zip(xs, br.bias)]

    def softmax_rows(br, width, chunks, first, cmp):
        def slices(c):
            r0 = c * rb
            return slice(r0, r0 + rb), slice(r0 % tq, r0 % tq + rb)

        for c in chunks:
            rs, bs = slices(c)
            xs = load_scores(br, rs, bs, width)
            mx = jnp.max(functools.reduce(jnp.maximum, xs), axis=-1, keepdims=True)
            if first:
                br.m[rs, :] = jnp.broadcast_to(mx, (rb, LANES))
            else:
                m_old = br.m[rs, :]
                m_new = jnp.maximum(m_old, mx)
                a_scr[rs, :] = jnp.exp2(m_old - m_new)
                br.m[rs, :] = m_new

        for c in chunks:
            rs, bs = slices(c)
            xs = load_scores(br, rs, bs, width)
            m = br.m[rs, :]
            ps = [jnp.exp2(x - m) for x in xs]
            if cmp:
                ps = [jnp.where(x > 0.5 * NEG, p, 0.0) for x, p in zip(xs, ps)]
                lsum = jnp.sum(functools.reduce(jnp.add, ps), axis=-1, keepdims=True)
                inv = 1.0 / jnp.where(lsum > 0.0, lsum, 1.0)
                ps = [p * inv for p in ps]
                for kk, p in enumerate(ps):
                    psum_scr[bs, lane(kk)] += p
            for kk, p in enumerate(ps):
                br.p[rs, lane(kk)] = p.astype(br.p.dtype)

    def row_parts(n):
        return [slice(p * rows // n, (p + 1) * rows // n) for p in range(n)]

    def scores(br, width, q_rows, k, parts=2):
        for hs in row_parts(parts):
            br.s[hs, :width] = lax.dot_general(q_rows(hs), k, NT_DIMS, preferred_element_type=F32)

    def weighted_values(br, width, v, *, first, cmp=False, parts=2):
        for hi, hs in enumerate(row_parts(parts)):
            chunks = range(hi * n_chunks // parts, (hi + 1) * n_chunks // parts)
            softmax_rows(br, width, chunks, first, cmp)
            pv = _dot(br.p[hs, :width], v)
            cols = slice(0, v.shape[1])
            if first:
                br.acc[hs, cols] = pv
            else:
                a = a_scr[hs, :]
                for kk in range(v.shape[1] // LANES):
                    br.acc[hs, lane(kk)] = a * br.acc[hs, lane(kk)] + pv[:, lane(kk)]

    q_plain = lambda hs: q[hs]
    q_ext = lambda hs: qx_scr[hs, :]
    row_id = lax.broadcasted_iota(jnp.int32, (tq, LANES), 0)
    col_id = lax.broadcasted_iota(jnp.int32, (tq, LANES), 1)

    n_cl = ncp // LANES
    for kk in range(n_cl):
        cmp_end = (col_id + kk * LANES) * CMP_STRIDE + (CMP_LEN - 1)
        bc_scr[:, lane(kk)] = jnp.where(cmp_end <= t_col, 0.0, NEG)
    n_dl = tq // LANES
    for kk in range(n_dl):
        own = (col_id <= row_id - kk * LANES) & (row_id < (kk + 1) * LANES)
        bd_scr[:, lane(kk)] = jnp.where(own, 0.0, NEG)
    n_wl = wlen // LANES
    for kk in range(n_dl):
        bw_scr[:, lane(kk)] = jnp.where(col_id + kk * LANES > row_id, 0.0, NEG)
        bw_scr[:, lane(n_dl + kk)] = jnp.where(col_id + kk * LANES <= row_id, 0.0, NEG)

    cmp_br = _Branch(sc_scr, pc_scr, mc_scr, accc_scr, [bc_scr.at[:, lane(kk)] for kk in range(n_cl)])
    diag_br = _Branch(sd_scr, pd_scr, m_scr, acc_scr, [bd_scr.at[:, lane(kk)] for kk in range(n_dl)])
    win_bias = ([bw_scr.at[:, lane(kk)] for kk in range(n_dl)] + [None] * (n_wl - 2 * n_dl)
                + [bw_scr.at[:, lane(n_dl + kk)] for kk in range(n_dl)])
    win_br = _Branch(sw_scr, pw_scr, mw_scr, accw_scr, win_bias)
    sel_a = _Branch(sa_scr, pa_scr, m_scr, acc_scr, [None] * (tk // LANES))
    sel_b = _Branch(sb_scr, pb_scr, m_scr, acc_scr, [None] * (tk // LANES))

    psum_scr[...] = jnp.zeros(psum_scr.shape, F32)
    pad_col = jnp.where(lax.broadcasted_iota(jnp.int32, (rows, LANES), 1) == 0, NEG, 0.0)
    qw_scr[:, :LANES] = q
    qw_scr[:, LANES:] = pad_col.astype(qw_scr.dtype)
    scores(cmp_br, ncp, q_plain, kc_ref[...])
    w_rows = pl.ds(q0 + (kw_ref.shape[0] - ksx_ref.shape[0] - WINDOW), wlen)
    scores(win_br, wlen, lambda hs: qw_scr[hs, :], kw_ref[w_rows, :])

    weighted_values(cmp_br, ncp, vc_ref[...], first=True, cmp=True)
    scores(diag_br, tq, q_plain, ksx_ref[pl.ds(q0, tq), :LANES])
    mselt = mselt_ref[...]
    imp = jnp.zeros((ns, tq), F32)
    rem = psum_scr[...]
    for _ in range(3):
        piece = rem.astype(MXU_DTYPE)
        imp = imp + lax.dot_general(mselt, piece, NT_DIMS, preferred_element_type=F32)
        rem = rem - piece.astype(F32)
    blk = lax.broadcasted_iota(jnp.int32, (ns, tq), 0)
    tb = (q0 + lax.broadcasted_iota(jnp.int32, (ns, tq), 1)) // SEL_LEN
    forced = (blk == 0) | (blk == tb) | (blk == tb - 1)
    imp_scr[...] = jnp.where(blk <= tb, jnp.where(forced, jnp.inf, imp), -jnp.inf)
    rank_scr[...] = jnp.zeros(rank_scr.shape, F32)

    weighted_values(win_br, wlen, vwx_ref[w_rows, :], first=True)
    weighted_values(diag_br, tq, vsx_ref[pl.ds(q0, tq), :], first=True)

    sub = 8
    n_grp = ns // sub
    sub_id = lax.broadcasted_iota(jnp.int32, (sub, tq), 0)
    grp = lambda v: slice(sub * v, sub * (v + 1))
    for gm in range(n_grp):
        @pl.when(sub * gm * SEL_LEN <= q0 + tq - 1)
        def _():
            xs = [imp_scr[grp(v), :] for v in range(n_grp)]
            ranks = [rank_scr[grp(v), :] for v in range(n_grp)]
            for mp in range(sub * gm, sub * (gm + 1)):
                row = jnp.broadcast_to(xs[gm][mp % sub:mp % sub + 1, :], (sub, tq))
                for v, x in enumerate(xs):
                    if sub * v > mp:
                        ahead = row >= x
                    elif sub * v + sub - 1 <= mp:
                        ahead = row > x
                    else:
                        ahead = (row > x) | ((row == x) & (sub_id > mp % sub))
                    ranks[v] = ranks[v] + jnp.where(ahead, 1.0, 0.0)
            for v in range(n_grp):
                rank_scr[grp(v), :] = ranks[v]

    own_start = (q0 + lax.broadcasted_iota(jnp.int32, (ns, tq), 1)) // LANES * (LANES // SEL_LEN)
    off = jnp.where((rank_scr[...] < float(min(N_SEL, ns))) & (blk < own_start), 0.0, NEG)
    off = off.T.astype(MXU_DTYPE)
    if ns < LANES:
        off = jnp.concatenate([off, jnp.zeros((tq, LANES - ns), MXU_DTYPE)], axis=1)
    for h in range(r):
        qx_scr[head(h), :LANES] = q_ref[h]
        qx_scr[head(h), LANES:] = off

    last_tile = ksx_ref.shape[0] // tk - 1

    def tile_rows(t):
        return pl.ds(pl.multiple_of(jnp.minimum(t, last_tile) * tk, tk), tk)

    def sel_pair(jj, carry):
        t = 2 * jj
        scores(sel_b, tk, q_ext, ksx_ref[tile_rows(t + 1), :], parts=1)
        weighted_values(sel_a, tk, vsx_ref[tile_rows(t), :], first=False, parts=1)
        scores(sel_a, tk, q_ext, ksx_ref[tile_rows(t + 2), :], parts=1)
        weighted_values(sel_b, tk, vsx_ref[tile_rows(t + 1), :], first=False, parts=1)
        return carry

    n_tiles = (q0 + tq - LANES + tk - 1) // tk
    scores(sel_a, tk, q_ext, ksx_ref[tile_rows(0), :])
    lax.fori_loop(0, n_tiles // 2, sel_pair, 0)

    @pl.when(n_tiles % 2 == 1)
    def _():
        weighted_values(sel_a, tk, vsx_ref[tile_rows(n_tiles - 1), :], first=False)

    n_gates = r * N_NSA_BRANCH
    half_g = n_gates // 2
    gb = gate_ref[...].astype(MXU_DTYPE)
    wide = [_dot(gb, spread_ref[:, :half_g * LANES]), _dot(gb, spread_ref[:, half_g * LANES:])]

    def gate(h, br):
        c = h * N_NSA_BRANCH + br
        return wide[c // half_g][:, lane(c % half_g)]

    o_sel = acc_scr[:, :LANES] * (1.0 / acc_scr[:, LANES:])
    o_win = accw_scr[:, :LANES] * (1.0 / accw_scr[:, LANES:])
    for h in range(r):
        o = (gate(h, 0) * accc_scr[head(h), :] + gate(h, 1) * o_sel[head(h), :]
             + gate(h, 2) * o_win[head(h), :])
        o_ref[:, h * HEAD_DIM:(h + 1) * HEAD_DIM] = o.astype(o_ref.dtype)


def _attention(q, ksx, vsx, kwp, vwx, kc, vc, gates, mselt, batch, seq, tq, tk):
    nq = seq // tq
    ncp = kc.shape[2]
    ns = mselt.shape[0]
    assert ns <= LANES and seq % (2 * tk) == 0 and tk % tq == 0 and WINDOW % tq == 0
    assert tq % LANES == 0 and ncp % LANES == 0 and kwp.shape[2] >= seq + WINDOW
    rows = GROUP * tq
    wlen = WINDOW + tq
    ext_spec = pl.BlockSpec((None, seq, 2 * LANES), lambda b, g, i: (g, b, 0))
    cmp_spec = pl.BlockSpec((None, None, ncp, HEAD_DIM), lambda b, g, i: (b, g, 0, 0))
    f32 = lambda *shape: pltpu.VMEM(shape, F32)
    mxu = lambda *shape: pltpu.VMEM(shape, MXU_DTYPE)
    n_gates = GROUP * N_NSA_BRANCH
    spread = jnp.asarray(np.arange(LANES)[:, None] == np.arange(n_gates * LANES)[None, :] // LANES,
                         dtype=MXU_DTYPE)
    return pl.pallas_call(
        functools.partial(_attn_kernel, tq=tq, tk=tk, rb=PACKED_ROWS),
        grid=(batch, N_KV, nq),
        in_specs=[pl.BlockSpec((GROUP, tq, HEAD_DIM), lambda b, g, i: (g, b * nq + i, 0)),
                  cmp_spec, cmp_spec, ext_spec, ext_spec,
                  pl.BlockSpec((None, None) + kwp.shape[2:], lambda b, g, i: (g, b, 0, 0)),
                  pl.BlockSpec((None, None) + vwx.shape[2:], lambda b, g, i: (g, b, 0, 0)),
                  pl.BlockSpec((None, tq, LANES), lambda b, g, i: (g, b * nq + i, 0)),
                  pl.BlockSpec((ns, ncp), lambda b, g, i: (0, 0)),
                  pl.BlockSpec(spread.shape, lambda b, g, i: (0, 0))],
        out_specs=pl.BlockSpec((tq, GROUP * HEAD_DIM), lambda b, g, i: (b * nq + i, g)),
        out_shape=jax.ShapeDtypeStruct((batch * seq, N_HEADS * HEAD_DIM), MXU_DTYPE),
        scratch_shapes=[mxu(rows, 2 * LANES), mxu(rows, 2 * LANES),
                        f32(rows, ncp), mxu(rows, ncp),
                        f32(rows, tq), mxu(rows, tq),
                        f32(rows, wlen), mxu(rows, wlen),
                        f32(rows, tk), f32(rows, tk),
                        mxu(rows, tk), mxu(rows, tk),
                        f32(tq, ncp), f32(tq, tq), f32(tq, 2 * tq),
                        f32(rows, LANES), f32(rows, LANES), f32(rows, LANES),
                        f32(rows, LANES),
                        f32(rows, LANES),
                        f32(rows, 2 * LANES), f32(rows, 2 * LANES),
                        f32(tq, ncp),
                        f32(ns, tq), f32(ns, tq)],
        compiler_params=_params("parallel", "parallel", "arbitrary"),
        name="nsa_attention",
    )(q, kc, vc, ksx, vsx, kwp, vwx, gates, mselt, spread)


def _conv_kernel(h_ref, wx_ref, wb_ref, wc_ref, cw_ref, o_ref, ubuf, *, tiles_per_seq):
    i = pl.program_id(1)
    tm = h_ref.shape[0]
    h = h_ref[...]

    @pl.when(i % tiles_per_seq == 0)
    def _():
        ubuf[0:8, :] = jnp.zeros((8, ubuf.shape[1]), F32)

    w = cw_ref[...]
    for cs in _column_parts(o_ref.shape[1]):
        x_in = _dot(h, wx_ref[:, cs])
        gate_b = _dot(h, wb_ref[:, cs])
        gate_c = _dot(h, wc_ref[:, cs])
        ubuf[8:tm + 8, cs] = gate_c * x_in
        conv = (w[2:3, cs] * ubuf[8:tm + 8, cs] + w[1:2, cs] * ubuf[7:tm + 7, cs]
                + w[0:1, cs] * ubuf[6:tm + 6, cs])
        o_ref[:, cs] = (gate_b * conv).astype(o_ref.dtype)
        ubuf[0:8, cs] = ubuf[tm:tm + 8, cs]


def _conv_mixer(h, wx, wb, wc, cw, seq, tm, tn):
    m, d = h.shape
    n = wx.shape[1]
    wspec = pl.BlockSpec((d, tn), lambda j, i: (0, j))
    return pl.pallas_call(
        functools.partial(_conv_kernel, tiles_per_seq=seq // tm),
        grid=(n // tn, m // tm),
        in_specs=[pl.BlockSpec((tm, d), lambda j, i: (i, 0)), wspec, wspec, wspec,
                  pl.BlockSpec((8, tn), lambda j, i: (0, j))],
        out_specs=pl.BlockSpec((tm, tn), lambda j, i: (i, j)),
        out_shape=jax.ShapeDtypeStruct((m, n), MXU_DTYPE),
        scratch_shapes=[pltpu.VMEM((tm + 8, tn), F32)],
        compiler_params=_params("parallel", "arbitrary"),
        name="conv_mixer",
    )(h, wx, wb, wc, cw)


def _merge_kernel(oa_ref, v_ref, h_ref, wap_ref, wco_ref, wga_ref, wgc_ref, o_ref):
    h = h_ref[...]
    oa = oa_ref[...]
    v = v_ref[...]
    for cs in _column_parts(o_ref.shape[1]):
        y_attn = _dot(oa, wap_ref[:, cs])
        y_conv = _dot(v, wco_ref[:, cs])
        g_attn = jax.nn.sigmoid(_dot(h, wga_ref[:, cs]))
        g_conv = jax.nn.sigmoid(_dot(h, wgc_ref[:, cs]))
        o_ref[:, cs] = (g_attn * y_attn + g_conv * y_conv).astype(o_ref.dtype)


def _merge(oa, v, h, wap, wco, wga, wgc, tm, tn):
    m, d = h.shape
    n = wap.shape[1]
    aspec = pl.BlockSpec((tm, d), lambda i, j: (i, 0))
    wspec = pl.BlockSpec((d, tn), lambda i, j: (0, j))
    return pl.pallas_call(
        _merge_kernel,
        grid=(m // tm, n // tn),
        in_specs=[aspec, aspec, aspec, wspec, wspec, wspec, wspec],
        out_specs=pl.BlockSpec((tm, tn), lambda i, j: (i, j)),
        out_shape=jax.ShapeDtypeStruct((m, n), MXU_DTYPE),
        compiler_params=_params("parallel", "arbitrary"),
        name="gated_merge",
    )(oa, v, h, wap, wco, wga, wgc)


def _outproj_kernel(a_ref, w_ref, x_ref, g_ref, xo_ref, ho_ref):
    x = x_ref[...] + _dot(a_ref[...], w_ref[...])
    xo_ref[...] = x
    ho_ref[...] = _rms(x, g_ref[...]).astype(ho_ref.dtype)


def _outproj(a, w, x, g, tm):
    m, d = x.shape
    row = pl.BlockSpec((tm, d), lambda i: (i, 0))
    return pl.pallas_call(
        _outproj_kernel,
        grid=(m // tm,),
        in_specs=[row, pl.BlockSpec((d, d), lambda i: (0, 0)), row,
                  pl.BlockSpec((1, d), lambda i: (0, 0))],
        out_specs=[row, row],
        out_shape=[jax.ShapeDtypeStruct((m, d), F32), jax.ShapeDtypeStruct((m, d), MXU_DTYPE)],
        compiler_params=_params("parallel"),
        name="out_proj",
    )(a, w, x, g.reshape(1, d))


def _mlp_kernel(h_ref, wu_ref, wd_ref, x_ref, g_ref, *refs, last):
    acc = refs[-1]
    f = pl.program_id(1)

    @pl.when(f == 0)
    def _():
        acc[...] = x_ref[...]

    a = jnp.maximum(_dot(h_ref[...], wu_ref[...]), 0.0)
    acc[...] += _dot((a * a).astype(MXU_DTYPE), wd_ref[...])

    @pl.when(f == pl.num_programs(1) - 1)
    def _():
        x = acc[...]
        normed = _rms(x, g_ref[...])
        if last:
            refs[0][...] = normed
        else:
            refs[0][...] = x
            refs[1][...] = normed.astype(refs[1].dtype)


def _mlp(h, wu, wd, x, g, tm, tf, last):
    m, d = x.shape
    ff = wu.shape[1]
    row = pl.BlockSpec((tm, d), lambda i, f: (i, 0))
    if last:
        out_specs, out_shape = [row], [jax.ShapeDtypeStruct((m, d), F32)]
    else:
        out_specs = [row, row]
        out_shape = [jax.ShapeDtypeStruct((m, d), F32), jax.ShapeDtypeStruct((m, d), MXU_DTYPE)]
    return pl.pallas_call(
        functools.partial(_mlp_kernel, last=last),
        grid=(m // tm, ff // tf),
        in_specs=[row, pl.BlockSpec((d, tf), lambda i, f: (0, f)),
                  pl.BlockSpec((tf, d), lambda i, f: (f, 0)), row,
                  pl.BlockSpec((1, d), lambda i, f: (0, 0))],
        out_specs=out_specs,
        out_shape=out_shape,
        scratch_shapes=[pltpu.VMEM((tm, d), F32)],
        compiler_params=_params("parallel", "arbitrary"),
        name="relu2_mlp",
    )(h, wu, wd, x, g.reshape(1, d))


def _rope_tables(pos):
    half = HEAD_DIM // 2
    inv_freq = jnp.exp(-math.log(ROPE_THETA) * jnp.arange(half, dtype=F32) / half)
    ang = pos.astype(F32)[:, None] * inv_freq[None, :]
    cos, sin = jnp.cos(ang), jnp.sin(ang)
    return jnp.concatenate([cos, cos], axis=-1), jnp.concatenate([-sin, sin], axis=-1)


def _cmp_to_sel_t(ncp, ns):
    nc = ncp - 1
    cs = np.arange(nc) * CMP_STRIDE
    ss = np.arange(ns) * SEL_LEN
    ov = np.minimum(cs[:, None] + CMP_LEN, ss[None, :] + SEL_LEN) - np.maximum(cs[:, None], ss[None, :])
    m = np.zeros((ncp, ns), np.float32)
    m[:nc] = np.clip(ov, 0, None) / CMP_LEN
    return jnp.asarray(m.T, dtype=MXU_DTYPE)


def _block_onehot(seq):
    e = (np.arange(seq)[:, None] // SEL_LEN == np.arange(LANES)[None, :]).astype(np.float32)
    return jnp.asarray(e, dtype=MXU_DTYPE)


def kernel(x, norm1_g, w_in, cmp_pos_k, cmp_w1_k, cmp_w2_k, cmp_pos_v, cmp_w1_v, cmp_w2_v,
           conv_w, w_attn_proj, w_conv_out, w_o, norm2_g, w_up, w_down, final_g):
    batch, seq, d = x.shape
    depth = w_in.shape[0]
    m = batch * seq
    attn_dim = N_HEADS * HEAD_DIM
    kv_dim = N_KV * HEAD_DIM
    n_gate = N_HEADS * N_NSA_BRANCH
    ncp = seq // CMP_STRIDE
    ns = seq // SEL_LEN
    cast = lambda a: a.astype(MXU_DTYPE)

    tm = min(1024, seq)
    tm_small = min(512, seq)
    tn_wide = 1024
    tf = 1024
    tq = 256
    tk = 1024

    cos_t, sin_t = _rope_tables(jnp.arange(seq))
    cos_c, sin_c = _rope_tables(jnp.arange(ncp) * CMP_STRIDE + CMP_LEN - 1)
    mselt = _cmp_to_sel_t(ncp, ns)
    onehot = jnp.broadcast_to(jnp.tile(_block_onehot(seq), (batch, 1))[None], (N_KV, m, LANES))
    zeros = jnp.zeros((N_KV, m, LANES), MXU_DTYPE)
    ksx0 = jnp.concatenate([zeros, onehot], axis=-1)
    vsx0 = jnp.concatenate([zeros, jnp.ones_like(zeros)], axis=-1)
    before_start = jnp.asarray(np.arange(2 * LANES) == LANES, dtype=MXU_DTYPE)
    front = lambda a, rows: jnp.concatenate(
        [jnp.broadcast_to(rows, (N_KV, batch, tm, 2 * LANES)),
         a.reshape(N_KV, batch, seq, 2 * LANES)], axis=2)
    kwx0 = front(jnp.zeros_like(vsx0), before_start)
    vwx0 = front(vsx0, jnp.zeros((), MXU_DTYPE))

    x2 = x.reshape(m, d)
    h = _rmsnorm(x2, norm1_g[0], tm)
    out = None
    for l in range(depth):
        wl = w_in[l]
        o_kv = attn_dim
        kv = [wl[:, o_kv + j * kv_dim:o_kv + (j + 1) * kv_dim] for j in range(6)]
        o_ng = o_kv + 6 * kv_dim
        w_ng = wl[:, o_ng:o_ng + n_gate].reshape(d, N_KV, GROUP * N_NSA_BRANCH)
        w_ng = jnp.pad(w_ng, ((0, 0), (0, 0), (0, LANES - GROUP * N_NSA_BRANCH))).reshape(d, N_KV * LANES)
        w_qkv = cast(jnp.concatenate([wl[:, :attn_dim], kv[2], kv[4], kv[0], kv[1], kv[3], kv[5], w_ng], axis=1))
        o_cv = o_ng + n_gate
        w_x, w_b, w_c = [cast(wl[:, o_cv + j * d:o_cv + (j + 1) * d]) for j in range(3)]
        o_mg = o_cv + 3 * d
        w_ga, w_gc = [cast(wl[:, o_mg + j * d:o_mg + (j + 1) * d]) for j in range(2)]

        qkv, cmp_kv, ksx, vsx, kwx, vwx, gates = _qkv_proj(h, w_qkv, cos_t, sin_t, ksx0, vsx0, kwx0, vwx0,
                                                           seq, tm)
        ksx0, vsx0, kwx0, vwx0 = ksx, vsx, kwx, vwx

        chunks = cmp_kv.reshape(2 * N_KV, batch, ncp, CMP_STRIDE * HEAD_DIM)
        pad_pos = lambda p: cast(jnp.pad(p.reshape(1, CMP_LEN * HEAD_DIM), ((0, 7), (0, 0))))
        kc, vc = _compress(chunks, 0, N_KV, cast(cmp_w1_k[l]), cast(cmp_w2_k[l]), pad_pos(cmp_pos_k[l]),
                           cast(cmp_w1_v[l]), cast(cmp_w2_v[l]), pad_pos(cmp_pos_v[l]), cos_c, sin_c, batch)

        o_attn = _attention(qkv, ksx, vsx, kwx, vwx, kc, vc, gates, mselt, batch, seq, tq, tk)

        cw = jnp.pad(conv_w[l], ((0, 8 - CONV_WIDTH), (0, 0)))
        v_conv = _conv_mixer(h, w_x, w_b, w_c, cw, seq, tm_small, tn_wide)

        merged = _merge(o_attn, v_conv, h, cast(w_attn_proj[l]), cast(w_conv_out[l]), w_ga, w_gc,
                        tm_small, tn_wide)
        x2, h2 = _outproj(merged, cast(w_o[l]), x2, norm2_g[l], tm_small)

        last = l == depth - 1
        g_next = final_g if last else norm1_g[l + 1]
        res = _mlp(h2, cast(w_up[l]), cast(w_down[l]), x2, g_next, tm_small, tf, last)
        if last:
            out = res[0]
        else:
            x2, h = res
    return out.reshape(batch, seq, d)
```

```python
import collections
import functools
import math

import numpy as np
import jax
import jax.numpy as jnp
from jax import lax
from jax.experimental import pallas as pl
from jax.experimental.pallas import tpu as pltpu

N_HEADS = 16
HEAD_DIM = 128
N_KV = 4
GROUP = N_HEADS // N_KV
CMP_LEN = 32
CMP_STRIDE = 16
SEL_LEN = 64
N_SEL = 16
WINDOW = 512
N_NSA_BRANCH = 3
CONV_WIDTH = 3
ROPE_THETA = 10000.0
EPS = 1e-6

MXU_DTYPE = jnp.bfloat16
F32 = jnp.float32
NEG = -1e30
VMEM_LIMIT = 56 * 1024 * 1024
LANES = 128
MXU_WIDTH = 256
PACKED_ROWS = 16
NT_DIMS = (((1,), (1,)), ((), ()))


def _params(*sem):
    return pltpu.CompilerParams(dimension_semantics=sem, vmem_limit_bytes=VMEM_LIMIT)


def _dot(a, b):
    return jnp.dot(a, b, preferred_element_type=F32)


def _rms(x, g):
    return x * lax.rsqrt(jnp.mean(x * x, axis=-1, keepdims=True) + EPS) * g


def _column_parts(n):
    return [slice(c, c + MXU_WIDTH) for c in range(0, n, MXU_WIDTH)]


def _norm_kernel(x_ref, g_ref, o_ref):
    o_ref[...] = _rms(x_ref[...], g_ref[...]).astype(o_ref.dtype)


def _rmsnorm(x2d, g, tm):
    m, d = x2d.shape
    return pl.pallas_call(
        _norm_kernel,
        grid=(m // tm,),
        in_specs=[pl.BlockSpec((tm, d), lambda i: (i, 0)), pl.BlockSpec((1, d), lambda i: (0, 0))],
        out_specs=pl.BlockSpec((tm, d), lambda i: (i, 0)),
        out_shape=jax.ShapeDtypeStruct((m, d), MXU_DTYPE),
        compiler_params=_params("parallel"),
        name="rmsnorm",
    )(x2d, g.reshape(1, d))


QKV_TILES = ("q", "q", "q", "q", "k_sel", "k_win", "k_cmp", "v_cmp", "v_sel", "v_win", "gate")


def _qkv_kernel(h_ref, w_ref, cos_ref, sin_ref, ksx_in, vsx_in, kwx_in, vwx_in,
                o_ref, cmp_ref, ksx_ref, vsx_ref, kwx_ref, vwx_ref, gate_ref, stage, *, q_scale):
    del ksx_in, vsx_in, kwx_in, vwx_in
    j = pl.program_id(1)
    dest = {"q": o_ref, "k_cmp": cmp_ref, "v_cmp": cmp_ref, "k_sel": ksx_ref, "v_sel": vsx_ref,
            "k_win": kwx_ref, "v_win": vwx_ref, "gate": gate_ref}

    def put(ref, rope, scale, chunked, gate):
        h = h_ref[...]
        for cs in _column_parts(w_ref.shape[1]):
            acc = _dot(h, w_ref[:, cs])
            for c in range(MXU_WIDTH // LANES):
                xc = acc[:, c * LANES:(c + 1) * LANES]
                head = cs.start // LANES + c
                if rope:
                    xc = (xc * (cos_ref[...] * scale)
                          + pltpu.roll(xc, HEAD_DIM // 2, 1) * (sin_ref[...] * scale))
                if chunked:
                    stage[...] = xc
                    n_rows = stage.shape[0] // CMP_STRIDE
                    for l in range(CMP_STRIDE):
                        ref[head, :, l * LANES:(l + 1) * LANES] = (
                            stage[pl.ds(l, n_rows, stride=CMP_STRIDE), :].astype(ref.dtype))
                elif gate:
                    ref[head] = jax.nn.sigmoid(xc)
                else:
                    ref[head] = xc.astype(ref.dtype)

    for name in dict.fromkeys(QKV_TILES):
        tiles = [t for t, n in enumerate(QKV_TILES) if n == name]

        @pl.when((j >= tiles[0]) & (j <= tiles[-1]))
        def _():
            put(dest[name], rope=name in ("q", "k_sel", "k_win"), scale=q_scale if name == "q" else 1.0,
                chunked=name in ("k_cmp", "v_cmp"), gate=name == "gate")


def _qkv_proj(h, w, cos, sin, ksx, vsx, kwx, vwx, seq, tm):
    m, d = h.shape
    tn = N_KV * LANES
    assert w.shape[1] == tn * len(QKV_TILES) and seq % tm == 0 and kwx.shape[2] == seq + tm
    tps = seq // tm
    n_q = QKV_TILES.count("q")
    cmp_first = QKV_TILES.index("k_cmp")
    assert QKV_TILES[cmp_first + 1] == "v_cmp" and tm % (PACKED_ROWS * CMP_STRIDE) == 0
    ext_spec = pl.BlockSpec((N_KV, tm, LANES), lambda i, j: (0, i, 0))
    win_spec = pl.BlockSpec((N_KV, None, tm, LANES), lambda i, j: (0, i // tps, 1 + i % tps, 0))
    any_spec = pl.BlockSpec(memory_space=pl.ANY)
    sds = lambda a: jax.ShapeDtypeStruct(a.shape, a.dtype)
    kern = functools.partial(_qkv_kernel, q_scale=HEAD_DIM ** -0.5 * math.log2(math.e))
    return pl.pallas_call(
        kern,
        grid=(m // tm, len(QKV_TILES)),
        in_specs=[pl.BlockSpec((tm, d), lambda i, j: (i, 0)),
                  pl.BlockSpec((d, tn), lambda i, j: (0, j)),
                  pl.BlockSpec((tm, LANES), lambda i, j: (i % tps, 0)),
                  pl.BlockSpec((tm, LANES), lambda i, j: (i % tps, 0)),
                  any_spec, any_spec, any_spec, any_spec],
        out_specs=[pl.BlockSpec((N_KV, tm, LANES), lambda i, j: (jnp.minimum(j, n_q - 1), i, 0)),
                   pl.BlockSpec((N_KV, tm // CMP_STRIDE, CMP_STRIDE * LANES),
                                lambda i, j: (jnp.clip(j - cmp_first, 0, 1), i, 0)),
                   ext_spec, ext_spec, win_spec, win_spec, ext_spec],
        out_shape=[jax.ShapeDtypeStruct((n_q * N_KV, m, LANES), MXU_DTYPE),
                   jax.ShapeDtypeStruct((2 * N_KV, m // CMP_STRIDE, CMP_STRIDE * LANES), MXU_DTYPE),
                   sds(ksx), sds(vsx), sds(kwx), sds(vwx),
                   jax.ShapeDtypeStruct((N_KV, m, LANES), F32)],
        scratch_shapes=[pltpu.VMEM((tm, LANES), F32)],
        input_output_aliases={4: 2, 5: 3, 6: 4, 7: 5},
        compiler_params=_params("parallel", "arbitrary"),
        name="qkv_proj",
    )(h, w, cos, sin, ksx, vsx, kwx, vwx)


def _compress_kernel(ck_ref, cv_ref, w1k_ref, w2k_ref, pk_ref, w1v_ref, w2v_ref, pv_ref,
                     cos_ref, sin_ref, kc_ref, vc_ref):
    nch = ck_ref.shape[0]
    half = w1k_ref.shape[0] // 2

    def phi(c_ref, w1_ref, w2_ref, p_ref):
        c = c_ref[...]
        first = _dot(c, w1_ref[:half, :])
        second = _dot(c, w1_ref[half:, :])
        pos = _dot(p_ref[...], w1_ref[...])[0:1, :]
        hid = first + pltpu.roll(second, nch - 1, 0) + pos
        act = hid * jax.nn.sigmoid(hid)
        return _dot(act.astype(MXU_DTYPE), w2_ref[...])

    kc = phi(ck_ref, w1k_ref, w2k_ref, pk_ref)
    kc = kc * cos_ref[...] + pltpu.roll(kc, HEAD_DIM // 2, 1) * sin_ref[...]
    kc_ref[...] = kc.astype(kc_ref.dtype)
    vc_ref[...] = phi(cv_ref, w1v_ref, w2v_ref, pv_ref).astype(vc_ref.dtype)


def _compress(chunks, kbase, vbase, w1k, w2k, pk, w1v, w2v, pv, cos_c, sin_c, batch):
    _, _, nch, cw = chunks.shape
    const = lambda a: pl.BlockSpec(a.shape, lambda b, g: (0,) * a.ndim)
    out_spec = pl.BlockSpec((None, None, nch, HEAD_DIM), lambda b, g: (b, g, 0, 0))
    out_sds = jax.ShapeDtypeStruct((batch, N_KV, nch, HEAD_DIM), MXU_DTYPE)
    return pl.pallas_call(
        _compress_kernel,
        grid=(batch, N_KV),
        in_specs=[pl.BlockSpec((None, None, nch, cw), lambda b, g: (kbase + g, b, 0, 0)),
                  pl.BlockSpec((None, None, nch, cw), lambda b, g: (vbase + g, b, 0, 0)),
                  const(w1k), const(w2k), const(pk), const(w1v), const(w2v), const(pv),
                  const(cos_c), const(sin_c)],
        out_specs=[out_spec, out_spec],
        out_shape=[out_sds, out_sds],
        compiler_params=_params("parallel", "parallel"),
        name="compress",
    )(chunks, chunks, w1k, w2k, pk, w1v, w2v, pv, cos_c, sin_c)


_Branch = collections.namedtuple("_Branch", "s p m acc bias")


def _attn_kernel(q_ref, kc_ref, vc_ref, ksx_ref, vsx_ref, kw_ref, vwx_ref, gate_ref, mselt_ref, spread_ref,
                 o_ref, qx_scr, qw_scr, sc_scr, pc_scr, sd_scr, pd_scr, sw_scr, pw_scr, sa_scr, sb_scr,
                 pa_scr, pb_scr, bc_scr, bd_scr, bw_scr, mc_scr, mw_scr, m_scr, a_scr,
                 accc_scr, accw_scr, acc_scr, psum_scr, imp_scr, rank_scr, *, tq, tk, rb):
    i = pl.program_id(2)
    q0 = pl.multiple_of(i * tq, tq)
    r = GROUP
    rows = r * tq
    ncp = kc_ref.shape[0]
    ns = mselt_ref.shape[0]
    wlen = WINDOW + tq
    n_chunks = rows // rb
    t_col = q0 + lax.broadcasted_iota(jnp.int32, (tq, 1), 0)
    q = q_ref[...].reshape(rows, HEAD_DIM)
    head = lambda h: slice(h * tq, (h + 1) * tq)
    lane = lambda kk: slice(kk * LANES, (kk + 1) * LANES)

    def load_scores(br, rs, bs, width):
        xs = [br.s[rs, lane(kk)] for kk in range(width // LANES)]
        return [x if b is None else x + b[bs, :] for x, b in zip(xs, br.bias)]

    def softmax_rows(br, width, chunks, first, cmp):
        def slices(c):
            r0 = c * rb
            return slice(r0, r0 + rb), slice(r0 % tq, r0 % tq + rb)

        for c in chunks:
            rs, bs = slices(c)
            xs = load_scores(br, rs, bs, width)
            mx = jnp.max(functools.reduce(jnp.maximum, xs), axis=-1, keepdims=True)
            if first:
                br.m[rs, :] = jnp.broadcast_to(mx, (rb, LANES))
            else:
                m_old = br.m[rs, :]
                m_new = jnp.maximum(m_old, mx)
                a_scr[rs, :] = jnp.exp2(m_old - m_new)
                br.m[rs, :] = m_new

        for c in chunks:
            rs, bs = slices(c)
            xs = load_scores(br, rs, bs, width)
            m = br.m[rs, :]
            ps = [jnp.exp2(x - m) for x in xs]
            if cmp:
                ps = [jnp.where(x > 0.5 * NEG, p, 0.0) for x, p in zip(xs, ps)]
                lsum = jnp.sum(functools.reduce(jnp.add, ps), axis=-1, keepdims=True)
                inv = 1.0 / jnp.where(lsum > 0.0, lsum, 1.0)
                ps = [p * inv for p in ps]
                for kk, p in enumerate(ps):
                    psum_scr[bs, lane(kk)] += p
            for kk, p in enumerate(ps):
                br.p[rs, lane(kk)] = p.astype(br.p.dtype)

    def row_parts(n):
        return [slice(p * rows // n, (p + 1) * rows // n) for p in range(n)]

    def scores(br, width, q_rows, k, parts=2):
        for hs in row_parts(parts):
            br.s[hs, :width] = lax.dot_general(q_rows(hs), k, NT_DIMS, preferred_element_type=F32)

    def weighted_values(br, width, v, *, first, cmp=False, parts=2):
        for hi, hs in enumerate(row_parts(parts)):
            chunks = range(hi * n_chunks // parts, (hi + 1) * n_chunks // parts)
            softmax_rows(br, width, chunks, first, cmp)
            pv = _dot(br.p[hs, :width], v)
            cols = slice(0, v.shape[1])
            if first:
                br.acc[hs, cols] = pv
            else:
                a = a_scr[hs, :]
                for kk in range(v.shape[1] // LANES):
                    br.acc[hs, lane(kk)] = a * br.acc[hs, lane(kk)] + pv[:, lane(kk)]

    q_plain = lambda hs: q[hs]
    q_ext = lambda hs: qx_scr[hs, :]
    row_id = lax.broadcasted_iota(jnp.int32, (tq, LANES), 0)
    col_id = lax.broadcasted_iota(jnp.int32, (tq, LANES), 1)

    n_cl = ncp // LANES
    for kk in range(n_cl):
        cmp_end = (col_id + kk * LANES) * CMP_STRIDE + (CMP_LEN - 1)
        bc_scr[:, lane(kk)] = jnp.where(cmp_end <= t_col, 0.0, NEG)
    n_dl = tq // LANES
    for kk in range(n_dl):
        own = (col_id <= row_id - kk * LANES) & (row_id < (kk + 1) * LANES)
        bd_scr[:, lane(kk)] = jnp.where(own, 0.0, NEG)
    n_wl = wlen // LANES
    for kk in range(n_dl):
        bw_scr[:, lane(kk)] = jnp.where(col_id + kk * LANES > row_id, 0.0, NEG)
        bw_scr[:, lane(n_dl + kk)] = jnp.where(col_id + kk * LANES <= row_id, 0.0, NEG)

    cmp_br = _Branch(sc_scr, pc_scr, mc_scr, accc_scr, [bc_scr.at[:, lane(kk)] for kk in range(n_cl)])
    diag_br = _Branch(sd_scr, pd_scr, m_scr, acc_scr, [bd_scr.at[:, lane(kk)] for kk in range(n_dl)])
    win_bias = ([bw_scr.at[:, lane(kk)] for kk in range(n_dl)] + [None] * (n_wl - 2 * n_dl)
                + [bw_scr.at[:, lane(n_dl + kk)] for kk in range(n_dl)])
    win_br = _Branch(sw_scr, pw_scr, mw_scr, accw_scr, win_bias)
    sel_a = _Branch(sa_scr, pa_scr, m_scr, acc_scr, [None] * (tk // LANES))
    sel_b = _Branch(sb_scr, pb_scr, m_scr, acc_scr, [None] * (tk // LANES))

    psum_scr[...] = jnp.zeros(psum_scr.shape, F32)
    pad_col = jnp.where(lax.broadcasted_iota(jnp.int32, (rows, LANES), 1) == 0, NEG, 0.0)
    qw_scr[:, :LANES] = q
    qw_scr[:, LANES:] = pad_col.astype(qw_scr.dtype)
    scores(cmp_br, ncp, q_plain, kc_ref[...])
    w_rows = pl.ds(q0 + (kw_ref.shape[0] - ksx_ref.shape[0] - WINDOW), wlen)
    scores(win_br, wlen, lambda hs: qw_scr[hs, :], kw_ref[w_rows, :])

    weighted_values(cmp_br, ncp, vc_ref[...], first=True, cmp=True)
    scores(diag_br, tq, q_plain, ksx_ref[pl.ds(q0, tq), :LANES])
    mselt = mselt_ref[...]
    imp = jnp.zeros((ns, tq), F32)
    rem = psum_scr[...]
    for _ in range(3):
        piece = rem.astype(MXU_DTYPE)
        imp = imp + lax.dot_general(mselt, piece, NT_DIMS, preferred_element_type=F32)
        rem = rem - piece.astype(F32)
    blk = lax.broadcasted_iota(jnp.int32, (ns, tq), 0)
    tb = (q0 + lax.broadcasted_iota(jnp.int32, (ns, tq), 1)) // SEL_LEN
    forced = (blk == 0) | (blk == tb) | (blk == tb - 1)
    imp_scr[...] = jnp.where(blk <= tb, jnp.where(forced, jnp.inf, imp), -jnp.inf)
    rank_scr[...] = jnp.zeros(rank_scr.shape, F32)

    weighted_values(win_br, wlen, vwx_ref[w_rows, :], first=True)
    weighted_values(diag_br, tq, vsx_ref[pl.ds(q0, tq), :], first=True)

    sub = 8
    n_grp = ns // sub
    sub_id = lax.broadcasted_iota(jnp.int32, (sub, tq), 0)
    grp = lambda v: slice(sub * v, sub * (v + 1))
    for gm in range(n_grp):
        @pl.when(sub * gm * SEL_LEN <= q0 + tq - 1)
        def _():
            xs = [imp_scr[grp(v), :] for v in range(n_grp)]
            ranks = [rank_scr[grp(v), :] for v in range(n_grp)]
            for mp in range(sub * gm, sub * (gm + 1)):
                row = jnp.broadcast_to(xs[gm][mp % sub:mp % sub + 1, :], (sub, tq))
                for v, x in enumerate(xs):
                    if sub * v > mp:
                        ahead = row >= x
                    elif sub * v + sub - 1 <= mp:
                        ahead = row > x
                    else:
                        ahead = (row > x) | ((row == x) & (sub_id > mp % sub))
                    ranks[v] = ranks[v] + jnp.where(ahead, 1.0, 0.0)
            for v in range(n_grp):
                rank_scr[grp(v), :] = ranks[v]

    own_start = (q0 + lax.broadcasted_iota(jnp.int32, (ns, tq), 1)) // LANES * (LANES // SEL_LEN)
    off = jnp.where((rank_scr[...] < float(min(N_SEL, ns))) & (blk < own_start), 0.0, NEG)
    off = off.T.astype(MXU_DTYPE)
    if ns < LANES:
        off = jnp.concatenate([off, jnp.zeros((tq, LANES - ns), MXU_DTYPE)], axis=1)
    for h in range(r):
        qx_scr[head(h), :LANES] = q_ref[h]
        qx_scr[head(h), LANES:] = off

    last_tile = ksx_ref.shape[0] // tk - 1

    def tile_rows(t):
        return pl.ds(pl.multiple_of(jnp.minimum(t, last_tile) * tk, tk), tk)

    def sel_pair(jj, carry):
        t = 2 * jj
        scores(sel_b, tk, q_ext, ksx_ref[tile_rows(t + 1), :], parts=1)
        weighted_values(sel_a, tk, vsx_ref[tile_rows(t), :], first=False, parts=1)
        scores(sel_a, tk, q_ext, ksx_ref[tile_rows(t + 2), :], parts=1)
        weighted_values(sel_b, tk, vsx_ref[tile_rows(t + 1), :], first=False, parts=1)
        return carry

    n_tiles = (q0 + tq - LANES + tk - 1) // tk
    scores(sel_a, tk, q_ext, ksx_ref[tile_rows(0), :])
    lax.fori_loop(0, n_tiles // 2, sel_pair, 0)

    @pl.when(n_tiles % 2 == 1)
    def _():
        weighted_values(sel_a, tk, vsx_ref[tile_rows(n_tiles - 1), :], first=False, parts=1)

    n_gates = r * N_NSA_BRANCH
    half_g = n_gates // 2
    gb = gate_ref[...].astype(MXU_DTYPE)
    wide = [_dot(gb, spread_ref[:, :half_g * LANES]), _dot(gb, spread_ref[:, half_g * LANES:])]

    def gate(h, br):
        c = h * N_NSA_BRANCH + br
        return wide[c // half_g][:, lane(c % half_g)]

    o_sel = acc_scr[:, :LANES] * (1.0 / acc_scr[:, LANES:])
    o_win = accw_scr[:, :LANES] * (1.0 / accw_scr[:, LANES:])
    for h in range(r):
        o = (gate(h, 0) * accc_scr[head(h), :] + gate(h, 1) * o_sel[head(h), :]
             + gate(h, 2) * o_win[head(h), :])
        o_ref[:, h * HEAD_DIM:(h + 1) * HEAD_DIM] = o.astype(o_ref.dtype)


def _attention(q, ksx, vsx, kwp, vwx, kc, vc, gates, mselt, batch, seq, tq, tk):
    nq = seq // tq
    ncp = kc.shape[2]
    ns = mselt.shape[0]
    assert ns <= LANES and seq % (2 * tk) == 0 and tk % tq == 0 and WINDOW % tq == 0
    assert tq % LANES == 0 and ncp % LANES == 0 and kwp.shape[2] >= seq + WINDOW
    rows = GROUP * tq
    wlen = WINDOW + tq
    ext_spec = pl.BlockSpec((None, seq, 2 * LANES), lambda b, g, i: (g, b, 0))
    cmp_spec = pl.BlockSpec((None, None, ncp, HEAD_DIM), lambda b, g, i: (b, g, 0, 0))
    f32 = lambda *shape: pltpu.VMEM(shape, F32)
    mxu = lambda *shape: pltpu.VMEM(shape, MXU_DTYPE)
    n_gates = GROUP * N_NSA_BRANCH
    spread = jnp.asarray(np.arange(LANES)[:, None] == np.arange(n_gates * LANES)[None, :] // LANES,
                         dtype=MXU_DTYPE)
    return pl.pallas_call(
        functools.partial(_attn_kernel, tq=tq, tk=tk, rb=PACKED_ROWS),
        grid=(batch, N_KV, nq),
        in_specs=[pl.BlockSpec((GROUP, tq, HEAD_DIM), lambda b, g, i: (g, b * nq + i, 0)),
                  cmp_spec, cmp_spec, ext_spec, ext_spec,
                  pl.BlockSpec((None, None) + kwp.shape[2:], lambda b, g, i: (g, b, 0, 0)),
                  pl.BlockSpec((None, None) + vwx.shape[2:], lambda b, g, i: (g, b, 0, 0)),
                  pl.BlockSpec((None, tq, LANES), lambda b, g, i: (g, b * nq + i, 0)),
                  pl.BlockSpec((ns, ncp), lambda b, g, i: (0, 0)),
                  pl.BlockSpec(spread.shape, lambda b, g, i: (0, 0))],
        out_specs=pl.BlockSpec((tq, GROUP * HEAD_DIM), lambda b, g, i: (b * nq + i, g)),
        out_shape=jax.ShapeDtypeStruct((batch * seq, N_HEADS * HEAD_DIM), MXU_DTYPE),
        scratch_shapes=[mxu(rows, 2 * LANES), mxu(rows, 2 * LANES),
                        f32(rows, ncp), mxu(rows, ncp),
                        f32(rows, tq), mxu(rows, tq),
                        f32(rows, wlen), mxu(rows, wlen),
                        f32(rows, tk), f32(rows, tk),
                        mxu(rows, tk), mxu(rows, tk),
                        f32(tq, ncp), f32(tq, tq), f32(tq, 2 * tq),
                        f32(rows, LANES), f32(rows, LANES), f32(rows, LANES),
                        f32(rows, LANES),
                        f32(rows, LANES),
                        f32(rows, 2 * LANES), f32(rows, 2 * LANES),
                        f32(tq, ncp),
                        f32(ns, tq), f32(ns, tq)],
        compiler_params=_params("parallel", "parallel", "arbitrary"),
        name="nsa_attention",
    )(q, kc, vc, ksx, vsx, kwp, vwx, gates, mselt, spread)


def _conv_kernel(h_ref, wx_ref, wb_ref, wc_ref, cw_ref, o_ref, ubuf, *, tiles_per_seq):
    i = pl.program_id(1)
    tm = h_ref.shape[0]
    h = h_ref[...]

    @pl.when(i % tiles_per_seq == 0)
    def _():
        ubuf[0:8, :] = jnp.zeros((8, ubuf.shape[1]), F32)

    w = cw_ref[...]
    for cs in _column_parts(o_ref.shape[1]):
        x_in = _dot(h, wx_ref[:, cs])
        gate_b = _dot(h, wb_ref[:, cs])
        gate_c = _dot(h, wc_ref[:, cs])
        ubuf[8:tm + 8, cs] = gate_c * x_in
        conv = (w[2:3, cs] * ubuf[8:tm + 8, cs] + w[1:2, cs] * ubuf[7:tm + 7, cs]
                + w[0:1, cs] * ubuf[6:tm + 6, cs])
        o_ref[:, cs] = (gate_b * conv).astype(o_ref.dtype)
        ubuf[0:8, cs] = ubuf[tm:tm + 8, cs]


def _conv_mixer(h, wx, wb, wc, cw, seq, tm, tn):
    m, d = h.shape
    n = wx.shape[1]
    wspec = pl.BlockSpec((d, tn), lambda j, i: (0, j))
    return pl.pallas_call(
        functools.partial(_conv_kernel, tiles_per_seq=seq // tm),
        grid=(n // tn, m // tm),
        in_specs=[pl.BlockSpec((tm, d), lambda j, i: (i, 0)), wspec, wspec, wspec,
                  pl.BlockSpec((8, tn), lambda j, i: (0, j))],
        out_specs=pl.BlockSpec((tm, tn), lambda j, i: (i, j)),
        out_shape=jax.ShapeDtypeStruct((m, n), MXU_DTYPE),
        scratch_shapes=[pltpu.VMEM((tm + 8, tn), F32)],
        compiler_params=_params("parallel", "arbitrary"),
        name="conv_mixer",
    )(h, wx, wb, wc, cw)


def _merge_kernel(oa_ref, v_ref, h_ref, wap_ref, wco_ref, wga_ref, wgc_ref, o_ref):
    h = h_ref[...]
    oa = oa_ref[...]
    v = v_ref[...]
    for cs in _column_parts(o_ref.shape[1]):
        y_attn = _dot(oa, wap_ref[:, cs])
        y_conv = _dot(v, wco_ref[:, cs])
        g_attn = jax.nn.sigmoid(_dot(h, wga_ref[:, cs]))
        g_conv = jax.nn.sigmoid(_dot(h, wgc_ref[:, cs]))
        o_ref[:, cs] = (g_attn * y_attn + g_conv * y_conv).astype(o_ref.dtype)


def _merge(oa, v, h, wap, wco, wga, wgc, tm, tn):
    m, d = h.shape
    n = wap.shape[1]
    aspec = pl.BlockSpec((tm, d), lambda i, j: (i, 0))
    wspec = pl.BlockSpec((d, tn), lambda i, j: (0, j))
    return pl.pallas_call(
        _merge_kernel,
        grid=(m // tm, n // tn),
        in_specs=[aspec, aspec, aspec, wspec, wspec, wspec, wspec],
        out_specs=pl.BlockSpec((tm, tn), lambda i, j: (i, j)),
        out_shape=jax.ShapeDtypeStruct((m, n), MXU_DTYPE),
        compiler_params=_params("parallel", "arbitrary"),
        name="gated_merge",
    )(oa, v, h, wap, wco, wga, wgc)


def _outproj_kernel(a_ref, w_ref, x_ref, g_ref, xo_ref, ho_ref):
    x = x_ref[...] + _dot(a_ref[...], w_ref[...])
    xo_ref[...] = x
    ho_ref[...] = _rms(x, g_ref[...]).astype(ho_ref.dtype)


def _outproj(a, w, x, g, tm):
    m, d = x.shape
    row = pl.BlockSpec((tm, d), lambda i: (i, 0))
    return pl.pallas_call(
        _outproj_kernel,
        grid=(m // tm,),
        in_specs=[row, pl.BlockSpec((d, d), lambda i: (0, 0)), row,
                  pl.BlockSpec((1, d), lambda i: (0, 0))],
        out_specs=[row, row],
        out_shape=[jax.ShapeDtypeStruct((m, d), F32), jax.ShapeDtypeStruct((m, d), MXU_DTYPE)],
        compiler_params=_params("parallel"),
        name="out_proj",
    )(a, w, x, g.reshape(1, d))


def _mlp_kernel(h_ref, wu_ref, wd_ref, x_ref, g_ref, *refs, last):
    acc = refs[-1]
    f = pl.program_id(1)

    @pl.when(f == 0)
    def _():
        acc[...] = x_ref[...]

    a = jnp.maximum(_dot(h_ref[...], wu_ref[...]), 0.0)
    acc[...] += _dot((a * a).astype(MXU_DTYPE), wd_ref[...])

    @pl.when(f == pl.num_programs(1) - 1)
    def _():
        x = acc[...]
        normed = _rms(x, g_ref[...])
        if last:
            refs[0][...] = normed
        else:
            refs[0][...] = x
            refs[1][...] = normed.astype(refs[1].dtype)


def _mlp(h, wu, wd, x, g, tm, tf, last):
    m, d = x.shape
    ff = wu.shape[1]
    row = pl.BlockSpec((tm, d), lambda i, f: (i, 0))
    if last:
        out_specs, out_shape = [row], [jax.ShapeDtypeStruct((m, d), F32)]
    else:
        out_specs = [row, row]
        out_shape = [jax.ShapeDtypeStruct((m, d), F32), jax.ShapeDtypeStruct((m, d), MXU_DTYPE)]
    return pl.pallas_call(
        functools.partial(_mlp_kernel, last=last),
        grid=(m // tm, ff // tf),
        in_specs=[row, pl.BlockSpec((d, tf), lambda i, f: (0, f)),
                  pl.BlockSpec((tf, d), lambda i, f: (f, 0)), row,
                  pl.BlockSpec((1, d), lambda i, f: (0, 0))],
        out_specs=out_specs,
        out_shape=out_shape,
        scratch_shapes=[pltpu.VMEM((tm, d), F32)],
        compiler_params=_params("parallel", "arbitrary"),
        name="relu2_mlp",
    )(h, wu, wd, x, g.reshape(1, d))


def _rope_tables(pos):
    half = HEAD_DIM // 2
    inv_freq = jnp.exp(-math.log(ROPE_THETA) * jnp.arange(half, dtype=F32) / half)
    ang = pos.astype(F32)[:, None] * inv_freq[None, :]
    cos, sin = jnp.cos(ang), jnp.sin(ang)
    return jnp.concatenate([cos, cos], axis=-1), jnp.concatenate([-sin, sin], axis=-1)


def _cmp_to_sel_t(ncp, ns):
    nc = ncp - 1
    cs = np.arange(nc) * CMP_STRIDE
    ss = np.arange(ns) * SEL_LEN
    ov = np.minimum(cs[:, None] + CMP_LEN, ss[None, :] + SEL_LEN) - np.maximum(cs[:, None], ss[None, :])
    m = np.zeros((ncp, ns), np.float32)
    m[:nc] = np.clip(ov, 0, None) / CMP_LEN
    return jnp.asarray(m.T, dtype=MXU_DTYPE)


def _block_onehot(seq):
    e = (np.arange(seq)[:, None] // SEL_LEN == np.arange(LANES)[None, :]).astype(np.float32)
    return jnp.asarray(e, dtype=MXU_DTYPE)


def kernel(x, norm1_g, w_in, cmp_pos_k, cmp_w1_k, cmp_w2_k, cmp_pos_v, cmp_w1_v, cmp_w2_v,
           conv_w, w_attn_proj, w_conv_out, w_o, norm2_g, w_up, w_down, final_g):
    batch, seq, d = x.shape
    depth = w_in.shape[0]
    m = batch * seq
    attn_dim = N_HEADS * HEAD_DIM
    kv_dim = N_KV * HEAD_DIM
    n_gate = N_HEADS * N_NSA_BRANCH
    ncp = seq // CMP_STRIDE
    ns = seq // SEL_LEN
    cast = lambda a: a.astype(MXU_DTYPE)

    tm = min(1024, seq)
    tm_small = min(512, seq)
    tn_wide = 1024
    tf = 1024
    tq = 256
    tk = 1024

    cos_t, sin_t = _rope_tables(jnp.arange(seq))
    cos_c, sin_c = _rope_tables(jnp.arange(ncp) * CMP_STRIDE + CMP_LEN - 1)
    mselt = _cmp_to_sel_t(ncp, ns)
    onehot = jnp.broadcast_to(jnp.tile(_block_onehot(seq), (batch, 1))[None], (N_KV, m, LANES))
    zeros = jnp.zeros((N_KV, m, LANES), MXU_DTYPE)
    ksx0 = jnp.concatenate([zeros, onehot], axis=-1)
    vsx0 = jnp.concatenate([zeros, jnp.ones_like(zeros)], axis=-1)
    before_start = jnp.asarray(np.arange(2 * LANES) == LANES, dtype=MXU_DTYPE)
    front = lambda a, rows: jnp.concatenate(
        [jnp.broadcast_to(rows, (N_KV, batch, tm, 2 * LANES)),
         a.reshape(N_KV, batch, seq, 2 * LANES)], axis=2)
    kwx0 = front(jnp.zeros_like(vsx0), before_start)
    vwx0 = front(vsx0, jnp.zeros((), MXU_DTYPE))

    x2 = x.reshape(m, d)
    h = _rmsnorm(x2, norm1_g[0], tm)
    out = None
    for l in range(depth):
        wl = w_in[l]
        o_kv = attn_dim
        kv = [wl[:, o_kv + j * kv_dim:o_kv + (j + 1) * kv_dim] for j in range(6)]
        o_ng = o_kv + 6 * kv_dim
        w_ng = wl[:, o_ng:o_ng + n_gate].reshape(d, N_KV, GROUP * N_NSA_BRANCH)
        w_ng = jnp.pad(w_ng, ((0, 0), (0, 0), (0, LANES - GROUP * N_NSA_BRANCH))).reshape(d, N_KV * LANES)
        w_qkv = cast(jnp.concatenate([wl[:, :attn_dim], kv[2], kv[4], kv[0], kv[1], kv[3], kv[5], w_ng], axis=1))
        o_cv = o_ng + n_gate
        w_x, w_b, w_c = [cast(wl[:, o_cv + j * d:o_cv + (j + 1) * d]) for j in range(3)]
        o_mg = o_cv + 3 * d
        w_ga, w_gc = [cast(wl[:, o_mg + j * d:o_mg + (j + 1) * d]) for j in range(2)]

        qkv, cmp_kv, ksx, vsx, kwx, vwx, gates = _qkv_proj(h, w_qkv, cos_t, sin_t, ksx0, vsx0, kwx0, vwx0,
                                                           seq, tm)
        ksx0, vsx0, kwx0, vwx0 = ksx, vsx, kwx, vwx

        chunks = cmp_kv.reshape(2 * N_KV, batch, ncp, CMP_STRIDE * HEAD_DIM)
        pad_pos = lambda p: cast(jnp.pad(p.reshape(1, CMP_LEN * HEAD_DIM), ((0, 7), (0, 0))))
        kc, vc = _compress(chunks, 0, N_KV, cast(cmp_w1_k[l]), cast(cmp_w2_k[l]), pad_pos(cmp_pos_k[l]),
                           cast(cmp_w1_v[l]), cast(cmp_w2_v[l]), pad_pos(cmp_pos_v[l]), cos_c, sin_c, batch)

        o_attn = _attention(qkv, ksx, vsx, kwx, vwx, kc, vc, gates, mselt, batch, seq, tq, tk)

        cw = jnp.pad(conv_w[l], ((0, 8 - CONV_WIDTH), (0, 0)))
        v_conv = _conv_mixer(h, w_x, w_b, w_c, cw, seq, tm_small, tn_wide)

        merged = _merge(o_attn, v_conv, h, cast(w_attn_proj[l]), cast(w_conv_out[l]), w_ga, w_gc,
                        tm_small, tn_wide)
        x2, h2 = _outproj(merged, cast(w_o[l]), x2, norm2_g[l], tm_small)

        last = l == depth - 1
        g_next = final_g if last else norm1_g[l + 1]
        res = _mlp(h2, cast(w_up[l]), cast(w_down[l]), x2, g_next, tm_small, tf, last)
        if last:
            out = res[0]
        else:
            x2, h = res
    return out.reshape(batch, seq, d)
```
